```python
import math
import jax, jax.numpy as jnp
from jax import lax
import numpy as np

D_MODEL = 1024
BATCH = 8
SEQ = 2048
DEPTH = 2
DEC_BATCH = 128
DEC_SEQ = 8
PAST_LEN = 16384
PAGE_SIZE = 128

N_MIXERS = 4
NH = 4
GROUP_WIDTH = D_MODEL // N_MIXERS
HEAD_DIM = GROUP_WIDTH // NH
MIX_WIDTH = N_MIXERS * GROUP_WIDTH
D_FF = 4 * D_MODEL
EPS = 1e-6

RWKV_R_W = 32
RWKV_R_A = 32
RWKV_R_G = 64
RWKV_GN_EPS = 64e-5
GLA_R = 16
GLA_CHUNK = 16
GLA_GATE_NORM = 16.0
DN_CONV = 4
DN_CHUNK = 64
SSM_STATE = 128
SSM_GROUPS = 2
SSM_CONV = 4
SSM_CHUNK = 64
SSM_XBC = GROUP_WIDTH + 2 * SSM_GROUPS * SSM_STATE

RWKV_SIZES = (GROUP_WIDTH, GROUP_WIDTH, GROUP_WIDTH, RWKV_R_W, RWKV_R_A, RWKV_R_G)
GLA_SIZES = (GROUP_WIDTH, GROUP_WIDTH, GROUP_WIDTH, GROUP_WIDTH, GLA_R)
DN_SIZES = (3 * GROUP_WIDTH, GROUP_WIDTH, NH, NH)
SSM_SIZES = (GROUP_WIDTH, SSM_XBC, NH)
RWKV_COLS = sum(RWKV_SIZES)
GLA_COLS = sum(GLA_SIZES)
DN_COLS = sum(DN_SIZES)
SSM_COLS = sum(SSM_SIZES)
IN_SIZES = (RWKV_COLS, GLA_COLS, DN_COLS, SSM_COLS)
P_TOTAL = sum(IN_SIZES)

kernel_name = "hybrid_rwkv7_gla_gdn_ssd_step"

F32 = jnp.float32


def split_last(x, sizes):
    offs, acc = [], 0
    for s in sizes[:-1]:
        acc += s
        offs.append(acc)
    return jnp.split(x, offs, axis=-1)


def heads(t):
    return t.reshape(t.shape[:-1] + (NH, HEAD_DIM))


def rmsnorm(x, w):
    xf = x.astype(F32)
    return (xf * lax.rsqrt(jnp.mean(xf * xf, -1, keepdims=True) + EPS)).astype(x.dtype) * w


def group_rmsnorm(y, w, n_groups):
    shp = y.shape
    yg = y.reshape(shp[:-1] + (n_groups, shp[-1] // n_groups)).astype(F32)
    yg = yg * lax.rsqrt(jnp.mean(yg * yg, -1, keepdims=True) + EPS)
    return yg.reshape(shp) * w


def l2norm(t):
    return t * lax.rsqrt(jnp.sum(t * t, -1, keepdims=True) + EPS)


def causal_conv(x, prev, w):
    K, T = w.shape[0], x.shape[1]
    xp = jnp.concatenate([prev, x], axis=1)
    y = sum(xp[:, i:i + T] * w[i] for i in range(K))
    return y, xp[:, T:]


def to_chunks(t, C):
    B, T, H = t.shape[:3]
    t = t.reshape((B, T // C, C, H) + t.shape[3:])
    return jnp.moveaxis(t, (1, 3), (0, 2))


def from_chunks(t):
    t = jnp.moveaxis(t, (0, 2), (1, 3))
    return t.reshape((t.shape[0], t.shape[1] * t.shape[2]) + t.shape[3:])


def seg_decay(cum):
    C = cum.shape[-1]
    mask = jnp.tril(jnp.ones((C, C), bool))
    diff = cum[..., :, None] - cum[..., None, :]
    return jnp.where(mask, jnp.exp(jnp.where(mask, diff, 0.0)), 0.0)


def rwkv7_mix(u, shift0, S0, mu, w0, w2, a0, a2, g2, k_k, k_a, r_k, ln_w, ln_b):
    odt = u.dtype
    u = u.astype(F32)
    B, T, _ = u.shape
    u_prev = jnp.concatenate([shift0.astype(F32)[:, None], u[:, :-1]], axis=1)
    xs = u + (u_prev - u) * mu
    r, k, v, xw, xa, xg = split_last(xs, RWKV_SIZES)
    log_w = -jax.nn.softplus(-(w0 + jnp.tanh(xw) @ w2)) - 0.5
    decay = jnp.exp(-jnp.exp(log_w))
    a = jax.nn.sigmoid(a0 + xa @ a2)
    g = jax.nn.sigmoid(xg) @ g2
    kk = l2norm(heads(k * k_k))
    k = k * (1 + (a - 1) * k_a)
    r, k, v, decay, a = (heads(t) for t in (r, k, v, decay, a))

    def step(S, inp):
        r_t, w_t, k_t, v_t, kk_t, a_t = inp
        Skk = jnp.einsum('bhvk,bhk->bhv', S, kk_t)
        S = (S * w_t[:, :, None, :] - Skk[..., None] * (kk_t * a_t)[:, :, None, :]
             + v_t[..., None] * k_t[:, :, None, :])
        return S, jnp.einsum('bhvk,bhk->bhv', S, r_t)

    seq = tuple(jnp.swapaxes(t, 0, 1) for t in (r, decay, k, v, kk, a))
    S, y = lax.scan(step, S0.astype(F32), seq)
    y = jnp.swapaxes(y, 0, 1)
    m = jnp.mean(y, -1, keepdims=True)
    var = jnp.mean(jnp.square(y - m), -1, keepdims=True)
    y = (y - m) * lax.rsqrt(var + RWKV_GN_EPS) * ln_w.reshape(NH, HEAD_DIM) + ln_b.reshape(NH, HEAD_DIM)
    y = y + jnp.sum(r * k * r_k.reshape(NH, HEAD_DIM), -1, keepdims=True) * v
    y = y.reshape(B, T, GROUP_WIDTH) * g
    return y.astype(odt), u[:, -1].astype(odt), S.astype(odt)


def gla_chunked(q, k, v, la, S0):
    C = math.gcd(q.shape[1], GLA_CHUNK)
    q, k, v, la = (to_chunks(t, C) for t in (q, k, v, la))
    b = jnp.cumsum(la, -2)
    last = b[..., -1:, :]
    q_in, k_in, k_out = q * jnp.exp(b), k * jnp.exp(-b), k * jnp.exp(last - b)
    causal = jnp.tril(jnp.ones((C, C), bool))
    A = jnp.where(causal, jnp.einsum('nbhid,nbhjd->nbhij', q_in, k_in), 0.0)
    o_intra = A @ v

    def step(S, inp):
        q_c, k_c, v_c, last_c, o_c = inp
        o = o_c + q_c @ S
        S = S * jnp.exp(last_c)[..., 0, :, None] + jnp.einsum('bhjd,bhjv->bhdv', k_c, v_c)
        return S, o

    S, o = lax.scan(step, S0, (q_in, k_out, v, last, o_intra))
    return from_chunks(o), S


def gla_mix(u, S0, gk_w2, gk_b, norm_w):
    odt = u.dtype
    u = u.astype(F32)
    B, T, _ = u.shape
    q, k, v, g, gl = split_last(u, GLA_SIZES)
    la = jax.nn.log_sigmoid(gl @ gk_w2 + gk_b) / GLA_GATE_NORM
    o, S = gla_chunked(heads(q) * HEAD_DIM ** -0.5, heads(k), heads(v), heads(la), S0.astype(F32))
    o = group_rmsnorm(o.reshape(B, T, GROUP_WIDTH), norm_w, NH) * jax.nn.silu(g)
    return o.astype(odt), S.astype(odt)


def gated_delta_chunked(q, k, v, g, beta, S0):
    C = math.gcd(q.shape[1], DN_CHUNK)
    q, k, v = (to_chunks(t, C) for t in (q, k, v))
    g, beta = to_chunks(g, C), to_chunks(beta, C)
    cum = jnp.cumsum(g, -1)
    L = seg_decay(cum)
    kb = k * beta[..., None]
    eye = jnp.eye(C, dtype=q.dtype)
    strict = jnp.tril(jnp.ones((C, C), bool), -1)
    M = jnp.where(strict, jnp.einsum('nbhid,nbhjd->nbhij', kb, k) * L, 0.0)
    Tm = lax.linalg.triangular_solve(eye + M, jnp.broadcast_to(eye, M.shape),
                                     left_side=True, lower=True, unit_diagonal=True)
    u = Tm @ (v * beta[..., None])
    w = Tm @ (kb * jnp.exp(cum)[..., None])
    A = jnp.einsum('nbhid,nbhjd->nbhij', q, k) * L

    def step(S, inp):
        q_c, k_c, u_c, w_c, A_c, cum_c = inp
        v_new = u_c - w_c @ S
        o = (q_c * jnp.exp(cum_c)[..., None]) @ S + A_c @ v_new
        last = cum_c[..., -1]
        S = (S * jnp.exp(last)[..., None, None]
             + jnp.einsum('bhjd,bhjv->bhdv', k_c * jnp.exp(last[..., None] - cum_c)[..., None], v_new))
        return S, o

    S, o = lax.scan(step, S0, (q, k, u, w, A, cum))
    return from_chunks(o), S


def deltanet_mix(u, conv0, S0, conv_w, A_log, dt_bias, norm_w):
    odt = u.dtype
    u = u.astype(F32)
    B, T, _ = u.shape
    qkv, z, a_in, b_in = split_last(u, DN_SIZES)
    qkv, conv1 = causal_conv(qkv, conv0.astype(F32), conv_w)
    q, k, v = split_last(jax.nn.silu(qkv), (GROUP_WIDTH,) * 3)
    q = l2norm(heads(q)) * HEAD_DIM ** -0.5
    k = l2norm(heads(k))
    beta = jax.nn.sigmoid(b_in)
    g = -jnp.exp(A_log) * jax.nn.softplus(a_in + dt_bias)
    o, S = gated_delta_chunked(q, k, heads(v), g, beta, S0.astype(F32))
    o = group_rmsnorm(o.reshape(B, T, GROUP_WIDTH), norm_w, NH) * jax.nn.silu(z)
    return o.astype(odt), conv1.astype(odt), S.astype(odt)


def ssd_chunked(xdt, la, Bh, Ch, S0):
    C = math.gcd(xdt.shape[1], SSM_CHUNK)
    xdt, Bh, Ch = (to_chunks(t, C) for t in (xdt, Bh, Ch))
    cum = jnp.cumsum(to_chunks(la, C), -1)
    L = seg_decay(cum)
    y_intra = (jnp.einsum('nbhis,nbhjs->nbhij', Ch, Bh) * L) @ xdt

    def step(S, inp):
        x_c, B_c, C_c, cum_c, y_c = inp
        last = cum_c[..., -1]
        y = y_c + jnp.exp(cum_c)[..., None] * jnp.einsum('bhis,bhps->bhip', C_c, S)
        S = (S * jnp.exp(last)[..., None, None]
             + jnp.einsum('bhjp,bhjs->bhps', x_c * jnp.exp(last[..., None] - cum_c)[..., None], B_c))
        return S, y

    S, y = lax.scan(step, S0, (xdt, Bh, Ch, cum, y_intra))
    return from_chunks(y), S


def ssd_mix(u, conv0, S0, conv_w, conv_b, dt_bias, A_log, D_skip, norm_w):
    odt = u.dtype
    u = u.astype(F32)
    B, T, _ = u.shape
    z, xbc, dt_raw = split_last(u, SSM_SIZES)
    xbc, conv1 = causal_conv(xbc, conv0.astype(F32), conv_w)
    xs, Bm, Cm = split_last(jax.nn.silu(xbc + conv_b),
                            (GROUP_WIDTH, SSM_GROUPS * SSM_STATE, SSM_GROUPS * SSM_STATE))
    rep = NH // SSM_GROUPS
    Bh = jnp.repeat(Bm.reshape(B, T, SSM_GROUPS, SSM_STATE), rep, axis=2)
    Ch = jnp.repeat(Cm.reshape(B, T, SSM_GROUPS, SSM_STATE), rep, axis=2)
    dt = jax.nn.softplus(dt_raw + dt_bias)
    la = dt * -jnp.exp(A_log)
    xh = heads(xs)
    y, S = ssd_chunked(xh * dt[..., None], la, Bh, Ch, S0.astype(F32))
    y = y + D_skip[:, None] * xh
    y = group_rmsnorm(y.reshape(B, T, GROUP_WIDTH) * jax.nn.silu(z), norm_w, SSM_GROUPS)
    return y.astype(odt), conv1.astype(odt), S.astype(odt)


def layer(x, c, states, p):
    shift0, wkv0, gla0, dnconv0, dn0, ssmconv0, ssm0 = states
    mod = jnp.einsum('bd,de->be', jax.nn.silu(c), p['ada_w']) + p['ada_b']
    sh1, sc1, gt1, sh2, sc2, gt2 = (m[:, None, :] for m in jnp.split(mod, 6, axis=-1))
    h = rmsnorm(x, p['norm1_w']) * (1 + sc1) + sh1
    u = jnp.einsum('btd,de->bte', h, p['w_in'])
    u_a, u_b, u_c, u_d = split_last(u, IN_SIZES)
    y_a, shift1, wkv1 = rwkv7_mix(u_a, shift0, wkv0, p['rwkv_mu'], p['rwkv_w0'], p['rwkv_w2'],
                                  p['rwkv_a0'], p['rwkv_a2'], p['rwkv_g2'], p['rwkv_k_k'],
                                  p['rwkv_k_a'], p['rwkv_r_k'], p['rwkv_ln_w'], p['rwkv_ln_b'])
    y_b, gla1 = gla_mix(u_b, gla0, p['gla_gk_w2'], p['gla_gk_b'], p['gla_norm_w'])
    y_c, dnconv1, dn1 = deltanet_mix(u_c, dnconv0, dn0, p['dn_conv_w'], p['dn_A_log'],
                                     p['dn_dt_bias'], p['dn_norm_w'])
    y_d, ssmconv1, ssm1 = ssd_mix(u_d, ssmconv0, ssm0, p['ssm_conv_w'], p['ssm_conv_b'],
                                  p['ssm_dt_bias'], p['ssm_A_log'], p['ssm_D'], p['ssm_norm_w'])
    y = jnp.concatenate([y_a, y_b, y_c, y_d], -1) @ p['w_out']
    x = x + gt1 * y
    h = rmsnorm(x, p['norm2_w']) * (1 + sc2) + sh2
    f = jnp.square(jax.nn.relu(h @ p['w_up'])) @ p['w_down']
    x = x + gt2 * f
    return x, (shift1, wkv1, gla1, dnconv1, dn1, ssmconv1, ssm1)


def trunk(x, c, states, P, final_norm_w):
    new = []
    for l in range(DEPTH):
        x, s = layer(x, c, tuple(st[l] for st in states), {n: w[l] for n, w in P.items()})
        new.append(s)
    y = rmsnorm(x, final_norm_w)
    return y, tuple(jnp.stack([s[i] for s in new]) for i in range(len(states)))


def setup_inputs(seed: int = 0) -> dict:
    key = jax.random.key(seed)
    ks = iter(jax.random.split(key, 64))
    def nrm(shape, s=1.0):
        return s * jax.random.normal(next(ks), shape, F32)
    def unif(shape, lo, hi):
        return jax.random.uniform(next(ks), shape, F32, lo, hi)
    L, GW = DEPTH, GROUP_WIDTH
    dn_dt = jnp.exp(unif((L, NH), math.log(1e-3), math.log(1e-1)))
    ssm_dt = jnp.exp(unif((L, NH), math.log(1e-3), math.log(1e-1)))
    inp = {}
    inp['x_prompt'] = nrm((BATCH, SEQ, D_MODEL))
    inp['x_sample'] = nrm((DEC_BATCH, DEC_SEQ, D_MODEL))
    inp['state_rwkv_shift'] = nrm((L, DEC_BATCH, RWKV_COLS))
    inp['state_rwkv_wkv'] = nrm((L, DEC_BATCH, NH, HEAD_DIM, HEAD_DIM), 0.1)
    inp['state_gla'] = nrm((L, DEC_BATCH, NH, HEAD_DIM, HEAD_DIM), 0.1)
    inp['state_dn_conv'] = nrm((L, DEC_BATCH, DN_CONV - 1, 3 * GW))
    inp['state_dn'] = nrm((L, DEC_BATCH, NH, HEAD_DIM, HEAD_DIM), 0.1)
    inp['state_ssm_conv'] = nrm((L, DEC_BATCH, SSM_CONV - 1, SSM_XBC))
    inp['state_ssm'] = nrm((L, DEC_BATCH, NH, HEAD_DIM, SSM_STATE), 0.1)
    inp['c_prompt'] = nrm((BATCH, D_MODEL))
    inp['c_sample'] = nrm((DEC_BATCH, D_MODEL))
    inp['ada_w'] = nrm((L, D_MODEL, 6 * D_MODEL), 0.3 * D_MODEL ** -0.5)
    inp['ada_b'] = nrm((L, 6 * D_MODEL), 0.02)
    inp['norm1_w'] = 1.0 + nrm((L, D_MODEL), 0.02)
    inp['norm2_w'] = 1.0 + nrm((L, D_MODEL), 0.02)
    inp['w_in'] = nrm((L, D_MODEL, P_TOTAL), D_MODEL ** -0.5)
    inp['w_out'] = nrm((L, MIX_WIDTH, D_MODEL), MIX_WIDTH ** -0.5)
    inp['w_up'] = nrm((L, D_MODEL, D_FF), D_MODEL ** -0.5)
    inp['w_down'] = nrm((L, D_FF, D_MODEL), D_FF ** -0.5)
    inp['rwkv_mu'] = unif((L, RWKV_COLS), 0.0, 1.0)
    inp['rwkv_w0'] = nrm((L, GW), 0.5)
    inp['rwkv_w2'] = nrm((L, RWKV_R_W, GW), 0.5 * RWKV_R_W ** -0.5)
    inp['rwkv_a0'] = nrm((L, GW), 0.1)
    inp['rwkv_a2'] = nrm((L, RWKV_R_A, GW), RWKV_R_A ** -0.5)
    inp['rwkv_g2'] = nrm((L, RWKV_R_G, GW), RWKV_R_G ** -0.5)
    inp['rwkv_k_k'] = 0.85 + nrm((L, GW), 0.05)
    inp['rwkv_k_a'] = 1.0 + nrm((L, GW), 0.05)
    inp['rwkv_r_k'] = nrm((L, GW), 0.1)
    inp['rwkv_ln_w'] = 1.0 + nrm((L, GW), 0.02)
    inp['rwkv_ln_b'] = nrm((L, GW), 0.01)
    inp['gla_gk_w2'] = nrm((L, GLA_R, GW), GLA_R ** -0.5)
    inp['gla_gk_b'] = nrm((L, GW), 0.5)
    inp['gla_norm_w'] = 1.0 + nrm((L, GW), 0.02)
    inp['dn_conv_w'] = nrm((L, DN_CONV, 3 * GW), 0.5)
    inp['dn_A_log'] = jnp.log(unif((L, NH), 1.0, 16.0))
    inp['dn_dt_bias'] = dn_dt + jnp.log(-jnp.expm1(-dn_dt))
    inp['dn_norm_w'] = 1.0 + nrm((L, GW), 0.02)
    inp['ssm_conv_w'] = nrm((L, SSM_CONV, SSM_XBC), 0.5)
    inp['ssm_conv_b'] = nrm((L, SSM_XBC), 0.01)
    inp['ssm_dt_bias'] = ssm_dt + jnp.log(-jnp.expm1(-ssm_dt))
    inp['ssm_A_log'] = jnp.log(unif((L, NH), 1.0, 16.0))
    inp['ssm_D'] = 1.0 + nrm((L, NH), 0.1)
    inp['ssm_norm_w'] = 1.0 + nrm((L, GW), 0.02)
    inp['final_norm_w'] = 1.0 + nrm((D_MODEL,), 0.02)
    return inp


def reference(x_prompt, x_sample, state_rwkv_shift, state_rwkv_wkv, state_gla, state_dn_conv,
              state_dn, state_ssm_conv, state_ssm, c_prompt, c_sample,
              ada_w, ada_b, norm1_w, norm2_w, w_in, w_out, w_up, w_down,
              rwkv_mu, rwkv_w0, rwkv_w2, rwkv_a0, rwkv_a2, rwkv_g2, rwkv_k_k, rwkv_k_a, rwkv_r_k,
              rwkv_ln_w, rwkv_ln_b, gla_gk_w2, gla_gk_b, gla_norm_w,
              dn_conv_w, dn_A_log, dn_dt_bias, dn_norm_w,
              ssm_conv_w, ssm_conv_b, ssm_dt_bias, ssm_A_log, ssm_D, ssm_norm_w, final_norm_w):
    P = dict(ada_w=ada_w, ada_b=ada_b, norm1_w=norm1_w, norm2_w=norm2_w, w_in=w_in, w_out=w_out,
             w_up=w_up, w_down=w_down, rwkv_mu=rwkv_mu, rwkv_w0=rwkv_w0, rwkv_w2=rwkv_w2,
             rwkv_a0=rwkv_a0, rwkv_a2=rwkv_a2, rwkv_g2=rwkv_g2, rwkv_k_k=rwkv_k_k, rwkv_k_a=rwkv_k_a,
             rwkv_r_k=rwkv_r_k, rwkv_ln_w=rwkv_ln_w, rwkv_ln_b=rwkv_ln_b, gla_gk_w2=gla_gk_w2,
             gla_gk_b=gla_gk_b, gla_norm_w=gla_norm_w, dn_conv_w=dn_conv_w, dn_A_log=dn_A_log,
             dn_dt_bias=dn_dt_bias, dn_norm_w=dn_norm_w, ssm_conv_w=ssm_conv_w, ssm_conv_b=ssm_conv_b,
             ssm_dt_bias=ssm_dt_bias, ssm_A_log=ssm_A_log, ssm_D=ssm_D, ssm_norm_w=ssm_norm_w)
    sample_states = (state_rwkv_shift, state_rwkv_wkv, state_gla, state_dn_conv,
                     state_dn, state_ssm_conv, state_ssm)
    prompt_states = tuple(jnp.zeros((DEPTH, x_prompt.shape[0]) + s.shape[2:], x_prompt.dtype)
                          for s in sample_states)
    y_prompt, ps = trunk(x_prompt, c_prompt, prompt_states, P, final_norm_w)
    y_sample, ss = trunk(x_sample, c_sample, sample_states, P, final_norm_w)
    p_shift, p_wkv, p_gla, p_dn_conv, p_dn, p_ssm_conv, p_ssm = ps
    s_shift, s_wkv, s_gla, s_dn_conv, s_dn, s_ssm_conv, s_ssm = ss
    return (y_prompt, y_sample, p_shift, p_wkv, p_gla, p_dn_conv, p_dn, p_ssm_conv, p_ssm,
            s_shift, s_wkv, s_gla, s_dn_conv, s_dn, s_ssm_conv, s_ssm)
```

```python
import functools
import math

import jax
import jax.numpy as jnp
from jax import lax
from jax.experimental import pallas as pl
from jax.experimental.pallas import tpu as pltpu

F32 = jnp.float32
MXU_DT = jnp.bfloat16

D_MODEL = 1024
NH = 4
HD = 64
GW = NH * HD
D_FF = 4 * D_MODEL
SSM_STATE = 128
EPS = 1e-6
RWKV_GN_EPS = 64e-5
GLA_GATE_NORM = 16.0
RWKV_COLS = 3 * GW + 32 + 32 + 64
CONV_COLS = 3 * GW
U_COLS = RWKV_COLS + 3 * 4 * GW + 128
MISC_GLA_GATE = 0
MISC_DN_A = 16
MISC_DN_B = 20
MISC_SSM_DT = 24

LANES = 128
SUBLANES = 8
VMEM_LIMIT = 56 * 1024 * 1024
PROMPT_CHUNK = 64
ROW_TILE = 256


def _mm(a, b):
    return jnp.dot(a.astype(MXU_DT), b.astype(MXU_DT), preferred_element_type=F32)


def _mm_nt(a, b):
    return lax.dot_general(a.astype(MXU_DT), b.astype(MXU_DT), (((1,), (1,)), ((), ())),
                           preferred_element_type=F32)


def _mm_tn(a, b):
    return lax.dot_general(a.astype(MXU_DT), b.astype(MXU_DT), (((0,), (0,)), ((), ())),
                           preferred_element_type=F32)


def _split(x, n):
    parts = []
    r = x
    for i in range(n):
        p = r.astype(MXU_DT)
        parts.append(p)
        if i + 1 < n:
            r = r - p.astype(F32)
    return parts


def _mm_sel_lhs(sel, x, n=3):
    acc = None
    for p in _split(x, n):
        d = jnp.dot(sel, p, preferred_element_type=F32)
        acc = d if acc is None else acc + d
    return acc


def _mm_sel_rhs(x, sel, n=2):
    acc = None
    for p in _split(x, n):
        d = jnp.dot(p, sel, preferred_element_type=F32)
        acc = d if acc is None else acc + d
    return acc


def _neumann_inv(n_mat, eye, size):
    t = eye + n_mat
    p = n_mat
    k = 1
    while 2 * k < size:
        p = _mm(p, p)
        t = t + _mm(t, p)
        k *= 2
    return t


def _seg_decay(cum_col, cum_row, incl):
    d = cum_col - cum_row
    return jnp.where(incl, jnp.exp(jnp.where(incl, d, 0.0)), 0.0)


def _silu(x):
    return x * jax.nn.sigmoid(x)


def _ada_body(c_ref, w_ref, b_ref, o_ref):
    c = c_ref[...]
    o_ref[0] = _mm(_silu(c), w_ref[0]) + b_ref[0]


def _ada_call(c_all, ada_w, ada_b):
    n_layers = ada_w.shape[0]
    rows = c_all.shape[0]
    tn = 1536
    return pl.pallas_call(
        _ada_body,
        grid=(n_layers, 6 * D_MODEL // tn),
        in_specs=[pl.BlockSpec((rows, D_MODEL), lambda l, j: (0, 0)),
                  pl.BlockSpec((1, D_MODEL, tn), lambda l, j: (l, 0, j)),
                  pl.BlockSpec((1, 1, tn), lambda l, j: (l, 0, j))],
        out_specs=pl.BlockSpec((1, rows, tn), lambda l, j: (l, 0, j)),
        out_shape=jax.ShapeDtypeStruct((n_layers, rows, 6 * D_MODEL), F32),
        compiler_params=pltpu.CompilerParams(vmem_limit_bytes=VMEM_LIMIT),
        name="ada_mod",
    )(c_all, ada_w, ada_b.reshape(n_layers, 1, 6 * D_MODEL))


def _rms(x):
    return x * lax.rsqrt(jnp.mean(x * x, -1, keepdims=True) + EPS)


def _inproj_body(x_ref, sh_ref, sc_ref, nw_ref, w_ref, o_rw, o_gla, o_dn, o_ssm, o_misc):
    bb, tt, _ = x_ref.shape
    h = _rms(x_ref[...]) * nw_ref[...]
    h = h * (1.0 + sc_ref[...]) + sh_ref[...]
    u = jnp.dot(h.reshape(bb * tt, D_MODEL).astype(MXU_DT), w_ref[...], preferred_element_type=F32)
    off = 0
    for ref in (o_rw, o_gla, o_dn, o_ssm, o_misc):
        w = ref.shape[-1]
        ref[...] = u[:, off:off + w].reshape(bb, tt, w)
        off += w


def _row_blocks(bsz, tlen):
    tt = min(tlen, ROW_TILE)
    bb = ROW_TILE // tt
    assert tlen % tt == 0 and bsz % bb == 0
    return bb, tt


def _inproj_call(x, mod, norm_w, w_r):
    bsz, tlen, _ = x.shape
    bb, tt = _row_blocks(bsz, tlen)
    widths = (RWKV_COLS, 4 * GW, 4 * GW, 4 * GW, LANES)

    def xmap(i, j):
        return (i, j, 0)

    return pl.pallas_call(
        _inproj_body,
        grid=(bsz // bb, tlen // tt),
        in_specs=[pl.BlockSpec((bb, tt, D_MODEL), xmap),
                  pl.BlockSpec((bb, 1, D_MODEL), lambda i, j: (i, 0, 0)),
                  pl.BlockSpec((bb, 1, D_MODEL), lambda i, j: (i, 0, 1)),
                  pl.BlockSpec((1, 1, D_MODEL), lambda i, j: (0, 0, 0)),
                  pl.BlockSpec((D_MODEL, U_COLS), lambda i, j: (0, 0), pipeline_mode=pl.Buffered(1))],
        out_specs=[pl.BlockSpec((bb, tt, w), xmap) for w in widths],
        out_shape=[jax.ShapeDtypeStruct((bsz, tlen, w), F32) for w in widths],
        compiler_params=pltpu.CompilerParams(vmem_limit_bytes=VMEM_LIMIT,
                                             dimension_semantics=("parallel", "parallel")),
        name="in_proj",
    )(x, mod, mod, norm_w.reshape(1, 1, D_MODEL), w_r)


def _outmlp_body(final, x_ref, y_ref, gt1_ref, sh_ref, sc_ref, gt2_ref, nw_ref, fnw_ref,
                 wo_ref, wu_ref, wd_ref, o_ref):
    bb, tt, _ = x_ref.shape
    rows = bb * tt
    att = jnp.dot(y_ref[...].reshape(rows, D_MODEL).astype(MXU_DT), wo_ref[...],
                  preferred_element_type=F32)
    x1 = x_ref[...] + gt1_ref[...] * att.reshape(bb, tt, D_MODEL)
    h = _rms(x1) * nw_ref[...]
    h = h * (1.0 + sc_ref[...]) + sh_ref[...]
    a = jnp.dot(h.reshape(rows, D_MODEL).astype(MXU_DT), wu_ref[...], preferred_element_type=F32)
    a = jnp.square(jnp.maximum(a, 0.0))
    f = jnp.dot(a.astype(MXU_DT), wd_ref[...], preferred_element_type=F32)
    x2 = x1 + gt2_ref[...] * f.reshape(bb, tt, D_MODEL)
    if final:
        x2 = _rms(x2) * fnw_ref[...]
    o_ref[...] = x2


def _outmlp_call(x, y_mix, mod, norm_w, final_norm_w, w_out, w_up, w_down, final):
    bsz, tlen, _ = x.shape
    bb, tt = _row_blocks(bsz, tlen)

    def xmap(i, j):
        return (i, j, 0)

    def modspec(k):
        return pl.BlockSpec((bb, 1, D_MODEL), lambda i, j: (i, 0, k))

    def wspec(shape):
        return pl.BlockSpec(shape, lambda i, j: (0, 0), pipeline_mode=pl.Buffered(1))

    vec = pl.BlockSpec((1, 1, D_MODEL), lambda i, j: (0, 0, 0))
    return pl.pallas_call(
        functools.partial(_outmlp_body, final),
        grid=(bsz // bb, tlen // tt),
        in_specs=[pl.BlockSpec((bb, tt, D_MODEL), xmap), pl.BlockSpec((bb, tt, D_MODEL), xmap),
                  modspec(2), modspec(3), modspec(4), modspec(5), vec, vec,
                  wspec((D_MODEL, D_MODEL)), wspec((D_MODEL, D_FF)), wspec((D_FF, D_MODEL))],
        out_specs=pl.BlockSpec((bb, tt, D_MODEL), xmap),
        out_shape=jax.ShapeDtypeStruct((bsz, tlen, D_MODEL), F32),
        compiler_params=pltpu.CompilerParams(vmem_limit_bytes=VMEM_LIMIT,
                                             dimension_semantics=("parallel", "parallel")),
        name="out_mlp",
    )(x, y_mix, mod, mod, mod, mod, norm_w.reshape(1, 1, D_MODEL),
      final_norm_w.reshape(1, 1, D_MODEL), w_out, w_up, w_down)


(V_RW_W0, V_RW_A0, V_RW_KK, V_RW_KA, V_RW_RK, V_RW_LNW, V_RW_LNB,
 V_GLA_B, V_GLA_NW, V_DN_NW, V_SSM_D, V_SSM_NW) = range(12)
P_DN_ALOG, P_DN_DTB, P_SSM_ALOG, P_SSM_DTB = range(4)
CV_DN_W, CV_SSM_W, CV_SSM_B = 0, 4, 8
LR_RW_W2, LR_RW_A2, LR_RW_G2, LR_GLA_GK = range(4)


def _blockdiag(a, b):
    za = jnp.zeros_like(a)
    return jnp.concatenate([jnp.concatenate([a, za], axis=1),
                            jnp.concatenate([za, b], axis=1)], axis=0)


def _mixer_body(C, n_t,
                u_rw_ref, u_gla_ref, u_dn_ref, u_ssm_ref, misc_ref,
                shift0_ref, wkv0_ref, gla0_ref, dnc0_ref, dn0_ref, ssc0_ref, ssm0_ref,
                mu_ref, v256_ref, v128_ref, conv_ref, lr_ref,
                y_ref, shift1_ref, wkv1_ref, gla1_ref, dnc1_ref, dn1_ref, ssc1_ref, ssm1_ref,
                xp_rw, xp_dn, xp_ss, s_wkv, s_gla, s_dn, s_ssm):
    t = pl.program_id(1)

    @pl.when(t == 0)
    def _init():
        xp_rw[0:SUBLANES, :] = shift0_ref[0]
        xp_dn[0:SUBLANES, :] = dnc0_ref[0]
        xp_ss[0:SUBLANES, :] = ssc0_ref[0]
        for p in range(2):
            s_wkv[p] = _blockdiag(wkv0_ref[0, 2 * p], wkv0_ref[0, 2 * p + 1])
            s_gla[p] = _blockdiag(gla0_ref[0, 2 * p], gla0_ref[0, 2 * p + 1]).T
            s_dn[p] = _blockdiag(dn0_ref[0, 2 * p], dn0_ref[0, 2 * p + 1])
            s_ssm[p] = jnp.concatenate([ssm0_ref[0, 2 * p], ssm0_ref[0, 2 * p + 1]], axis=0)

    ri = lax.broadcasted_iota(jnp.int32, (C, C), 0)
    ci = lax.broadcasted_iota(jnp.int32, (C, C), 1)
    incl = ci <= ri
    strict = ci < ri
    eye = jnp.where(ri == ci, 1.0, 0.0).astype(F32)
    tri = jnp.where(incl, 1.0, 0.0).astype(MXU_DT)
    lane = lax.broadcasted_iota(jnp.int32, (1, LANES), 1)
    hmask = (lane < HD, lane >= HD)
    r128 = lax.shift_right_logical(lax.broadcasted_iota(jnp.int32, (LANES, LANES), 0), 6)
    c128 = lax.shift_right_logical(lax.broadcasted_iota(jnp.int32, (LANES, LANES), 1), 6)
    bd = r128 == c128
    r256 = lax.broadcasted_iota(jnp.int32, (GW, GW), 0)
    c256 = lax.broadcasted_iota(jnp.int32, (GW, GW), 1)
    blk64 = jnp.where(lax.shift_right_logical(r256, 6) == lax.shift_right_logical(c256, 6),
                      1.0, 0.0).astype(MXU_DT)
    blk128 = jnp.where(lax.shift_right_logical(r256, 7) == lax.shift_right_logical(c256, 7),
                       1.0, 0.0).astype(MXU_DT)

    def vrow(i):
        return v256_ref[i:i + 1, :]

    def hsum(x):
        return _mm_sel_rhs(x, blk64)

    def head(x, h):
        return jnp.where(hmask[h], x, 0.0)

    def expand(tile, base):
        cols = [jnp.broadcast_to(tile[:, base + h:base + h + 1], (C, LANES)) for h in range(NH)]
        return [jnp.where(hmask[0], cols[0], cols[1]), jnp.where(hmask[0], cols[2], cols[3])]

    def psl(p):
        return slice(LANES * p, LANES * (p + 1))

    misc = misc_ref[0]
    g_dn = -jnp.exp(v128_ref[P_DN_ALOG:P_DN_ALOG + 1, :]) * jax.nn.softplus(
        misc + v128_ref[P_DN_DTB:P_DN_DTB + 1, :])
    dt_ss = jax.nn.softplus(misc + v128_ref[P_SSM_DTB:P_SSM_DTB + 1, :])
    la_ss = dt_ss * -jnp.exp(v128_ref[P_SSM_ALOG:P_SSM_ALOG + 1, :])
    beta = jax.nn.sigmoid(misc)
    cum = _mm_sel_lhs(tri, jnp.where(lane < MISC_DN_B, g_dn, la_ss))
    cum_t = cum.T

    def seg(base, hh):
        return _seg_decay(cum[:, base + hh:base + hh + 1], cum_t[base + hh:base + hh + 1, :], incl)

    u_rw = u_rw_ref[0]
    xp_rw[SUBLANES:SUBLANES + C, :] = u_rw
    u_prev = xp_rw[SUBLANES - 1:SUBLANES - 1 + C, :]
    xp_rw[0:SUBLANES, :] = xp_rw[C:C + SUBLANES, :]
    xs = u_rw + (u_prev - u_rw) * mu_ref[...]
    r = xs[:, 0:GW]
    k = xs[:, GW:2 * GW]
    v = xs[:, 2 * GW:3 * GW]
    x7 = xs[:, 3 * GW:RWKV_COLS]
    log_w = -jax.nn.softplus(-(vrow(V_RW_W0) + _mm(jnp.tanh(x7), lr_ref[LR_RW_W2]))) - 0.5
    lw = -jnp.exp(log_w)
    a = jax.nn.sigmoid(vrow(V_RW_A0) + _mm(x7, lr_ref[LR_RW_A2]))
    g = _mm(jax.nn.sigmoid(x7), lr_ref[LR_RW_G2])
    kk = k * vrow(V_RW_KK)
    kk = kk * lax.rsqrt(hsum(kk * kk) + EPS)
    k2 = k * (1.0 + (a - 1.0) * vrow(V_RW_KA))
    bcum = _mm_sel_lhs(tri, lw)
    e_nb = jnp.exp(-bcum)
    blast = bcum[C - 1:C, :]
    e_rem = jnp.exp(blast - bcum)
    nka = -kk * a
    r_t = r * jnp.exp(bcum)
    a_t = kk * jnp.exp(bcum - lw)
    b_t = nka * e_nb
    k_t = k2 * e_nb
    k_rem = k2 * e_rem
    b_rem = nka * e_rem
    y_pairs = []
    for p in range(2):
        sl = psl(p)
        s = s_wkv[p]
        vp = v[:, sl]
        a_s = _mm_nt(a_t[:, sl], s)
        u_p = None
        a_rb = []
        a_rk = []
        for h in range(2):
            bh = head(b_t[:, sl], h)
            kh = head(k_t[:, sl], h)
            n_ab = jnp.where(strict, _mm_nt(a_t[:, sl], bh), 0.0)
            n_ak = jnp.where(strict, _mm_nt(a_t[:, sl], kh), 0.0)
            a_rb.append(jnp.where(incl, _mm_nt(r_t[:, sl], bh), 0.0))
            a_rk.append(jnp.where(incl, _mm_nt(r_t[:, sl], kh), 0.0))
            tinv = _neumann_inv(n_ab, eye, C)
            uh = _mm(tinv, head(a_s, h) + _mm(n_ak, head(vp, h)))
            u_p = uh if u_p is None else u_p + uh
        yp = _mm_nt(r_t[:, sl], s)
        for h in range(2):
            yp = yp + _mm(a_rb[h], head(u_p, h)) + _mm(a_rk[h], head(vp, h))
        s_new = s * jnp.exp(blast[:, sl]) + _mm_tn(vp, k_rem[:, sl]) + _mm_tn(u_p, b_rem[:, sl])
        s_wkv[p] = jnp.where(bd, s_new, 0.0)
        y_pairs.append(yp)
    y = jnp.concatenate(y_pairs, axis=1)
    mean = hsum(y) * (1.0 / HD)
    yc = y - mean
    var = hsum(yc * yc) * (1.0 / HD)
    y = yc * lax.rsqrt(var + RWKV_GN_EPS) * vrow(V_RW_LNW) + vrow(V_RW_LNB)
    y = y + hsum(r * k2 * vrow(V_RW_RK)) * v
    y_ref[0, :, 0:GW] = y * g

    u_gla = u_gla_ref[0]
    q = u_gla[:, 0:GW] * (HD ** -0.5)
    k = u_gla[:, GW:2 * GW]
    v = u_gla[:, 2 * GW:3 * GW]
    gz = u_gla[:, 3 * GW:4 * GW]
    la = jax.nn.log_sigmoid(_mm(misc, lr_ref[LR_GLA_GK]) + vrow(V_GLA_B)) * (1.0 / GLA_GATE_NORM)
    bcum = _mm_sel_lhs(tri, la)
    blast = bcum[C - 1:C, :]
    q_in = q * jnp.exp(bcum)
    k_in = k * jnp.exp(-bcum)
    k_out = k * jnp.exp(blast - bcum)
    o_pairs = []
    for p in range(2):
        sl = psl(p)
        st = s_gla[p]
        vp = v[:, sl]
        op = _mm_nt(q_in[:, sl], st)
        for h in range(2):
            a_h = jnp.where(incl, _mm_nt(q_in[:, sl], head(k_in[:, sl], h)), 0.0)
            op = op + _mm(a_h, head(vp, h))
        st_new = st * jnp.exp(blast[:, sl]) + _mm_tn(vp, k_out[:, sl])
        s_gla[p] = jnp.where(bd, st_new, 0.0)
        o_pairs.append(op)
    o = jnp.concatenate(o_pairs, axis=1)
    o = o * lax.rsqrt(hsum(o * o) * (1.0 / HD) + EPS) * vrow(V_GLA_NW)
    y_ref[0, :, GW:2 * GW] = o * _silu(gz)

    u_dn = u_dn_ref[0]
    xp_dn[SUBLANES:SUBLANES + C, :] = u_dn[:, 0:CONV_COLS]
    qkv = None
    for i in range(4):
        term = xp_dn[SUBLANES - 3 + i:SUBLANES - 3 + i + C, :] * conv_ref[CV_DN_W + i:CV_DN_W + i + 1, :]
        qkv = term if qkv is None else qkv + term
    xp_dn[0:SUBLANES, :] = xp_dn[C:C + SUBLANES, :]
    qkv = _silu(qkv)
    q = qkv[:, 0:GW]
    k = qkv[:, GW:2 * GW]
    v = qkv[:, 2 * GW:3 * GW]
    z = u_dn[:, CONV_COLS:CONV_COLS + GW]
    q = q * lax.rsqrt(hsum(q * q) + EPS) * (HD ** -0.5)
    k = k * lax.rsqrt(hsum(k * k) + EPS)
    beta_x = expand(beta, MISC_DN_B)
    cum_x = expand(cum, MISC_DN_A)
    o_pairs = []
    for p in range(2):
        sl = psl(p)
        qp, kp, vp = q[:, sl], k[:, sl], v[:, sl]
        cx = cum_x[p]
        ecx = jnp.exp(cx)
        last = cx[C - 1:C, :]
        kb = kp * beta_x[p]
        vb = vp * beta_x[p]
        kbe = kb * ecx
        s = s_dn[p]
        u_p = None
        w_p = None
        a_l = []
        for h in range(2):
            dec = seg(MISC_DN_A, 2 * p + h)
            kh = head(kp, h)
            n_m = jnp.where(strict, -(_mm_nt(kb, kh) * dec), 0.0)
            tinv = _neumann_inv(n_m, eye, C)
            a_l.append(_mm_nt(qp, kh) * dec)
            uh = _mm(tinv, head(vb, h))
            wh = _mm(tinv, head(kbe, h))
            u_p = uh if u_p is None else u_p + uh
            w_p = wh if w_p is None else w_p + wh
        v_new = u_p - _mm(w_p, s)
        op = _mm(qp * ecx, s) + _mm(a_l[0], head(v_new, 0)) + _mm(a_l[1], head(v_new, 1))
        s_new = s * jnp.exp(last) + _mm_tn(kp * jnp.exp(last - cx), v_new)
        s_dn[p] = jnp.where(bd, s_new, 0.0)
        o_pairs.append(op)
    o = jnp.concatenate(o_pairs, axis=1)
    o = o * lax.rsqrt(hsum(o * o) * (1.0 / HD) + EPS) * vrow(V_DN_NW)
    y_ref[0, :, 2 * GW:3 * GW] = o * _silu(z)

    u_ssm = u_ssm_ref[0]
    z = u_ssm[:, 0:GW]
    xp_ss[SUBLANES:SUBLANES + C, :] = u_ssm[:, GW:GW + CONV_COLS]
    xbc = conv_ref[CV_SSM_B:CV_SSM_B + 1, :]
    for i in range(4):
        xbc = xbc + xp_ss[SUBLANES - 3 + i:SUBLANES - 3 + i + C, :] * conv_ref[CV_SSM_W + i:CV_SSM_W + i + 1, :]
    xp_ss[0:SUBLANES, :] = xp_ss[C:C + SUBLANES, :]
    xbc = _silu(xbc)
    xs_ = xbc[:, 0:GW]
    bm = xbc[:, GW:2 * GW]
    cm = xbc[:, 2 * GW:3 * GW]
    dt_x = expand(dt_ss, MISC_SSM_DT)
    cum_x = expand(cum, MISC_SSM_DT)
    y_pairs = []
    for p in range(2):
        sl = psl(p)
        xh = xs_[:, sl]
        xdt = xh * dt_x[p]
        bp, cp = bm[:, sl], cm[:, sl]
        cx = cum_x[p]
        last = cx[C - 1:C, :]
        s = s_ssm[p]
        gmat = _mm_nt(cp, bp)
        yp = jnp.exp(cx) * _mm_nt(cp, s)
        for h in range(2):
            yp = yp + _mm(gmat * seg(MISC_SSM_DT, 2 * p + h), head(xdt, h))
        lane0 = MISC_SSM_DT + 2 * p
        dcol = jnp.concatenate(
            [jnp.broadcast_to(jnp.exp(cum[C - 1:C, lane0:lane0 + 1]), (HD, SSM_STATE)),
             jnp.broadcast_to(jnp.exp(cum[C - 1:C, lane0 + 1:lane0 + 2]), (HD, SSM_STATE))], axis=0)
        s_ssm[p] = s * dcol + _mm_tn(xdt * jnp.exp(last - cx), bp)
        y_pairs.append(yp + v256_ref[V_SSM_D:V_SSM_D + 1, sl] * xh)
    y = jnp.concatenate(y_pairs, axis=1) * _silu(z)
    y = y * lax.rsqrt(_mm_sel_rhs(y * y, blk128) * (1.0 / (2 * HD)) + EPS) * vrow(V_SSM_NW)
    y_ref[0, :, 3 * GW:4 * GW] = y

    @pl.when(t == n_t - 1)
    def _fin():
        shift1_ref[0] = xp_rw[0:SUBLANES, :]
        dnc1_ref[0] = xp_dn[0:SUBLANES, :]
        ssc1_ref[0] = xp_ss[0:SUBLANES, :]
        for p in range(2):
            sw = s_wkv[p]
            sg = s_gla[p].T
            sd = s_dn[p]
            ss = s_ssm[p]
            for h in range(2):
                hs = slice(HD * h, HD * (h + 1))
                wkv1_ref[0, 2 * p + h] = sw[hs, hs]
                gla1_ref[0, 2 * p + h] = sg[hs, hs]
                dn1_ref[0, 2 * p + h] = sd[hs, hs]
                ssm1_ref[0, 2 * p + h] = ss[hs, :]


def _mixer_call(u_parts, states, mparams):
    u_rw, u_gla, u_dn, u_ssm, misc = u_parts
    shift0, wkv0, gla0, dnc0, dn0, ssc0, ssm0 = states
    bsz, tlen, _ = u_rw.shape
    C = math.gcd(tlen, PROMPT_CHUNK)
    n_t = tlen // C

    def tmap(b, t):
        return (b, t, 0)

    def bmap3(b, t):
        return (b, 0, 0)

    def bmap4(b, t):
        return (b, 0, 0, 0)

    def full(arr):
        nd = arr.ndim
        return pl.BlockSpec(arr.shape, lambda b, t: (0,) * nd)

    sq = (1, NH, HD, HD)
    state_specs = [pl.BlockSpec((1, SUBLANES, RWKV_COLS), bmap3), pl.BlockSpec(sq, bmap4),
                   pl.BlockSpec(sq, bmap4), pl.BlockSpec((1, SUBLANES, CONV_COLS), bmap3),
                   pl.BlockSpec(sq, bmap4), pl.BlockSpec((1, SUBLANES, CONV_COLS), bmap3),
                   pl.BlockSpec((1, NH, HD, SSM_STATE), bmap4)]
    state_shapes = [jax.ShapeDtypeStruct((bsz, SUBLANES, RWKV_COLS), F32),
                    jax.ShapeDtypeStruct((bsz, NH, HD, HD), F32),
                    jax.ShapeDtypeStruct((bsz, NH, HD, HD), F32),
                    jax.ShapeDtypeStruct((bsz, SUBLANES, CONV_COLS), F32),
                    jax.ShapeDtypeStruct((bsz, NH, HD, HD), F32),
                    jax.ShapeDtypeStruct((bsz, SUBLANES, CONV_COLS), F32),
                    jax.ShapeDtypeStruct((bsz, NH, HD, SSM_STATE), F32)]
    outs = pl.pallas_call(
        functools.partial(_mixer_body, C, n_t),
        grid=(bsz, n_t),
        in_specs=[pl.BlockSpec((1, C, RWKV_COLS), tmap), pl.BlockSpec((1, C, 4 * GW), tmap),
                  pl.BlockSpec((1, C, 4 * GW), tmap), pl.BlockSpec((1, C, 4 * GW), tmap),
                  pl.BlockSpec((1, C, LANES), tmap)] + state_specs + [full(a) for a in mparams],
        out_specs=[pl.BlockSpec((1, C, D_MODEL), tmap)] + state_specs,
        out_shape=[jax.ShapeDtypeStruct((bsz, tlen, D_MODEL), F32)] + state_shapes,
        scratch_shapes=[pltpu.VMEM((C + SUBLANES, RWKV_COLS), F32),
                        pltpu.VMEM((C + SUBLANES, CONV_COLS), F32),
                        pltpu.VMEM((C + SUBLANES, CONV_COLS), F32),
                        pltpu.VMEM((2, LANES, LANES), F32), pltpu.VMEM((2, LANES, LANES), F32),
                        pltpu.VMEM((2, LANES, LANES), F32), pltpu.VMEM((2, LANES, SSM_STATE), F32)],
        compiler_params=pltpu.CompilerParams(vmem_limit_bytes=VMEM_LIMIT,
                                             dimension_semantics=("parallel", "arbitrary")),
        name="mixers",
    )(u_rw, u_gla, u_dn, u_ssm, misc, shift0, wkv0, gla0, dnc0, dn0, ssc0, ssm0, *mparams)
    return outs[0], tuple(outs[1:])


def _pad_rows(m, lo, total):
    return jnp.pad(m, ((lo, total - lo - m.shape[0]), (0, 0)))


def _lane_vec(vals, lo):
    return jnp.pad(vals, (lo, LANES - lo - vals.shape[0]))


def _pack_layer(P, l):
    w_in = P['w_in'][l]
    o_gla = RWKV_COLS
    o_dn = o_gla + 4 * GW + 16
    o_ssm = o_dn + 4 * GW + 8
    w_r = jnp.concatenate(
        [w_in[:, 0:o_gla + 4 * GW],
         w_in[:, o_dn:o_dn + 4 * GW],
         w_in[:, o_ssm:o_ssm + 4 * GW],
         w_in[:, o_gla + 4 * GW:o_dn],
         w_in[:, o_dn + 4 * GW:o_ssm],
         w_in[:, o_ssm + 4 * GW:],
         jnp.zeros((D_MODEL, LANES - 28), F32)], axis=1).astype(MXU_DT)
    v256 = jnp.stack([P['rwkv_w0'][l], P['rwkv_a0'][l], P['rwkv_k_k'][l], P['rwkv_k_a'][l],
                      P['rwkv_r_k'][l], P['rwkv_ln_w'][l], P['rwkv_ln_b'][l],
                      P['gla_gk_b'][l], P['gla_norm_w'][l], P['dn_norm_w'][l],
                      jnp.repeat(P['ssm_D'][l], HD), P['ssm_norm_w'][l]])
    v256 = jnp.pad(v256, ((0, 16 - v256.shape[0]), (0, 0)))
    v128 = jnp.stack([_lane_vec(P['dn_A_log'][l], MISC_DN_A), _lane_vec(P['dn_dt_bias'][l], MISC_DN_A),
                      _lane_vec(P['ssm_A_log'][l], MISC_SSM_DT), _lane_vec(P['ssm_dt_bias'][l], MISC_SSM_DT)])
    v128 = jnp.pad(v128, ((0, 4), (0, 0)))
    conv = jnp.concatenate([P['dn_conv_w'][l], P['ssm_conv_w'][l], P['ssm_conv_b'][l][None],
                            jnp.zeros((7, CONV_COLS), F32)], axis=0)
    lr = jnp.stack([_pad_rows(P['rwkv_w2'][l], 0, LANES), _pad_rows(P['rwkv_a2'][l], 32, LANES),
                    _pad_rows(P['rwkv_g2'][l], 64, LANES),
                    _pad_rows(P['gla_gk_w2'][l], MISC_GLA_GATE, LANES)]).astype(MXU_DT)
    mparams = (P['rwkv_mu'][l][None], v256, v128, conv, lr)
    dense = dict(w_r=w_r, w_out=P['w_out'][l].astype(MXU_DT), w_up=P['w_up'][l].astype(MXU_DT),
                 w_down=P['w_down'][l].astype(MXU_DT), norm1=P['norm1_w'][l], norm2=P['norm2_w'][l])
    return mparams, dense


def _pad_tail_rows(a):
    return jnp.pad(a, ((0, 0), (SUBLANES - a.shape[1], 0), (0, 0)))


def _trunk(x, mod_rows, states, packed, final_norm_w):
    n_layers = len(packed)
    new = []
    for l in range(n_layers):
        mparams, dense = packed[l]
        mod = mod_rows[l][:, None, :]
        shift0, wkv0, gla0, dnc0, dn0, ssc0, ssm0 = (s[l] for s in states)
        st_in = (_pad_tail_rows(shift0[:, None, :]), wkv0, gla0, _pad_tail_rows(dnc0), dn0,
                 _pad_tail_rows(ssc0), ssm0)
        u_parts = _inproj_call(x, mod, dense['norm1'], dense['w_r'])
        y_mix, st = _mixer_call(u_parts, st_in, mparams)
        x = _outmlp_call(x, y_mix, mod, dense['norm2'], final_norm_w, dense['w_out'],
                         dense['w_up'], dense['w_down'], final=(l == n_layers - 1))
        shift1, wkv1, gla1, dnc1, dn1, ssc1, ssm1 = st
        new.append((shift1[:, SUBLANES - 1], wkv1, gla1, dnc1[:, SUBLANES - 3:], dn1,
                    ssc1[:, SUBLANES - 3:], ssm1))
    return x, tuple(jnp.stack([s[i] for s in new]) for i in range(7))


def kernel(x_prompt, x_sample, state_rwkv_shift, state_rwkv_wkv, state_gla, state_dn_conv,
           state_dn, state_ssm_conv, state_ssm, c_prompt, c_sample,
           ada_w, ada_b, norm1_w, norm2_w, w_in, w_out, w_up, w_down,
           rwkv_mu, rwkv_w0, rwkv_w2, rwkv_a0, rwkv_a2, rwkv_g2, rwkv_k_k, rwkv_k_a, rwkv_r_k,
           rwkv_ln_w, rwkv_ln_b, gla_gk_w2, gla_gk_b, gla_norm_w,
           dn_conv_w, dn_A_log, dn_dt_bias, dn_norm_w,
           ssm_conv_w, ssm_conv_b, ssm_dt_bias, ssm_A_log, ssm_D, ssm_norm_w, final_norm_w):
    P = dict(norm1_w=norm1_w, norm2_w=norm2_w, w_in=w_in, w_out=w_out, w_up=w_up, w_down=w_down,
             rwkv_mu=rwkv_mu, rwkv_w0=rwkv_w0, rwkv_w2=rwkv_w2, rwkv_a0=rwkv_a0, rwkv_a2=rwkv_a2,
             rwkv_g2=rwkv_g2, rwkv_k_k=rwkv_k_k, rwkv_k_a=rwkv_k_a, rwkv_r_k=rwkv_r_k,
             rwkv_ln_w=rwkv_ln_w, rwkv_ln_b=rwkv_ln_b, gla_gk_w2=gla_gk_w2, gla_gk_b=gla_gk_b,
             gla_norm_w=gla_norm_w, dn_conv_w=dn_conv_w, dn_A_log=dn_A_log, dn_dt_bias=dn_dt_bias,
             dn_norm_w=dn_norm_w, ssm_conv_w=ssm_conv_w, ssm_conv_b=ssm_conv_b,
             ssm_dt_bias=ssm_dt_bias, ssm_A_log=ssm_A_log, ssm_D=ssm_D, ssm_norm_w=ssm_norm_w)
    n_layers = w_in.shape[0]
    n_prompt = x_prompt.shape[0]
    packed = [_pack_layer(P, l) for l in range(n_layers)]
    mod_all = _ada_call(jnp.concatenate([c_prompt, c_sample], axis=0), ada_w, ada_b)
    sample_states = (state_rwkv_shift, state_rwkv_wkv, state_gla, state_dn_conv,
                     state_dn, state_ssm_conv, state_ssm)
    prompt_states = tuple(jnp.zeros((n_layers, n_prompt) + s.shape[2:], F32) for s in sample_states)
    y_prompt, ps = _trunk(x_prompt, mod_all[:, :n_prompt], prompt_states, packed, final_norm_w)
    y_sample, ss = _trunk(x_sample, mod_all[:, n_prompt:], sample_states, packed, final_norm_w)
    return (y_prompt, y_sample) + ps + ss
```

```python
import functools
import math

import jax
import jax.numpy as jnp
from jax import lax
from jax.experimental import pallas as pl
from jax.experimental.pallas import tpu as pltpu

F32 = jnp.float32
MXU_DT = jnp.bfloat16

D_MODEL = 1024
NH = 4
HD = 64
GW = NH * HD
D_FF = 4 * D_MODEL
SSM_STATE = 128
EPS = 1e-6
RWKV_GN_EPS = 64e-5
GLA_GATE_NORM = 16.0
RWKV_COLS = 3 * GW + 32 + 32 + 64
CONV_COLS = 3 * GW
U_COLS = RWKV_COLS + 3 * 4 * GW + 128
MISC_GLA_GATE = 0
MISC_DN_A = 16
MISC_DN_B = 20
MISC_SSM_DT = 24

LANES = 128
SUBLANES = 8
VMEM_LIMIT = 56 * 1024 * 1024
PROMPT_CHUNK = 64
ROW_TILE = 256
SEQS_PER_STEP_LONG = 2
SEQS_PER_STEP_SHORT = 4


def _mm(a, b):
    return jnp.dot(a.astype(MXU_DT), b.astype(MXU_DT), preferred_element_type=F32)


def _mm_nt(a, b):
    return lax.dot_general(a.astype(MXU_DT), b.astype(MXU_DT), (((1,), (1,)), ((), ())),
                           preferred_element_type=F32)


def _mm_tn(a, b):
    return lax.dot_general(a.astype(MXU_DT), b.astype(MXU_DT), (((0,), (0,)), ((), ())),
                           preferred_element_type=F32)


def _split(x, n):
    parts = []
    r = x
    for i in range(n):
        p = r.astype(MXU_DT)
        parts.append(p)
        if i + 1 < n:
            r = r - p.astype(F32)
    return parts


def _mm_sel_lhs(sel, x, n=3):
    acc = None
    for p in _split(x, n):
        d = jnp.dot(sel, p, preferred_element_type=F32)
        acc = d if acc is None else acc + d
    return acc


def _mm_sel_rhs(x, sel, n=2):
    acc = None
    for p in _split(x, n):
        d = jnp.dot(p, sel, preferred_element_type=F32)
        acc = d if acc is None else acc + d
    return acc


def _seg_decay(cum_col, cum_row, incl):
    d = cum_col - cum_row
    return jnp.where(incl, jnp.exp(jnp.where(incl, d, 0.0)), 0.0)


def _silu(x):
    return x * jax.nn.sigmoid(x)


def _ada_body(c_ref, w_ref, b_ref, o_ref):
    c = c_ref[...]
    o_ref[0] = _mm(_silu(c), w_ref[0]) + b_ref[0]


def _ada_call(c_all, ada_w, ada_b):
    n_layers = ada_w.shape[0]
    rows = c_all.shape[0]
    tn = 1536
    return pl.pallas_call(
        _ada_body,
        grid=(n_layers, 6 * D_MODEL // tn),
        in_specs=[pl.BlockSpec((rows, D_MODEL), lambda l, j: (0, 0)),
                  pl.BlockSpec((1, D_MODEL, tn), lambda l, j: (l, 0, j)),
                  pl.BlockSpec((1, 1, tn), lambda l, j: (l, 0, j))],
        out_specs=pl.BlockSpec((1, rows, tn), lambda l, j: (l, 0, j)),
        out_shape=jax.ShapeDtypeStruct((n_layers, rows, 6 * D_MODEL), F32),
        compiler_params=pltpu.CompilerParams(vmem_limit_bytes=VMEM_LIMIT),
        name="ada_mod",
    )(c_all, ada_w, ada_b.reshape(n_layers, 1, 6 * D_MODEL))


W_IN_COLS = RWKV_COLS + (4 * GW + 16) + (4 * GW + 8) + (4 * GW + 4)


def _wprep_body(w_ref, o_ref):
    w = w_ref[0]
    o_gla = RWKV_COLS
    o_dn = o_gla + 4 * GW + 16
    o_ssm = o_dn + 4 * GW + 8
    parts = [w[:, 0:o_gla + 4 * GW],
             w[:, o_dn:o_dn + 4 * GW],
             w[:, o_ssm:o_ssm + 4 * GW],
             w[:, o_gla + 4 * GW:o_dn],
             w[:, o_dn + 4 * GW:o_ssm],
             w[:, o_ssm + 4 * GW:W_IN_COLS],
             jnp.zeros((w.shape[0], LANES - 28), F32)]
    o_ref[0] = jnp.concatenate(parts, axis=1).astype(MXU_DT)


def _wprep_call(w_in):
    n_layers = w_in.shape[0]
    tr = 128
    return pl.pallas_call(
        _wprep_body,
        grid=(n_layers, D_MODEL // tr),
        in_specs=[pl.BlockSpec((1, tr, W_IN_COLS), lambda l, i: (l, i, 0))],
        out_specs=pl.BlockSpec((1, tr, U_COLS), lambda l, i: (l, i, 0)),
        out_shape=jax.ShapeDtypeStruct((n_layers, D_MODEL, U_COLS), MXU_DT),
        compiler_params=pltpu.CompilerParams(vmem_limit_bytes=VMEM_LIMIT,
                                             dimension_semantics=("parallel", "parallel")),
        name="w_in_relayout",
    )(w_in)


def _rms(x):
    return x * lax.rsqrt(jnp.mean(x * x, -1, keepdims=True) + EPS)


def _inproj_body(x_ref, sh_ref, sc_ref, nw_ref, w_ref, o_rw, o_gla, o_dn, o_ssm, o_misc):
    bb, tt, _ = x_ref.shape
    h = _rms(x_ref[...]) * nw_ref[...]
    h = h * (1.0 + sc_ref[...]) + sh_ref[...]
    u = jnp.dot(h.reshape(bb * tt, D_MODEL).astype(MXU_DT), w_ref[...], preferred_element_type=F32)
    off = 0
    for ref in (o_rw, o_gla, o_dn, o_ssm, o_misc):
        w = ref.shape[-1]
        ref[...] = u[:, off:off + w].reshape(bb, tt, w)
        off += w


def _row_blocks(bsz, tlen):
    tt = min(tlen, ROW_TILE)
    bb = ROW_TILE // tt
    assert tlen % tt == 0 and bsz % bb == 0
    return bb, tt


def _inproj_call(x, mod, norm_w, w_r, layer):
    bsz, tlen, _ = x.shape
    bb, tt = _row_blocks(bsz, tlen)
    widths = (RWKV_COLS, 4 * GW, 4 * GW, 4 * GW, LANES)

    def xmap(i, j):
        return (i, j, 0)

    return pl.pallas_call(
        _inproj_body,
        grid=(bsz // bb, tlen // tt),
        in_specs=[pl.BlockSpec((bb, tt, D_MODEL), xmap),
                  pl.BlockSpec((bb, 1, D_MODEL), lambda i, j: (i, 0, 0)),
                  pl.BlockSpec((bb, 1, D_MODEL), lambda i, j: (i, 0, 1)),
                  pl.BlockSpec((1, 1, D_MODEL), lambda i, j: (0, 0, 0)),
                  pl.BlockSpec((None, D_MODEL, U_COLS), lambda i, j: (layer, 0, 0),
                               pipeline_mode=pl.Buffered(1))],
        out_specs=[pl.BlockSpec((bb, tt, w), xmap) for w in widths],
        out_shape=[jax.ShapeDtypeStruct((bsz, tlen, w), F32) for w in widths],
        compiler_params=pltpu.CompilerParams(vmem_limit_bytes=VMEM_LIMIT,
                                             dimension_semantics=("parallel", "parallel")),
        name="in_proj",
    )(x, mod, mod, norm_w.reshape(1, 1, D_MODEL), w_r)


def _outmlp_body(final, x_ref, y_ref, gt1_ref, sh_ref, sc_ref, gt2_ref, nw_ref, fnw_ref,
                 wo_ref, wu_ref, wd_ref, o_ref):
    bb, tt, _ = x_ref.shape
    rows = bb * tt
    att = jnp.dot(y_ref[...].reshape(rows, D_MODEL).astype(MXU_DT), wo_ref[...],
                  preferred_element_type=F32)
    x1 = x_ref[...] + gt1_ref[...] * att.reshape(bb, tt, D_MODEL)
    h = _rms(x1) * nw_ref[...]
    h = h * (1.0 + sc_ref[...]) + sh_ref[...]
    a = jnp.dot(h.reshape(rows, D_MODEL).astype(MXU_DT), wu_ref[...], preferred_element_type=F32)
    a = jnp.square(jnp.maximum(a, 0.0))
    f = jnp.dot(a.astype(MXU_DT), wd_ref[...], preferred_element_type=F32)
    x2 = x1 + gt2_ref[...] * f.reshape(bb, tt, D_MODEL)
    if final:
        x2 = _rms(x2) * fnw_ref[...]
    o_ref[...] = x2


def _outmlp_call(x, y_mix, mod, norm_w, final_norm_w, w_out, w_up, w_down, layer, final):
    bsz, tlen, _ = x.shape
    bb, tt = _row_blocks(bsz, tlen)

    def xmap(i, j):
        return (i, j, 0)

    def modspec(k):
        return pl.BlockSpec((bb, 1, D_MODEL), lambda i, j: (i, 0, k))

    def wspec(shape):
        return pl.BlockSpec((None,) + shape, lambda i, j: (layer, 0, 0), pipeline_mode=pl.Buffered(1))

    vec = pl.BlockSpec((1, 1, D_MODEL), lambda i, j: (0, 0, 0))
    return pl.pallas_call(
        functools.partial(_outmlp_body, final),
        grid=(bsz // bb, tlen // tt),
        in_specs=[pl.BlockSpec((bb, tt, D_MODEL), xmap), pl.BlockSpec((bb, tt, D_MODEL), xmap),
                  modspec(2), modspec(3), modspec(4), modspec(5), vec, vec,
                  wspec((D_MODEL, D_MODEL)), wspec((D_MODEL, D_FF)), wspec((D_FF, D_MODEL))],
        out_specs=pl.BlockSpec((bb, tt, D_MODEL), xmap),
        out_shape=jax.ShapeDtypeStruct((bsz, tlen, D_MODEL), F32),
        compiler_params=pltpu.CompilerParams(vmem_limit_bytes=VMEM_LIMIT,
                                             dimension_semantics=("parallel", "parallel")),
        name="out_mlp",
    )(x, y_mix, mod, mod, mod, mod, norm_w.reshape(1, 1, D_MODEL),
      final_norm_w.reshape(1, 1, D_MODEL), w_out, w_up, w_down)


(V_RW_W0, V_RW_A0, V_RW_KK, V_RW_KA, V_RW_RK, V_RW_LNW, V_RW_LNB,
 V_GLA_B, V_GLA_NW, V_DN_NW, V_SSM_D, V_SSM_NW) = range(12)
P_DN_ALOG, P_DN_DTB, P_SSM_ALOG, P_SSM_DTB = range(4)
CV_DN_W, CV_SSM_W, CV_SSM_B = 0, 4, 8
LR_RW_W2, LR_RW_A2, LR_RW_G2, LR_GLA_GK = range(4)


def _blockdiag(a, b):
    za = jnp.zeros_like(a)
    return jnp.concatenate([jnp.concatenate([a, za], axis=1),
                            jnp.concatenate([za, b], axis=1)], axis=0)


def _mixer_body(C, n_t, G,
                u_rw_ref, u_gla_ref, u_dn_ref, u_ssm_ref, misc_ref,
                shift0_ref, wkv0_ref, gla0_ref, dnc0_ref, dn0_ref, ssc0_ref, ssm0_ref,
                mu_ref, v256_ref, v128_ref, conv_ref, lr_ref,
                y_ref, shift1_ref, wkv1_ref, gla1_ref, dnc1_ref, dn1_ref, ssc1_ref, ssm1_ref,
                xp_rw, xp_dn, xp_ss, s_wkv, s_gla, s_dn, s_ssm):
    t = pl.program_id(1)

    @pl.when(t == 0)
    def _init():
        for g in range(G):
            xp_rw[g, 0:SUBLANES, :] = shift0_ref[g]
            xp_dn[g, 0:SUBLANES, :] = dnc0_ref[g]
            xp_ss[g, 0:SUBLANES, :] = ssc0_ref[g]
            for p in range(2):
                s_wkv[g, p] = _blockdiag(wkv0_ref[g, 2 * p], wkv0_ref[g, 2 * p + 1])
                s_gla[g, p] = _blockdiag(gla0_ref[g, 2 * p], gla0_ref[g, 2 * p + 1]).T
                s_dn[g, p] = _blockdiag(dn0_ref[g, 2 * p], dn0_ref[g, 2 * p + 1])
                s_ssm[g, p] = jnp.concatenate([ssm0_ref[g, 2 * p], ssm0_ref[g, 2 * p + 1]], axis=0)

    ri = lax.broadcasted_iota(jnp.int32, (C, C), 0)
    ci = lax.broadcasted_iota(jnp.int32, (C, C), 1)
    incl = ci <= ri
    strict = ci < ri
    eye = jnp.where(ri == ci, 1.0, 0.0).astype(F32)
    tri = jnp.where(incl, 1.0, 0.0).astype(MXU_DT)
    lane = lax.broadcasted_iota(jnp.int32, (1, LANES), 1)
    hmask = (lane < HD, lane >= HD)
    r128 = lax.shift_right_logical(lax.broadcasted_iota(jnp.int32, (LANES, LANES), 0), 6)
    c128 = lax.shift_right_logical(lax.broadcasted_iota(jnp.int32, (LANES, LANES), 1), 6)
    bd = r128 == c128
    r256 = lax.broadcasted_iota(jnp.int32, (GW, GW), 0)
    c256 = lax.broadcasted_iota(jnp.int32, (GW, GW), 1)
    blk64 = jnp.where(lax.shift_right_logical(r256, 6) == lax.shift_right_logical(c256, 6),
                      1.0, 0.0).astype(MXU_DT)
    blk128 = jnp.where(lax.shift_right_logical(r256, 7) == lax.shift_right_logical(c256, 7),
                       1.0, 0.0).astype(MXU_DT)
    n_iter = 0
    while 2 * (1 << n_iter) < C:
        n_iter += 1

    def vrow(i):
        return v256_ref[i:i + 1, :]

    def hsum(x):
        return _mm_sel_rhs(x, blk64)

    def head(x, h):
        return jnp.where(hmask[h], x, 0.0)

    def expand(tile, base):
        cols = [jnp.broadcast_to(tile[:, base + h:base + h + 1], (C, LANES)) for h in range(NH)]
        return [jnp.where(hmask[0], cols[0], cols[1]), jnp.where(hmask[0], cols[2], cols[3])]

    def psl(p):
        return slice(LANES * p, LANES * (p + 1))

    def rows(*xs):
        return jnp.concatenate(xs, axis=0)

    shared = {}

    def neumann(n_list):
        ts = [eye + n for n in n_list]
        ps = list(n_list)
        for _ in range(n_iter):
            ps = [_mm(p, p) for p in ps]
            yield None
            ts = [tt + _mm(tt, p) for tt, p in zip(ts, ps)]
        yield ts

    def rwkv(g):
        u_rw = u_rw_ref[g]
        xp_rw[g, SUBLANES:SUBLANES + C, :] = u_rw
        u_prev = xp_rw[g, SUBLANES - 1:SUBLANES - 1 + C, :]
        xp_rw[g, 0:SUBLANES, :] = xp_rw[g, C:C + SUBLANES, :]
        xs = u_rw + (u_prev - u_rw) * mu_ref[...]
        r = xs[:, 0:GW]
        k = xs[:, GW:2 * GW]
        v = xs[:, 2 * GW:3 * GW]
        x7 = xs[:, 3 * GW:RWKV_COLS]
        w_pre = _mm(jnp.tanh(x7), lr_ref[LR_RW_W2])
        a_pre = _mm(x7, lr_ref[LR_RW_A2])
        gate = _mm(jax.nn.sigmoid(x7), lr_ref[LR_RW_G2])
        kk = k * vrow(V_RW_KK)
        kk_ss = hsum(kk * kk)
        yield
        log_w = -jax.nn.softplus(-(vrow(V_RW_W0) + w_pre)) - 0.5
        lw = -jnp.exp(log_w)
        a = jax.nn.sigmoid(vrow(V_RW_A0) + a_pre)
        kk = kk * lax.rsqrt(kk_ss + EPS)
        k2 = k * (1.0 + (a - 1.0) * vrow(V_RW_KA))
        bcum = _mm_sel_lhs(tri, lw)
        bonus = hsum(r * k2 * vrow(V_RW_RK))
        yield
        e_nb = jnp.exp(-bcum)
        blast = bcum[C - 1:C, :]
        e_rem = jnp.exp(blast - bcum)
        nka = -kk * a
        r_t = r * jnp.exp(bcum)
        a_t = kk * jnp.exp(bcum - lw)
        b_t = nka * e_nb
        k_t = k2 * e_nb
        k_rem = k2 * e_rem
        b_rem = nka * e_rem
        s = [s_wkv[g, p] for p in range(2)]
        ars = [_mm_nt(rows(a_t[:, psl(p)], r_t[:, psl(p)]), s[p]) for p in range(2)]
        n_ab, n_ak, a_r = [], [], []
        for p in range(2):
            sl = psl(p)
            for h in range(2):
                bh = head(b_t[:, sl], h)
                kh = head(k_t[:, sl], h)
                n_ab.append(_mm_nt(a_t[:, sl], bh))
                n_ak.append(_mm_nt(a_t[:, sl], kh))
                a_r.append((_mm_nt(r_t[:, sl], bh), _mm_nt(r_t[:, sl], kh)))
        yield
        n_ab = [jnp.where(strict, n, 0.0) for n in n_ab]
        n_ak = [jnp.where(strict, n, 0.0) for n in n_ak]
        a_r = [(jnp.where(incl, x, 0.0), jnp.where(incl, y, 0.0)) for x, y in a_r]
        rhs = [head(ars[i // 2][0:C], i % 2) + _mm(n_ak[i], head(v[:, psl(i // 2)], i % 2))
               for i in range(4)]
        tinv = None
        for step in neumann(n_ab):
            if step is None:
                yield
            else:
                tinv = step
        yield
        uh = [_mm(tinv[i], rhs[i]) for i in range(4)]
        yield
        y_pairs = []
        for p in range(2):
            sl = psl(p)
            vp = v[:, sl]
            u_p = uh[2 * p] + uh[2 * p + 1]
            yp = ars[p][C:2 * C]
            for h in range(2):
                yp = yp + _mm(a_r[2 * p + h][0], head(u_p, h)) + _mm(a_r[2 * p + h][1], head(vp, h))
            s_new = s[p] * jnp.exp(blast[:, sl]) + _mm_tn(rows(vp, u_p), rows(k_rem[:, sl], b_rem[:, sl]))
            s_wkv[g, p] = jnp.where(bd, s_new, 0.0)
            y_pairs.append(yp)
        yield
        y = jnp.concatenate(y_pairs, axis=1)
        mean = hsum(y) * (1.0 / HD)
        yield
        yc = y - mean
        var = hsum(yc * yc) * (1.0 / HD)
        yield
        y = yc * lax.rsqrt(var + RWKV_GN_EPS) * vrow(V_RW_LNW) + vrow(V_RW_LNB)
        y_ref[g, :, 0:GW] = (y + bonus * v) * gate

    def gla(g):
        u_gla = u_gla_ref[g]
        q = u_gla[:, 0:GW] * (HD ** -0.5)
        k = u_gla[:, GW:2 * GW]
        v = u_gla[:, 2 * GW:3 * GW]
        gz = u_gla[:, 3 * GW:4 * GW]
        gate = _mm(misc_ref[g], lr_ref[LR_GLA_GK])
        yield
        la = jax.nn.log_sigmoid(gate + vrow(V_GLA_B)) * (1.0 / GLA_GATE_NORM)
        bcum = _mm_sel_lhs(tri, la)
        yield
        blast = bcum[C - 1:C, :]
        q_in = q * jnp.exp(bcum)
        k_in = k * jnp.exp(-bcum)
        k_out = k * jnp.exp(blast - bcum)
        st = [s_gla[g, p] for p in range(2)]
        qs = [_mm_nt(q_in[:, psl(p)], st[p]) for p in range(2)]
        a_h = [_mm_nt(q_in[:, psl(i // 2)], head(k_in[:, psl(i // 2)], i % 2)) for i in range(4)]
        for p in range(2):
            sl = psl(p)
            st_new = st[p] * jnp.exp(blast[:, sl]) + _mm_tn(v[:, sl], k_out[:, sl])
            s_gla[g, p] = jnp.where(bd, st_new, 0.0)
        yield
        o_pairs = []
        for p in range(2):
            vp = v[:, psl(p)]
            op = qs[p]
            for h in range(2):
                op = op + _mm(jnp.where(incl, a_h[2 * p + h], 0.0), head(vp, h))
            o_pairs.append(op)
        yield
        o = jnp.concatenate(o_pairs, axis=1)
        ss = hsum(o * o)
        yield
        o = o * lax.rsqrt(ss * (1.0 / HD) + EPS) * vrow(V_GLA_NW)
        y_ref[g, :, GW:2 * GW] = o * _silu(gz)

    def dn(g):
        misc = misc_ref[g]
        g_dn = -jnp.exp(v128_ref[P_DN_ALOG:P_DN_ALOG + 1, :]) * jax.nn.softplus(
            misc + v128_ref[P_DN_DTB:P_DN_DTB + 1, :])
        dt_ss = jax.nn.softplus(misc + v128_ref[P_SSM_DTB:P_SSM_DTB + 1, :])
        la_ss = dt_ss * -jnp.exp(v128_ref[P_SSM_ALOG:P_SSM_ALOG + 1, :])
        beta = jax.nn.sigmoid(misc)
        cum = _mm_sel_lhs(tri, jnp.where(lane < MISC_DN_B, g_dn, la_ss))
        u_dn = u_dn_ref[g]
        xp_dn[g, SUBLANES:SUBLANES + C, :] = u_dn[:, 0:CONV_COLS]
        qkv = None
        for i in range(4):
            term = (xp_dn[g, SUBLANES - 3 + i:SUBLANES - 3 + i + C, :]
                    * conv_ref[CV_DN_W + i:CV_DN_W + i + 1, :])
            qkv = term if qkv is None else qkv + term
        xp_dn[g, 0:SUBLANES, :] = xp_dn[g, C:C + SUBLANES, :]
        qkv = _silu(qkv)
        q = qkv[:, 0:GW]
        k = qkv[:, GW:2 * GW]
        v = qkv[:, 2 * GW:3 * GW]
        z = u_dn[:, CONV_COLS:CONV_COLS + GW]
        q_ss = hsum(q * q)
        k_ss = hsum(k * k)
        yield
        cum_t = cum.T
        shared[g] = (cum, cum_t, dt_ss)
        q = q * lax.rsqrt(q_ss + EPS) * (HD ** -0.5)
        k = k * lax.rsqrt(k_ss + EPS)
        beta_x = expand(beta, MISC_DN_B)
        cum_x = expand(cum, MISC_DN_A)
        s = [s_dn[g, p] for p in range(2)]
        ecx, kb, vb, kbe, qs, kq = [], [], [], [], [], []
        for p in range(2):
            sl = psl(p)
            ecx.append(jnp.exp(cum_x[p]))
            kb.append(k[:, sl] * beta_x[p])
            vb.append(v[:, sl] * beta_x[p])
            kbe.append(kb[p] * ecx[p])
            qs.append(_mm(q[:, sl] * ecx[p], s[p]))
            for h in range(2):
                kq.append(_mm_nt(rows(kb[p], q[:, sl]), head(k[:, sl], h)))
        yield
        n_m, a_l = [], []
        for i in range(4):
            dec = _seg_decay(cum[:, MISC_DN_A + i:MISC_DN_A + i + 1],
                             cum_t[MISC_DN_A + i:MISC_DN_A + i + 1, :], incl)
            n_m.append(jnp.where(strict, -(kq[i][0:C] * dec), 0.0))
            a_l.append(kq[i][C:2 * C] * dec)
        tinv = None
        for step in neumann(n_m):
            if step is None:
                yield
            else:
                tinv = step
        yield
        uw = [_mm(tinv[i], jnp.concatenate([head(vb[i // 2], i % 2), head(kbe[i // 2], i % 2)], axis=1))
              for i in range(4)]
        yield
        u_p = [uw[2 * p][:, 0:LANES] + uw[2 * p + 1][:, 0:LANES] for p in range(2)]
        w_p = [uw[2 * p][:, LANES:2 * LANES] + uw[2 * p + 1][:, LANES:2 * LANES] for p in range(2)]
        ws = [_mm(w_p[p], s[p]) for p in range(2)]
        yield
        o_pairs = []
        for p in range(2):
            sl = psl(p)
            cx = cum_x[p]
            last = cx[C - 1:C, :]
            v_new = u_p[p] - ws[p]
            op = qs[p] + _mm(a_l[2 * p], head(v_new, 0)) + _mm(a_l[2 * p + 1], head(v_new, 1))
            s_new = s[p] * jnp.exp(last) + _mm_tn(k[:, sl] * jnp.exp(last - cx), v_new)
            s_dn[g, p] = jnp.where(bd, s_new, 0.0)
            o_pairs.append(op)
        yield
        o = jnp.concatenate(o_pairs, axis=1)
        ss = hsum(o * o)
        yield
        o = o * lax.rsqrt(ss * (1.0 / HD) + EPS) * vrow(V_DN_NW)
        y_ref[g, :, 2 * GW:3 * GW] = o * _silu(z)

    def ssd(g):
        u_ssm = u_ssm_ref[g]
        z = u_ssm[:, 0:GW]
        xp_ss[g, SUBLANES:SUBLANES + C, :] = u_ssm[:, GW:GW + CONV_COLS]
        xbc = conv_ref[CV_SSM_B:CV_SSM_B + 1, :]
        for i in range(4):
            xbc = xbc + (xp_ss[g, SUBLANES - 3 + i:SUBLANES - 3 + i + C, :]
                         * conv_ref[CV_SSM_W + i:CV_SSM_W + i + 1, :])
        xp_ss[g, 0:SUBLANES, :] = xp_ss[g, C:C + SUBLANES, :]
        xbc = _silu(xbc)
        xs_ = xbc[:, 0:GW]
        bm = xbc[:, GW:2 * GW]
        cm = xbc[:, 2 * GW:3 * GW]
        s = [s_ssm[g, p] for p in range(2)]
        gmat = [_mm_nt(cm[:, psl(p)], bm[:, psl(p)]) for p in range(2)]
        cs = [_mm_nt(cm[:, psl(p)], s[p]) for p in range(2)]
        yield
        yield
        cum, cum_t, dt_ss = shared[g]
        dt_x = expand(dt_ss, MISC_SSM_DT)
        cum_x = expand(cum, MISC_SSM_DT)
        y_pairs = []
        for p in range(2):
            sl = psl(p)
            xh = xs_[:, sl]
            xdt = xh * dt_x[p]
            cx = cum_x[p]
            last = cx[C - 1:C, :]
            yp = jnp.exp(cx) * cs[p] + v256_ref[V_SSM_D:V_SSM_D + 1, sl] * xh
            for h in range(2):
                i = MISC_SSM_DT + 2 * p + h
                dec = _seg_decay(cum[:, i:i + 1], cum_t[i:i + 1, :], incl)
                yp = yp + _mm(gmat[p] * dec, head(xdt, h))
            lane0 = MISC_SSM_DT + 2 * p
            dcol = rows(jnp.broadcast_to(jnp.exp(cum[C - 1:C, lane0:lane0 + 1]), (HD, SSM_STATE)),
                        jnp.broadcast_to(jnp.exp(cum[C - 1:C, lane0 + 1:lane0 + 2]), (HD, SSM_STATE)))
            s_ssm[g, p] = s[p] * dcol + _mm_tn(xdt * jnp.exp(last - cx), bm[:, sl])
            y_pairs.append(yp)
        yield
        y = jnp.concatenate(y_pairs, axis=1) * _silu(z)
        ss = _mm_sel_rhs(y * y, blk128)
        yield
        y_ref[g, :, 3 * GW:4 * GW] = y * lax.rsqrt(ss * (1.0 / (2 * HD)) + EPS) * vrow(V_SSM_NW)

    active = []
    for g in range(G):
        active += [rwkv(g), dn(g), gla(g), ssd(g)]
    while active:
        alive = []
        for gen in active:
            try:
                next(gen)
                alive.append(gen)
            except StopIteration:
                pass
        active = alive

    @pl.when(t == n_t - 1)
    def _fin():
        for g in range(G):
            shift1_ref[g] = xp_rw[g, 0:SUBLANES, :]
            dnc1_ref[g] = xp_dn[g, 0:SUBLANES, :]
            ssc1_ref[g] = xp_ss[g, 0:SUBLANES, :]
            for p in range(2):
                sw = s_wkv[g, p]
                sg = s_gla[g, p].T
                sd = s_dn[g, p]
                ss = s_ssm[g, p]
                for h in range(2):
                    hs = slice(HD * h, HD * (h + 1))
                    wkv1_ref[g, 2 * p + h] = sw[hs, hs]
                    gla1_ref[g, 2 * p + h] = sg[hs, hs]
                    dn1_ref[g, 2 * p + h] = sd[hs, hs]
                    ssm1_ref[g, 2 * p + h] = ss[hs, :]


def _mixer_call(u_parts, states, mparams, layer):
    u_rw, u_gla, u_dn, u_ssm, misc = u_parts
    shift0, wkv0, gla0, dnc0, dn0, ssc0, ssm0 = states
    bsz, tlen, _ = u_rw.shape
    C = math.gcd(tlen, PROMPT_CHUNK)
    n_t = tlen // C
    G = SEQS_PER_STEP_LONG if n_t > 1 else SEQS_PER_STEP_SHORT
    assert bsz % G == 0

    def tmap(b, t):
        return (b, t, 0)

    def bmap3(b, t):
        return (b, 0, 0)

    def bmap4(b, t):
        return (b, 0, 0, 0)

    def full(arr):
        nd = arr.ndim
        return pl.BlockSpec(arr.shape, lambda b, t: (0,) * nd)

    def lmap3(b, t):
        return (layer, b, 0, 0)

    def lmap4(b, t):
        return (layer, b, 0, 0, 0)

    sq = (G, NH, HD, HD)
    in_state_specs = [pl.BlockSpec((None, G, SUBLANES, RWKV_COLS), lmap3), pl.BlockSpec((None,) + sq, lmap4),
                      pl.BlockSpec((None,) + sq, lmap4), pl.BlockSpec((None, G, SUBLANES, CONV_COLS), lmap3),
                      pl.BlockSpec((None,) + sq, lmap4), pl.BlockSpec((None, G, SUBLANES, CONV_COLS), lmap3),
                      pl.BlockSpec((None, G, NH, HD, SSM_STATE), lmap4)]
    state_specs = [pl.BlockSpec((G, SUBLANES, RWKV_COLS), bmap3), pl.BlockSpec(sq, bmap4),
                   pl.BlockSpec(sq, bmap4), pl.BlockSpec((G, SUBLANES, CONV_COLS), bmap3),
                   pl.BlockSpec(sq, bmap4), pl.BlockSpec((G, SUBLANES, CONV_COLS), bmap3),
                   pl.BlockSpec((G, NH, HD, SSM_STATE), bmap4)]
    state_shapes = [jax.ShapeDtypeStruct((bsz, SUBLANES, RWKV_COLS), F32),
                    jax.ShapeDtypeStruct((bsz, NH, HD, HD), F32),
                    jax.ShapeDtypeStruct((bsz, NH, HD, HD), F32),
                    jax.ShapeDtypeStruct((bsz, SUBLANES, CONV_COLS), F32),
                    jax.ShapeDtypeStruct((bsz, NH, HD, HD), F32),
                    jax.ShapeDtypeStruct((bsz, SUBLANES, CONV_COLS), F32),
                    jax.ShapeDtypeStruct((bsz, NH, HD, SSM_STATE), F32)]
    outs = pl.pallas_call(
        functools.partial(_mixer_body, C, n_t, G),
        grid=(bsz // G, n_t),
        in_specs=[pl.BlockSpec((G, C, RWKV_COLS), tmap), pl.BlockSpec((G, C, 4 * GW), tmap),
                  pl.BlockSpec((G, C, 4 * GW), tmap), pl.BlockSpec((G, C, 4 * GW), tmap),
                  pl.BlockSpec((G, C, LANES), tmap)] + in_state_specs + [full(a) for a in mparams],
        out_specs=[pl.BlockSpec((G, C, D_MODEL), tmap)] + state_specs,
        out_shape=[jax.ShapeDtypeStruct((bsz, tlen, D_MODEL), F32)] + state_shapes,
        scratch_shapes=[pltpu.VMEM((G, C + SUBLANES, RWKV_COLS), F32),
                        pltpu.VMEM((G, C + SUBLANES, CONV_COLS), F32),
                        pltpu.VMEM((G, C + SUBLANES, CONV_COLS), F32),
                        pltpu.VMEM((G, 2, LANES, LANES), F32), pltpu.VMEM((G, 2, LANES, LANES), F32),
                        pltpu.VMEM((G, 2, LANES, LANES), F32), pltpu.VMEM((G, 2, LANES, SSM_STATE), F32)],
        compiler_params=pltpu.CompilerParams(vmem_limit_bytes=VMEM_LIMIT,
                                             dimension_semantics=("parallel", "arbitrary")),
        name="mixers",
    )(u_rw, u_gla, u_dn, u_ssm, misc, shift0, wkv0, gla0, dnc0, dn0, ssc0, ssm0, *mparams)
    return outs[0], tuple(outs[1:])


def _pad_rows(m, lo, total):
    return jnp.pad(m, ((lo, total - lo - m.shape[0]), (0, 0)))


def _lane_vec(vals, lo):
    return jnp.pad(vals, (lo, LANES - lo - vals.shape[0]))


def _pack_layer(P, l):
    v256 = jnp.stack([P['rwkv_w0'][l], P['rwkv_a0'][l], P['rwkv_k_k'][l], P['rwkv_k_a'][l],
                      P['rwkv_r_k'][l], P['rwkv_ln_w'][l], P['rwkv_ln_b'][l],
                      P['gla_gk_b'][l], P['gla_norm_w'][l], P['dn_norm_w'][l],
                      jnp.repeat(P['ssm_D'][l], HD), P['ssm_norm_w'][l]])
    v256 = jnp.pad(v256, ((0, 16 - v256.shape[0]), (0, 0)))
    v128 = jnp.stack([_lane_vec(P['dn_A_log'][l], MISC_DN_A), _lane_vec(P['dn_dt_bias'][l], MISC_DN_A),
                      _lane_vec(P['ssm_A_log'][l], MISC_SSM_DT), _lane_vec(P['ssm_dt_bias'][l], MISC_SSM_DT)])
    v128 = jnp.pad(v128, ((0, 4), (0, 0)))
    conv = jnp.concatenate([P['dn_conv_w'][l], P['ssm_conv_w'][l], P['ssm_conv_b'][l][None],
                            jnp.zeros((7, CONV_COLS), F32)], axis=0)
    lr = jnp.stack([_pad_rows(P['rwkv_w2'][l], 0, LANES), _pad_rows(P['rwkv_a2'][l], 32, LANES),
                    _pad_rows(P['rwkv_g2'][l], 64, LANES),
                    _pad_rows(P['gla_gk_w2'][l], MISC_GLA_GATE, LANES)]).astype(MXU_DT)
    return (P['rwkv_mu'][l][None], v256, v128, conv, lr)


def _pad_tail_rows(a):
    return jnp.pad(a, ((0, 0), (0, 0), (SUBLANES - a.shape[2], 0), (0, 0)))


def _trunk(x, mod_rows, states, mixer_params, dense, final_norm_w):
    n_layers = len(mixer_params)
    shift0, wkv0, gla0, dnc0, dn0, ssc0, ssm0 = states
    st_in = (_pad_tail_rows(shift0[:, :, None, :]), wkv0, gla0, _pad_tail_rows(dnc0), dn0,
             _pad_tail_rows(ssc0), ssm0)
    new = []
    for l in range(n_layers):
        mod = mod_rows[l][:, None, :]
        u_parts = _inproj_call(x, mod, dense['norm1'][l], dense['w_r'], l)
        y_mix, st = _mixer_call(u_parts, st_in, mixer_params[l], l)
        x = _outmlp_call(x, y_mix, mod, dense['norm2'][l], final_norm_w, dense['w_out'],
                         dense['w_up'], dense['w_down'], l, final=(l == n_layers - 1))
        shift1, wkv1, gla1, dnc1, dn1, ssc1, ssm1 = st
        new.append((shift1[:, SUBLANES - 1], wkv1, gla1, dnc1[:, SUBLANES - 3:], dn1,
                    ssc1[:, SUBLANES - 3:], ssm1))
    return x, tuple(jnp.stack([s[i] for s in new]) for i in range(7))


def kernel(x_prompt, x_sample, state_rwkv_shift, state_rwkv_wkv, state_gla, state_dn_conv,
           state_dn, state_ssm_conv, state_ssm, c_prompt, c_sample,
           ada_w, ada_b, norm1_w, norm2_w, w_in, w_out, w_up, w_down,
           rwkv_mu, rwkv_w0, rwkv_w2, rwkv_a0, rwkv_a2, rwkv_g2, rwkv_k_k, rwkv_k_a, rwkv_r_k,
           rwkv_ln_w, rwkv_ln_b, gla_gk_w2, gla_gk_b, gla_norm_w,
           dn_conv_w, dn_A_log, dn_dt_bias, dn_norm_w,
           ssm_conv_w, ssm_conv_b, ssm_dt_bias, ssm_A_log, ssm_D, ssm_norm_w, final_norm_w):
    P = dict(rwkv_mu=rwkv_mu, rwkv_w0=rwkv_w0, rwkv_w2=rwkv_w2, rwkv_a0=rwkv_a0, rwkv_a2=rwkv_a2,
             rwkv_g2=rwkv_g2, rwkv_k_k=rwkv_k_k, rwkv_k_a=rwkv_k_a, rwkv_r_k=rwkv_r_k,
             rwkv_ln_w=rwkv_ln_w, rwkv_ln_b=rwkv_ln_b, gla_gk_w2=gla_gk_w2, gla_gk_b=gla_gk_b,
             gla_norm_w=gla_norm_w, dn_conv_w=dn_conv_w, dn_A_log=dn_A_log, dn_dt_bias=dn_dt_bias,
             dn_norm_w=dn_norm_w, ssm_conv_w=ssm_conv_w, ssm_conv_b=ssm_conv_b,
             ssm_dt_bias=ssm_dt_bias, ssm_A_log=ssm_A_log, ssm_D=ssm_D, ssm_norm_w=ssm_norm_w)
    n_layers = w_in.shape[0]
    n_prompt = x_prompt.shape[0]
    mixer_params = [_pack_layer(P, l) for l in range(n_layers)]
    dense = dict(w_r=_wprep_call(w_in), w_out=w_out.astype(MXU_DT), w_up=w_up.astype(MXU_DT),
                 w_down=w_down.astype(MXU_DT), norm1=norm1_w, norm2=norm2_w)
    mod_all = _ada_call(jnp.concatenate([c_prompt, c_sample], axis=0), ada_w, ada_b)
    sample_states = (state_rwkv_shift, state_rwkv_wkv, state_gla, state_dn_conv,
                     state_dn, state_ssm_conv, state_ssm)
    prompt_states = tuple(jnp.zeros((n_layers, n_prompt) + s.shape[2:], F32) for s in sample_states)
    y_prompt, ps = _trunk(x_prompt, mod_all[:, :n_prompt], prompt_states, mixer_params, dense, final_norm_w)
    y_sample, ss = _trunk(x_sample, mod_all[:, n_prompt:], sample_states, mixer_params, dense, final_norm_w)
    return (y_prompt, y_sample) + ps + ss
```

```python
import functools
import math

import jax
import jax.numpy as jnp
from jax import lax
from jax.experimental import pallas as pl
from jax.experimental.pallas import tpu as pltpu

F32 = jnp.float32
MXU_DT = jnp.bfloat16

D_MODEL = 1024
NH = 4
HD = 64
GW = NH * HD
D_FF = 4 * D_MODEL
SSM_STATE = 128
EPS = 1e-6
RWKV_GN_EPS = 64e-5
GLA_GATE_NORM = 16.0
RWKV_COLS = 3 * GW + 32 + 32 + 64
CONV_COLS = 3 * GW
U_COLS = RWKV_COLS + 3 * 4 * GW + 128
MISC_GLA_GATE = 0
MISC_DN_A = 16
MISC_DN_B = 20
MISC_SSM_DT = 24

LANES = 128
SUBLANES = 8
VMEM_LIMIT = 56 * 1024 * 1024
PROMPT_CHUNK = 64
ROW_TILE = 256
SEQS_PER_STEP_LONG = 4
SEQS_PER_STEP_SHORT = 8


def _mm(a, b):
    return jnp.dot(a.astype(MXU_DT), b.astype(MXU_DT), preferred_element_type=F32)


def _mm_nt(a, b):
    return lax.dot_general(a.astype(MXU_DT), b.astype(MXU_DT), (((1,), (1,)), ((), ())),
                           preferred_element_type=F32)


def _mm_tn(a, b):
    return lax.dot_general(a.astype(MXU_DT), b.astype(MXU_DT), (((0,), (0,)), ((), ())),
                           preferred_element_type=F32)


def _mm3(a, b):
    a_hi, a_lo = _split(a, 2)
    b_hi, b_lo = _split(b, 2)
    return (jnp.dot(a_hi, b_hi, preferred_element_type=F32) + jnp.dot(a_hi, b_lo, preferred_element_type=F32)
            + jnp.dot(a_lo, b_hi, preferred_element_type=F32))


def _split(x, n):
    parts = []
    r = x
    for i in range(n):
        p = r.astype(MXU_DT)
        parts.append(p)
        if i + 1 < n:
            r = r - p.astype(F32)
    return parts


def _mm_sel_lhs(sel, x, n=2):
    acc = None
    for p in _split(x, n):
        d = jnp.dot(sel, p, preferred_element_type=F32)
        acc = d if acc is None else acc + d
    return acc


def _mm_sel_rhs(x, sel, n=2):
    acc = None
    for p in _split(x, n):
        d = jnp.dot(p, sel, preferred_element_type=F32)
        acc = d if acc is None else acc + d
    return acc


def _seg_decay(cum_col, cum_row, incl):
    d = cum_col - cum_row
    return jnp.where(incl, jnp.exp(jnp.where(incl, d, 0.0)), 0.0)


def _silu(x):
    return x * jax.nn.sigmoid(x)


def _ada_body(c_ref, w_ref, b_ref, o_ref):
    c = c_ref[...]
    o_ref[0] = _mm(_silu(c), w_ref[0]) + b_ref[0]


def _ada_call(c_all, ada_w, ada_b):
    n_layers = ada_w.shape[0]
    rows = c_all.shape[0]
    tn = 1536
    return pl.pallas_call(
        _ada_body,
        grid=(n_layers, 6 * D_MODEL // tn),
        in_specs=[pl.BlockSpec((rows, D_MODEL), lambda l, j: (0, 0)),
                  pl.BlockSpec((1, D_MODEL, tn), lambda l, j: (l, 0, j)),
                  pl.BlockSpec((1, 1, tn), lambda l, j: (l, 0, j))],
        out_specs=pl.BlockSpec((1, rows, tn), lambda l, j: (l, 0, j)),
        out_shape=jax.ShapeDtypeStruct((n_layers, rows, 6 * D_MODEL), F32),
        compiler_params=pltpu.CompilerParams(vmem_limit_bytes=VMEM_LIMIT),
        name="ada_mod",
    )(c_all, ada_w, ada_b.reshape(n_layers, 1, 6 * D_MODEL))


W_IN_COLS = RWKV_COLS + (4 * GW + 16) + (4 * GW + 8) + (4 * GW + 4)


def _wprep_body(w_ref, o_ref):
    w = w_ref[0]
    o_gla = RWKV_COLS
    o_dn = o_gla + 4 * GW + 16
    o_ssm = o_dn + 4 * GW + 8
    parts = [w[:, 0:o_gla + 4 * GW],
             w[:, o_dn:o_dn + 4 * GW],
             w[:, o_ssm:o_ssm + 4 * GW],
             w[:, o_gla + 4 * GW:o_dn],
             w[:, o_dn + 4 * GW:o_ssm],
             w[:, o_ssm + 4 * GW:W_IN_COLS],
             jnp.zeros((w.shape[0], LANES - 28), F32)]
    o_ref[0] = jnp.concatenate(parts, axis=1).astype(MXU_DT)


def _wprep_call(w_in):
    n_layers = w_in.shape[0]
    tr = 128
    return pl.pallas_call(
        _wprep_body,
        grid=(n_layers, D_MODEL // tr),
        in_specs=[pl.BlockSpec((1, tr, W_IN_COLS), lambda l, i: (l, i, 0))],
        out_specs=pl.BlockSpec((1, tr, U_COLS), lambda l, i: (l, i, 0)),
        out_shape=jax.ShapeDtypeStruct((n_layers, D_MODEL, U_COLS), MXU_DT),
        compiler_params=pltpu.CompilerParams(vmem_limit_bytes=VMEM_LIMIT,
                                             dimension_semantics=("parallel", "parallel")),
        name="w_in_relayout",
    )(w_in)


def _rms(x):
    return x * lax.rsqrt(jnp.mean(x * x, -1, keepdims=True) + EPS)


def _inproj_body(x_ref, sh_ref, sc_ref, nw_ref, w_ref, o_rw, o_gla, o_dn, o_ssm, o_misc):
    bb, tt, _ = x_ref.shape
    h = _rms(x_ref[...]) * nw_ref[...]
    h = h * (1.0 + sc_ref[...]) + sh_ref[...]
    u = jnp.dot(h.reshape(bb * tt, D_MODEL).astype(MXU_DT), w_ref[...], preferred_element_type=F32)
    off = 0
    for ref in (o_rw, o_gla, o_dn, o_ssm, o_misc):
        w = ref.shape[-1]
        ref[...] = u[:, off:off + w].reshape(bb, tt, w)
        off += w


def _row_blocks(bsz, tlen):
    tt = min(tlen, ROW_TILE)
    bb = ROW_TILE // tt
    assert tlen % tt == 0 and bsz % bb == 0
    return bb, tt


def _inproj_call(x, mod, norm_w, w_r, layer):
    bsz, tlen, _ = x.shape
    bb, tt = _row_blocks(bsz, tlen)
    widths = (RWKV_COLS, 4 * GW, 4 * GW, 4 * GW, LANES)

    def xmap(i, j):
        return (i, j, 0)

    return pl.pallas_call(
        _inproj_body,
        grid=(bsz // bb, tlen // tt),
        in_specs=[pl.BlockSpec((bb, tt, D_MODEL), xmap),
                  pl.BlockSpec((bb, 1, D_MODEL), lambda i, j: (i, 0, 0)),
                  pl.BlockSpec((bb, 1, D_MODEL), lambda i, j: (i, 0, 1)),
                  pl.BlockSpec((1, 1, D_MODEL), lambda i, j: (0, 0, 0)),
                  pl.BlockSpec((None, D_MODEL, U_COLS), lambda i, j: (layer, 0, 0),
                               pipeline_mode=pl.Buffered(1))],
        out_specs=[pl.BlockSpec((bb, tt, w), xmap) for w in widths],
        out_shape=[jax.ShapeDtypeStruct((bsz, tlen, w), F32) for w in widths],
        compiler_params=pltpu.CompilerParams(vmem_limit_bytes=VMEM_LIMIT,
                                             dimension_semantics=("parallel", "parallel")),
        name="in_proj",
    )(x, mod, mod, norm_w.reshape(1, 1, D_MODEL), w_r)


def _outmlp_body(final, x_ref, y_ref, gt1_ref, sh_ref, sc_ref, gt2_ref, nw_ref, fnw_ref,
                 wo_ref, wu_ref, wd_ref, o_ref):
    bb, tt, _ = x_ref.shape
    rows = bb * tt
    att = jnp.dot(y_ref[...].reshape(rows, D_MODEL).astype(MXU_DT), wo_ref[...],
                  preferred_element_type=F32)
    x1 = x_ref[...] + gt1_ref[...] * att.reshape(bb, tt, D_MODEL)
    h = _rms(x1) * nw_ref[...]
    h = h * (1.0 + sc_ref[...]) + sh_ref[...]
    a = jnp.dot(h.reshape(rows, D_MODEL).astype(MXU_DT), wu_ref[...], preferred_element_type=F32)
    a = jnp.square(jnp.maximum(a, 0.0))
    f = jnp.dot(a.astype(MXU_DT), wd_ref[...], preferred_element_type=F32)
    x2 = x1 + gt2_ref[...] * f.reshape(bb, tt, D_MODEL)
    if final:
        x2 = _rms(x2) * fnw_ref[...]
    o_ref[...] = x2


def _outmlp_call(x, y_mix, mod, norm_w, final_norm_w, w_out, w_up, w_down, layer, final):
    bsz, tlen, _ = x.shape
    bb, tt = _row_blocks(bsz, tlen)

    def xmap(i, j):
        return (i, j, 0)

    def modspec(k):
        return pl.BlockSpec((bb, 1, D_MODEL), lambda i, j: (i, 0, k))

    def wspec(shape):
        return pl.BlockSpec((None,) + shape, lambda i, j: (layer, 0, 0), pipeline_mode=pl.Buffered(1))

    vec = pl.BlockSpec((1, 1, D_MODEL), lambda i, j: (0, 0, 0))
    return pl.pallas_call(
        functools.partial(_outmlp_body, final),
        grid=(bsz // bb, tlen // tt),
        in_specs=[pl.BlockSpec((bb, tt, D_MODEL), xmap), pl.BlockSpec((bb, tt, D_MODEL), xmap),
                  modspec(2), modspec(3), modspec(4), modspec(5), vec, vec,
                  wspec((D_MODEL, D_MODEL)), wspec((D_MODEL, D_FF)), wspec((D_FF, D_MODEL))],
        out_specs=pl.BlockSpec((bb, tt, D_MODEL), xmap),
        out_shape=jax.ShapeDtypeStruct((bsz, tlen, D_MODEL), F32),
        compiler_params=pltpu.CompilerParams(vmem_limit_bytes=VMEM_LIMIT,
                                             dimension_semantics=("parallel", "parallel")),
        name="out_mlp",
    )(x, y_mix, mod, mod, mod, mod, norm_w.reshape(1, 1, D_MODEL),
      final_norm_w.reshape(1, 1, D_MODEL), w_out, w_up, w_down)


(V_RW_W0, V_RW_A0, V_RW_KK, V_RW_KA, V_RW_RK, V_RW_LNW, V_RW_LNB,
 V_GLA_B, V_GLA_NW, V_DN_NW, V_SSM_D, V_SSM_NW) = range(12)
P_DN_ALOG, P_DN_DTB, P_SSM_ALOG, P_SSM_DTB = range(4)
CV_DN_W, CV_SSM_W, CV_SSM_B = 0, 4, 8
LR_RW_W2, LR_RW_A2, LR_RW_G2, LR_GLA_GK = range(4)


def _blockdiag(a, b):
    za = jnp.zeros_like(a)
    return jnp.concatenate([jnp.concatenate([a, za], axis=1),
                            jnp.concatenate([za, b], axis=1)], axis=0)


def _mixer_body(C, n_t, G,
                u_rw_ref, u_gla_ref, u_dn_ref, u_ssm_ref, misc_ref,
                shift0_ref, wkv0_ref, gla0_ref, dnc0_ref, dn0_ref, ssc0_ref, ssm0_ref,
                mu_ref, v256_ref, v128_ref, conv_ref, lr_ref,
                y_ref, shift1_ref, wkv1_ref, gla1_ref, dnc1_ref, dn1_ref, ssc1_ref, ssm1_ref,
                xp_rw, xp_dn, xp_ss, s_wkv, s_gla, s_dn, s_ssm):
    t = pl.program_id(1)

    @pl.when(t == 0)
    def _init():
        for g in range(G):
            xp_rw[g, 0:SUBLANES, :] = shift0_ref[g]
            xp_dn[g, 0:SUBLANES, :] = dnc0_ref[g]
            xp_ss[g, 0:SUBLANES, :] = ssc0_ref[g]
            for p in range(2):
                s_wkv[g, p] = _blockdiag(wkv0_ref[g, 2 * p], wkv0_ref[g, 2 * p + 1])
                s_gla[g, p] = _blockdiag(gla0_ref[g, 2 * p], gla0_ref[g, 2 * p + 1]).T
                s_dn[g, p] = _blockdiag(dn0_ref[g, 2 * p], dn0_ref[g, 2 * p + 1])
                s_ssm[g, p] = jnp.concatenate([ssm0_ref[g, 2 * p], ssm0_ref[g, 2 * p + 1]], axis=0)

    ri = lax.broadcasted_iota(jnp.int32, (C, C), 0)
    ci = lax.broadcasted_iota(jnp.int32, (C, C), 1)
    incl = ci <= ri
    strict = ci < ri
    tri = jnp.where(incl, 1.0, 0.0).astype(MXU_DT)
    lane = lax.broadcasted_iota(jnp.int32, (1, LANES), 1)
    hmask = (lane < HD, lane >= HD)
    r128 = lax.shift_right_logical(lax.broadcasted_iota(jnp.int32, (LANES, LANES), 0), 6)
    c128 = lax.shift_right_logical(lax.broadcasted_iota(jnp.int32, (LANES, LANES), 1), 6)
    bd = r128 == c128
    r256 = lax.broadcasted_iota(jnp.int32, (GW, GW), 0)
    c256 = lax.broadcasted_iota(jnp.int32, (GW, GW), 1)
    blk64 = jnp.where(lax.shift_right_logical(r256, 6) == lax.shift_right_logical(c256, 6),
                      1.0, 0.0).astype(MXU_DT)
    blk128 = jnp.where(lax.shift_right_logical(r256, 7) == lax.shift_right_logical(c256, 7),
                       1.0, 0.0).astype(MXU_DT)
    n_iter = 0
    while 2 * (1 << n_iter) < C:
        n_iter += 1

    def vrow(i):
        return v256_ref[i:i + 1, :]

    def hsum(x):
        return _mm_sel_rhs(x, blk64, n=1)

    def head(x, h):
        return jnp.where(hmask[h], x, 0.0)

    def expand(tile, base):
        cols = [jnp.broadcast_to(tile[:, base + h:base + h + 1], (C, LANES)) for h in range(NH)]
        return [jnp.where(hmask[0], cols[0], cols[1]), jnp.where(hmask[0], cols[2], cols[3])]

    def psl(p):
        return slice(LANES * p, LANES * (p + 1))

    def rows(*xs):
        return jnp.concatenate(xs, axis=0)

    shared = {}

    def neumann(n_list):
        rs = list(n_list)
        ps = list(n_list)
        for _ in range(n_iter):
            ps = [_mm(p, p) for p in ps]
            yield None
            rs = [r + p + _mm(r, p) for r, p in zip(rs, ps)]
        yield rs

    def rwkv(g):
        u_rw = u_rw_ref[g]
        xp_rw[g, SUBLANES:SUBLANES + C, :] = u_rw
        u_prev = xp_rw[g, SUBLANES - 1:SUBLANES - 1 + C, :]
        xp_rw[g, 0:SUBLANES, :] = xp_rw[g, C:C + SUBLANES, :]
        xs = u_rw + (u_prev - u_rw) * mu_ref[...]
        r = xs[:, 0:GW]
        k = xs[:, GW:2 * GW]
        v = xs[:, 2 * GW:3 * GW]
        x7 = xs[:, 3 * GW:RWKV_COLS]
        w_pre = _mm(jnp.tanh(x7), lr_ref[LR_RW_W2])
        a_pre = _mm(x7, lr_ref[LR_RW_A2])
        gate = _mm(jax.nn.sigmoid(x7), lr_ref[LR_RW_G2])
        kk = k * vrow(V_RW_KK)
        kk_ss = hsum(kk * kk)
        yield
        log_w = -jax.nn.softplus(-(vrow(V_RW_W0) + w_pre)) - 0.5
        lw = -jnp.exp(log_w)
        a = jax.nn.sigmoid(vrow(V_RW_A0) + a_pre)
        kk = kk * lax.rsqrt(kk_ss + EPS)
        k2 = k * (1.0 + (a - 1.0) * vrow(V_RW_KA))
        bcum = _mm_sel_lhs(tri, lw)
        bonus = hsum(r * k2 * vrow(V_RW_RK))
        yield
        e_nb = jnp.exp(-bcum)
        blast = bcum[C - 1:C, :]
        e_rem = jnp.exp(blast - bcum)
        nka = -kk * a
        r_t = r * jnp.exp(bcum)
        a_t = kk * jnp.exp(bcum - lw)
        b_t = nka * e_nb
        k_t = k2 * e_nb
        k_rem = k2 * e_rem
        b_rem = nka * e_rem
        s = [s_wkv[g, p] for p in range(2)]
        ars = [_mm_nt(rows(a_t[:, psl(p)], r_t[:, psl(p)]), s[p]) for p in range(2)]
        n_ab, n_ak, a_r = [], [], []
        for p in range(2):
            sl = psl(p)
            for h in range(2):
                bh = head(b_t[:, sl], h)
                kh = head(k_t[:, sl], h)
                n_ab.append(_mm_nt(a_t[:, sl], bh))
                n_ak.append(_mm_nt(a_t[:, sl], kh))
                a_r.append((_mm_nt(r_t[:, sl], bh), _mm_nt(r_t[:, sl], kh)))
        yield
        n_ab = [jnp.where(strict, n, 0.0) for n in n_ab]
        n_ak = [jnp.where(strict, n, 0.0) for n in n_ak]
        a_r = [(jnp.where(incl, x, 0.0), jnp.where(incl, y, 0.0)) for x, y in a_r]
        rhs = [head(ars[i // 2][0:C], i % 2) + _mm(n_ak[i], head(v[:, psl(i // 2)], i % 2))
               for i in range(4)]
        tinv = None
        for step in neumann(n_ab):
            if step is None:
                yield
            else:
                tinv = step
        yield
        uh = [rhs[i] + _mm(tinv[i], rhs[i]) for i in range(4)]
        yield
        y_pairs = []
        for p in range(2):
            sl = psl(p)
            vp = v[:, sl]
            u_p = uh[2 * p] + uh[2 * p + 1]
            yp = ars[p][C:2 * C]
            for h in range(2):
                yp = yp + _mm(a_r[2 * p + h][0], head(u_p, h)) + _mm(a_r[2 * p + h][1], head(vp, h))
            s_new = s[p] * jnp.exp(blast[:, sl]) + _mm_tn(rows(vp, u_p), rows(k_rem[:, sl], b_rem[:, sl]))
            s_wkv[g, p] = jnp.where(bd, s_new, 0.0)
            y_pairs.append(yp)
        yield
        y = jnp.concatenate(y_pairs, axis=1)
        mean = hsum(y) * (1.0 / HD)
        yield
        yc = y - mean
        var = hsum(yc * yc) * (1.0 / HD)
        yield
        y = yc * lax.rsqrt(var + RWKV_GN_EPS) * vrow(V_RW_LNW) + vrow(V_RW_LNB)
        y_ref[g, :, 0:GW] = (y + bonus * v) * gate

    def gla(g):
        u_gla = u_gla_ref[g]
        q = u_gla[:, 0:GW] * (HD ** -0.5)
        k = u_gla[:, GW:2 * GW]
        v = u_gla[:, 2 * GW:3 * GW]
        gz = u_gla[:, 3 * GW:4 * GW]
        gate = _mm(misc_ref[g], lr_ref[LR_GLA_GK])
        yield
        la = jax.nn.log_sigmoid(gate + vrow(V_GLA_B)) * (1.0 / GLA_GATE_NORM)
        bcum = _mm_sel_lhs(tri, la)
        yield
        blast = bcum[C - 1:C, :]
        q_in = q * jnp.exp(bcum)
        k_in = k * jnp.exp(-bcum)
        k_out = k * jnp.exp(blast - bcum)
        st = [s_gla[g, p] for p in range(2)]
        qs = [_mm_nt(q_in[:, psl(p)], st[p]) for p in range(2)]
        a_h = [_mm_nt(q_in[:, psl(i // 2)], head(k_in[:, psl(i // 2)], i % 2)) for i in range(4)]
        for p in range(2):
            sl = psl(p)
            st_new = st[p] * jnp.exp(blast[:, sl]) + _mm_tn(v[:, sl], k_out[:, sl])
            s_gla[g, p] = jnp.where(bd, st_new, 0.0)
        yield
        o_pairs = []
        for p in range(2):
            vp = v[:, psl(p)]
            op = qs[p]
            for h in range(2):
                op = op + _mm(jnp.where(incl, a_h[2 * p + h], 0.0), head(vp, h))
            o_pairs.append(op)
        yield
        o = jnp.concatenate(o_pairs, axis=1)
        ss = hsum(o * o)
        yield
        o = o * lax.rsqrt(ss * (1.0 / HD) + EPS) * vrow(V_GLA_NW)
        y_ref[g, :, GW:2 * GW] = o * _silu(gz)

    def dn(g):
        misc = misc_ref[g]
        g_dn = -jnp.exp(v128_ref[P_DN_ALOG:P_DN_ALOG + 1, :]) * jax.nn.softplus(
            misc + v128_ref[P_DN_DTB:P_DN_DTB + 1, :])
        dt_ss = jax.nn.softplus(misc + v128_ref[P_SSM_DTB:P_SSM_DTB + 1, :])
        la_ss = dt_ss * -jnp.exp(v128_ref[P_SSM_ALOG:P_SSM_ALOG + 1, :])
        beta = jax.nn.sigmoid(misc)
        cum = _mm_sel_lhs(tri, jnp.where(lane < MISC_DN_B, g_dn, la_ss))
        u_dn = u_dn_ref[g]
        xp_dn[g, SUBLANES:SUBLANES + C, :] = u_dn[:, 0:CONV_COLS]
        qkv = None
        for i in range(4):
            term = (xp_dn[g, SUBLANES - 3 + i:SUBLANES - 3 + i + C, :]
                    * conv_ref[CV_DN_W + i:CV_DN_W + i + 1, :])
            qkv = term if qkv is None else qkv + term
        xp_dn[g, 0:SUBLANES, :] = xp_dn[g, C:C + SUBLANES, :]
        qkv = _silu(qkv)
        q = qkv[:, 0:GW]
        k = qkv[:, GW:2 * GW]
        v = qkv[:, 2 * GW:3 * GW]
        z = u_dn[:, CONV_COLS:CONV_COLS + GW]
        q_ss = hsum(q * q)
        k_ss = hsum(k * k)
        yield
        cum_t = cum.T
        shared[g] = (cum, cum_t, dt_ss)
        q = q * lax.rsqrt(q_ss + EPS) * (HD ** -0.5)
        k = k * lax.rsqrt(k_ss + EPS)
        beta_x = expand(beta, MISC_DN_B)
        cum_x = expand(cum, MISC_DN_A)
        s = [s_dn[g, p] for p in range(2)]
        ecx, kb, vb, kbe, qs, kq = [], [], [], [], [], []
        for p in range(2):
            sl = psl(p)
            ecx.append(jnp.exp(cum_x[p]))
            kb.append(k[:, sl] * beta_x[p])
            vb.append(v[:, sl] * beta_x[p])
            kbe.append(kb[p] * ecx[p])
            qs.append(_mm(q[:, sl] * ecx[p], s[p]))
            for h in range(2):
                kq.append(_mm_nt(rows(kb[p], q[:, sl]), head(k[:, sl], h)))
        yield
        n_m, a_l = [], []
        for i in range(4):
            dec = _seg_decay(cum[:, MISC_DN_A + i:MISC_DN_A + i + 1],
                             cum_t[MISC_DN_A + i:MISC_DN_A + i + 1, :], incl)
            n_m.append(jnp.where(strict, -(kq[i][0:C] * dec), 0.0))
            a_l.append(kq[i][C:2 * C] * dec)
        tinv = None
        for step in neumann(n_m):
            if step is None:
                yield
            else:
                tinv = step
        yield
        err = [n_m[i] - tinv[i] + _mm3(n_m[i], tinv[i]) for i in range(4)]
        yield
        tinv = [tinv[i] + err[i] + _mm(tinv[i], err[i]) for i in range(4)]
        yield
        vk = [jnp.concatenate([head(vb[i // 2], i % 2), head(kbe[i // 2], i % 2)], axis=1) for i in range(4)]
        uw = [vk[i] + _mm(tinv[i], vk[i]) for i in range(4)]
        yield
        u_p = [uw[2 * p][:, 0:LANES] + uw[2 * p + 1][:, 0:LANES] for p in range(2)]
        w_p = [uw[2 * p][:, LANES:2 * LANES] + uw[2 * p + 1][:, LANES:2 * LANES] for p in range(2)]
        ws = [_mm(w_p[p], s[p]) for p in range(2)]
        yield
        o_pairs = []
        for p in range(2):
            sl = psl(p)
            cx = cum_x[p]
            last = cx[C - 1:C, :]
            v_new = u_p[p] - ws[p]
            op = qs[p] + _mm(a_l[2 * p], head(v_new, 0)) + _mm(a_l[2 * p + 1], head(v_new, 1))
            s_new = s[p] * jnp.exp(last) + _mm_tn(k[:, sl] * jnp.exp(last - cx), v_new)
            s_dn[g, p] = jnp.where(bd, s_new, 0.0)
            o_pairs.append(op)
        yield
        o = jnp.concatenate(o_pairs, axis=1)
        ss = hsum(o * o)
        yield
        o = o * lax.rsqrt(ss * (1.0 / HD) + EPS) * vrow(V_DN_NW)
        y_ref[g, :, 2 * GW:3 * GW] = o * _silu(z)

    def ssd(g):
        u_ssm = u_ssm_ref[g]
        z = u_ssm[:, 0:GW]
        xp_ss[g, SUBLANES:SUBLANES + C, :] = u_ssm[:, GW:GW + CONV_COLS]
        xbc = conv_ref[CV_SSM_B:CV_SSM_B + 1, :]
        for i in range(4):
            xbc = xbc + (xp_ss[g, SUBLANES - 3 + i:SUBLANES - 3 + i + C, :]
                         * conv_ref[CV_SSM_W + i:CV_SSM_W + i + 1, :])
        xp_ss[g, 0:SUBLANES, :] = xp_ss[g, C:C + SUBLANES, :]
        xbc = _silu(xbc)
        xs_ = xbc[:, 0:GW]
        bm = xbc[:, GW:2 * GW]
        cm = xbc[:, 2 * GW:3 * GW]
        s = [s_ssm[g, p] for p in range(2)]
        gmat = [_mm_nt(cm[:, psl(p)], bm[:, psl(p)]) for p in range(2)]
        cs = [_mm_nt(cm[:, psl(p)], s[p]) for p in range(2)]
        yield
        yield
        cum, cum_t, dt_ss = shared[g]
        dt_x = expand(dt_ss, MISC_SSM_DT)
        cum_x = expand(cum, MISC_SSM_DT)
        y_pairs = []
        for p in range(2):
            sl = psl(p)
            xh = xs_[:, sl]
            xdt = xh * dt_x[p]
            cx = cum_x[p]
            last = cx[C - 1:C, :]
            yp = jnp.exp(cx) * cs[p] + v256_ref[V_SSM_D:V_SSM_D + 1, sl] * xh
            for h in range(2):
                i = MISC_SSM_DT + 2 * p + h
                dec = _seg_decay(cum[:, i:i + 1], cum_t[i:i + 1, :], incl)
                yp = yp + _mm(gmat[p] * dec, head(xdt, h))
            lane0 = MISC_SSM_DT + 2 * p
            dcol = rows(jnp.broadcast_to(jnp.exp(cum[C - 1:C, lane0:lane0 + 1]), (HD, SSM_STATE)),
                        jnp.broadcast_to(jnp.exp(cum[C - 1:C, lane0 + 1:lane0 + 2]), (HD, SSM_STATE)))
            s_ssm[g, p] = s[p] * dcol + _mm_tn(xdt * jnp.exp(last - cx), bm[:, sl])
            y_pairs.append(yp)
        yield
        y = jnp.concatenate(y_pairs, axis=1) * _silu(z)
        ss = _mm_sel_rhs(y * y, blk128)
        yield
        y_ref[g, :, 3 * GW:4 * GW] = y * lax.rsqrt(ss * (1.0 / (2 * HD)) + EPS) * vrow(V_SSM_NW)

    active = []
    for g in range(G):
        active += [rwkv(g), dn(g), gla(g), ssd(g)]
    while active:
        alive = []
        for gen in active:
            try:
                next(gen)
                alive.append(gen)
            except StopIteration:
                pass
        active = alive

    @pl.when(t == n_t - 1)
    def _fin():
        for g in range(G):
            shift1_ref[g] = xp_rw[g, 0:SUBLANES, :]
            dnc1_ref[g] = xp_dn[g, 0:SUBLANES, :]
            ssc1_ref[g] = xp_ss[g, 0:SUBLANES, :]
            for p in range(2):
                sw = s_wkv[g, p]
                sg = s_gla[g, p].T
                sd = s_dn[g, p]
                ss = s_ssm[g, p]
                for h in range(2):
                    hs = slice(HD * h, HD * (h + 1))
                    wkv1_ref[g, 2 * p + h] = sw[hs, hs]
                    gla1_ref[g, 2 * p + h] = sg[hs, hs]
                    dn1_ref[g, 2 * p + h] = sd[hs, hs]
                    ssm1_ref[g, 2 * p + h] = ss[hs, :]


def _mixer_call(u_parts, states, mparams, layer):
    u_rw, u_gla, u_dn, u_ssm, misc = u_parts
    shift0, wkv0, gla0, dnc0, dn0, ssc0, ssm0 = states
    bsz, tlen, _ = u_rw.shape
    C = math.gcd(tlen, PROMPT_CHUNK)
    n_t = tlen // C
    G = SEQS_PER_STEP_LONG if n_t > 1 else SEQS_PER_STEP_SHORT
    assert bsz % G == 0

    def tmap(b, t):
        return (b, t, 0)

    def bmap3(b, t):
        return (b, 0, 0)

    def bmap4(b, t):
        return (b, 0, 0, 0)

    def full(arr):
        nd = arr.ndim
        return pl.BlockSpec(arr.shape, lambda b, t: (0,) * nd)

    def lmap3(b, t):
        return (layer, b, 0, 0)

    def lmap4(b, t):
        return (layer, b, 0, 0, 0)

    sq = (G, NH, HD, HD)
    in_state_specs = [pl.BlockSpec((None, G, SUBLANES, RWKV_COLS), lmap3), pl.BlockSpec((None,) + sq, lmap4),
                      pl.BlockSpec((None,) + sq, lmap4), pl.BlockSpec((None, G, SUBLANES, CONV_COLS), lmap3),
                      pl.BlockSpec((None,) + sq, lmap4), pl.BlockSpec((None, G, SUBLANES, CONV_COLS), lmap3),
                      pl.BlockSpec((None, G, NH, HD, SSM_STATE), lmap4)]
    state_specs = [pl.BlockSpec((G, SUBLANES, RWKV_COLS), bmap3), pl.BlockSpec(sq, bmap4),
                   pl.BlockSpec(sq, bmap4), pl.BlockSpec((G, SUBLANES, CONV_COLS), bmap3),
                   pl.BlockSpec(sq, bmap4), pl.BlockSpec((G, SUBLANES, CONV_COLS), bmap3),
                   pl.BlockSpec((G, NH, HD, SSM_STATE), bmap4)]
    state_shapes = [jax.ShapeDtypeStruct((bsz, SUBLANES, RWKV_COLS), F32),
                    jax.ShapeDtypeStruct((bsz, NH, HD, HD), F32),
                    jax.ShapeDtypeStruct((bsz, NH, HD, HD), F32),
                    jax.ShapeDtypeStruct((bsz, SUBLANES, CONV_COLS), F32),
                    jax.ShapeDtypeStruct((bsz, NH, HD, HD), F32),
                    jax.ShapeDtypeStruct((bsz, SUBLANES, CONV_COLS), F32),
                    jax.ShapeDtypeStruct((bsz, NH, HD, SSM_STATE), F32)]
    outs = pl.pallas_call(
        functools.partial(_mixer_body, C, n_t, G),
        grid=(bsz // G, n_t),
        in_specs=[pl.BlockSpec((G, C, RWKV_COLS), tmap), pl.BlockSpec((G, C, 4 * GW), tmap),
                  pl.BlockSpec((G, C, 4 * GW), tmap), pl.BlockSpec((G, C, 4 * GW), tmap),
                  pl.BlockSpec((G, C, LANES), tmap)] + in_state_specs + [full(a) for a in mparams],
        out_specs=[pl.BlockSpec((G, C, D_MODEL), tmap)] + state_specs,
        out_shape=[jax.ShapeDtypeStruct((bsz, tlen, D_MODEL), F32)] + state_shapes,
        scratch_shapes=[pltpu.VMEM((G, C + SUBLANES, RWKV_COLS), F32),
                        pltpu.VMEM((G, C + SUBLANES, CONV_COLS), F32),
                        pltpu.VMEM((G, C + SUBLANES, CONV_COLS), F32),
                        pltpu.VMEM((G, 2, LANES, LANES), F32), pltpu.VMEM((G, 2, LANES, LANES), F32),
                        pltpu.VMEM((G, 2, LANES, LANES), F32), pltpu.VMEM((G, 2, LANES, SSM_STATE), F32)],
        compiler_params=pltpu.CompilerParams(vmem_limit_bytes=VMEM_LIMIT,
                                             dimension_semantics=("parallel", "arbitrary")),
        name="mixers",
    )(u_rw, u_gla, u_dn, u_ssm, misc, shift0, wkv0, gla0, dnc0, dn0, ssc0, ssm0, *mparams)
    return outs[0], tuple(outs[1:])


def _pad_rows(m, lo, total):
    return jnp.pad(m, ((lo, total - lo - m.shape[0]), (0, 0)))


def _lane_vec(vals, lo):
    return jnp.pad(vals, (lo, LANES - lo - vals.shape[0]))


def _pack_layer(P, l):
    v256 = jnp.stack([P['rwkv_w0'][l], P['rwkv_a0'][l], P['rwkv_k_k'][l], P['rwkv_k_a'][l],
                      P['rwkv_r_k'][l], P['rwkv_ln_w'][l], P['rwkv_ln_b'][l],
                      P['gla_gk_b'][l], P['gla_norm_w'][l], P['dn_norm_w'][l],
                      jnp.repeat(P['ssm_D'][l], HD), P['ssm_norm_w'][l]])
    v256 = jnp.pad(v256, ((0, 16 - v256.shape[0]), (0, 0)))
    v128 = jnp.stack([_lane_vec(P['dn_A_log'][l], MISC_DN_A), _lane_vec(P['dn_dt_bias'][l], MISC_DN_A),
                      _lane_vec(P['ssm_A_log'][l], MISC_SSM_DT), _lane_vec(P['ssm_dt_bias'][l], MISC_SSM_DT)])
    v128 = jnp.pad(v128, ((0, 4), (0, 0)))
    conv = jnp.concatenate([P['dn_conv_w'][l], P['ssm_conv_w'][l], P['ssm_conv_b'][l][None],
                            jnp.zeros((7, CONV_COLS), F32)], axis=0)
    lr = jnp.stack([_pad_rows(P['rwkv_w2'][l], 0, LANES), _pad_rows(P['rwkv_a2'][l], 32, LANES),
                    _pad_rows(P['rwkv_g2'][l], 64, LANES),
                    _pad_rows(P['gla_gk_w2'][l], MISC_GLA_GATE, LANES)]).astype(MXU_DT)
    return (P['rwkv_mu'][l][None], v256, v128, conv, lr)


def _pad_tail_rows(a):
    return jnp.pad(a, ((0, 0), (0, 0), (SUBLANES - a.shape[2], 0), (0, 0)))


def _trunk(x, mod_rows, states, mixer_params, dense, final_norm_w):
    n_layers = len(mixer_params)
    shift0, wkv0, gla0, dnc0, dn0, ssc0, ssm0 = states
    st_in = (_pad_tail_rows(shift0[:, :, None, :]), wkv0, gla0, _pad_tail_rows(dnc0), dn0,
             _pad_tail_rows(ssc0), ssm0)
    new = []
    for l in range(n_layers):
        mod = mod_rows[l][:, None, :]
        u_parts = _inproj_call(x, mod, dense['norm1'][l], dense['w_r'], l)
        y_mix, st = _mixer_call(u_parts, st_in, mixer_params[l], l)
        x = _outmlp_call(x, y_mix, mod, dense['norm2'][l], final_norm_w, dense['w_out'],
                         dense['w_up'], dense['w_down'], l, final=(l == n_layers - 1))
        shift1, wkv1, gla1, dnc1, dn1, ssc1, ssm1 = st
        new.append((shift1[:, SUBLANES - 1], wkv1, gla1, dnc1[:, SUBLANES - 3:], dn1,
                    ssc1[:, SUBLANES - 3:], ssm1))
    return x, tuple(jnp.stack([s[i] for s in new]) for i in range(7))


def kernel(x_prompt, x_sample, state_rwkv_shift, state_rwkv_wkv, state_gla, state_dn_conv,
           state_dn, state_ssm_conv, state_ssm, c_prompt, c_sample,
           ada_w, ada_b, norm1_w, norm2_w, w_in, w_out, w_up, w_down,
           rwkv_mu, rwkv_w0, rwkv_w2, rwkv_a0, rwkv_a2, rwkv_g2, rwkv_k_k, rwkv_k_a, rwkv_r_k,
           rwkv_ln_w, rwkv_ln_b, gla_gk_w2, gla_gk_b, gla_norm_w,
           dn_conv_w, dn_A_log, dn_dt_bias, dn_norm_w,
           ssm_conv_w, ssm_conv_b, ssm_dt_bias, ssm_A_log, ssm_D, ssm_norm_w, final_norm_w):
    P = dict(rwkv_mu=rwkv_mu, rwkv_w0=rwkv_w0, rwkv_w2=rwkv_w2, rwkv_a0=rwkv_a0, rwkv_a2=rwkv_a2,
             rwkv_g2=rwkv_g2, rwkv_k_k=rwkv_k_k, rwkv_k_a=rwkv_k_a, rwkv_r_k=rwkv_r_k,
             rwkv_ln_w=rwkv_ln_w, rwkv_ln_b=rwkv_ln_b, gla_gk_w2=gla_gk_w2, gla_gk_b=gla_gk_b,
             gla_norm_w=gla_norm_w, dn_conv_w=dn_conv_w, dn_A_log=dn_A_log, dn_dt_bias=dn_dt_bias,
             dn_norm_w=dn_norm_w, ssm_conv_w=ssm_conv_w, ssm_conv_b=ssm_conv_b,
             ssm_dt_bias=ssm_dt_bias, ssm_A_log=ssm_A_log, ssm_D=ssm_D, ssm_norm_w=ssm_norm_w)
    n_layers = w_in.shape[0]
    n_prompt = x_prompt.shape[0]
    mixer_params = [_pack_layer(P, l) for l in range(n_layers)]
    dense = dict(w_r=_wprep_call(w_in), w_out=w_out.astype(MXU_DT), w_up=w_up.astype(MXU_DT),
                 w_down=w_down.astype(MXU_DT), norm1=norm1_w, norm2=norm2_w)
    mod_all = _ada_call(jnp.concatenate([c_prompt, c_sample], axis=0), ada_w, ada_b)
    sample_states = (state_rwkv_shift, state_rwkv_wkv, state_gla, state_dn_conv,
                     state_dn, state_ssm_conv, state_ssm)
    prompt_states = tuple(jnp.zeros((n_layers, n_prompt) + s.shape[2:], F32) for s in sample_states)
    y_prompt, ps = _trunk(x_prompt, mod_all[:, :n_prompt], prompt_states, mixer_params, dense, final_norm_w)
    y_sample, ss = _trunk(x_sample, mod_all[:, n_prompt:], sample_states, mixer_params, dense, final_norm_w)
    return (y_prompt, y_sample) + ps + ss
```

```python
import functools
import math

import jax
import jax.numpy as jnp
from jax import lax
from jax.experimental import pallas as pl
from jax.experimental.pallas import tpu as pltpu

F32 = jnp.float32
MXU_DT = jnp.bfloat16

D_MODEL = 1024
NH = 4
HD = 64
GW = NH * HD
D_FF = 4 * D_MODEL
SSM_STATE = 128
EPS = 1e-6
RWKV_GN_EPS = 64e-5
GLA_GATE_NORM = 16.0
RWKV_COLS = 3 * GW + 32 + 32 + 64
CONV_COLS = 3 * GW
U_COLS = RWKV_COLS + 3 * 4 * GW + 128
MISC_GLA_GATE = 0
MISC_DN_A = 16
MISC_DN_B = 20
MISC_SSM_DT = 24

LANES = 128
SUBLANES = 8
VMEM_LIMIT = 56 * 1024 * 1024
PROMPT_CHUNK = 64
ROW_TILE = 256
SEQS_PER_STEP_LONG = 4
SEQS_PER_STEP_SHORT = 8


def _mm(a, b):
    return jnp.dot(a.astype(MXU_DT), b.astype(MXU_DT), preferred_element_type=F32)


def _mm_nt(a, b):
    return lax.dot_general(a.astype(MXU_DT), b.astype(MXU_DT), (((1,), (1,)), ((), ())),
                           preferred_element_type=F32)


def _mm_tn(a, b):
    return lax.dot_general(a.astype(MXU_DT), b.astype(MXU_DT), (((0,), (0,)), ((), ())),
                           preferred_element_type=F32)


def _mm3(a, b):
    a_hi, a_lo = _split(a, 2)
    b_hi, b_lo = _split(b, 2)
    return (jnp.dot(a_hi, b_hi, preferred_element_type=F32) + jnp.dot(a_hi, b_lo, preferred_element_type=F32)
            + jnp.dot(a_lo, b_hi, preferred_element_type=F32))


def _split(x, n):
    parts = []
    r = x
    for i in range(n):
        p = r.astype(MXU_DT)
        parts.append(p)
        if i + 1 < n:
            r = r - p.astype(F32)
    return parts


def _mm_sel_lhs(sel, x, n=2):
    acc = None
    for p in _split(x, n):
        d = jnp.dot(sel, p, preferred_element_type=F32)
        acc = d if acc is None else acc + d
    return acc


def _mm_sel_rhs(x, sel, n=2):
    acc = None
    for p in _split(x, n):
        d = jnp.dot(p, sel, preferred_element_type=F32)
        acc = d if acc is None else acc + d
    return acc


def _seg_decay(cum_col, cum_row, incl):
    d = cum_col - cum_row
    return jnp.where(incl, jnp.exp(jnp.where(incl, d, 0.0)), 0.0)


def _silu(x):
    return x * jax.nn.sigmoid(x)


def _ada_body(c_ref, w_ref, b_ref, o_ref):
    c = c_ref[...]
    o_ref[0] = _mm(_silu(c), w_ref[0]) + b_ref[0]


def _ada_call(c_all, ada_w, ada_b):
    n_layers = ada_w.shape[0]
    rows = c_all.shape[0]
    tn = 1536
    return pl.pallas_call(
        _ada_body,
        grid=(n_layers, 6 * D_MODEL // tn),
        in_specs=[pl.BlockSpec((rows, D_MODEL), lambda l, j: (0, 0)),
                  pl.BlockSpec((1, D_MODEL, tn), lambda l, j: (l, 0, j)),
                  pl.BlockSpec((1, 1, tn), lambda l, j: (l, 0, j))],
        out_specs=pl.BlockSpec((1, rows, tn), lambda l, j: (l, 0, j)),
        out_shape=jax.ShapeDtypeStruct((n_layers, rows, 6 * D_MODEL), F32),
        compiler_params=pltpu.CompilerParams(vmem_limit_bytes=VMEM_LIMIT),
        name="ada_mod",
    )(c_all, ada_w, ada_b.reshape(n_layers, 1, 6 * D_MODEL))


W_IN_COLS = RWKV_COLS + (4 * GW + 16) + (4 * GW + 8) + (4 * GW + 4)


def _wprep_body(w_ref, o_ref):
    w = w_ref[0]
    o_gla = RWKV_COLS
    o_dn = o_gla + 4 * GW + 16
    o_ssm = o_dn + 4 * GW + 8
    parts = [w[:, 0:o_gla + 4 * GW],
             w[:, o_dn:o_dn + 4 * GW],
             w[:, o_ssm:o_ssm + 4 * GW],
             w[:, o_gla + 4 * GW:o_dn],
             w[:, o_dn + 4 * GW:o_ssm],
             w[:, o_ssm + 4 * GW:W_IN_COLS],
             jnp.zeros((w.shape[0], LANES - 28), F32)]
    o_ref[0] = jnp.concatenate(parts, axis=1).astype(MXU_DT)


def _wprep_call(w_in):
    n_layers = w_in.shape[0]
    tr = 128
    return pl.pallas_call(
        _wprep_body,
        grid=(n_layers, D_MODEL // tr),
        in_specs=[pl.BlockSpec((1, tr, W_IN_COLS), lambda l, i: (l, i, 0))],
        out_specs=pl.BlockSpec((1, tr, U_COLS), lambda l, i: (l, i, 0)),
        out_shape=jax.ShapeDtypeStruct((n_layers, D_MODEL, U_COLS), MXU_DT),
        compiler_params=pltpu.CompilerParams(vmem_limit_bytes=VMEM_LIMIT,
                                             dimension_semantics=("parallel", "parallel")),
        name="w_in_relayout",
    )(w_in)


def _rms(x):
    return x * lax.rsqrt(jnp.mean(x * x, -1, keepdims=True) + EPS)


def _inproj_body(x_ref, sh_ref, sc_ref, nw_ref, w_ref, o_rw, o_gla, o_dn, o_ssm, o_misc):
    bb, tt, _ = x_ref.shape
    h = _rms(x_ref[...]) * nw_ref[...]
    h = h * (1.0 + sc_ref[...]) + sh_ref[...]
    u = jnp.dot(h.reshape(bb * tt, D_MODEL).astype(MXU_DT), w_ref[...], preferred_element_type=F32)
    off = 0
    for ref in (o_rw, o_gla, o_dn, o_ssm, o_misc):
        w = ref.shape[-1]
        ref[...] = u[:, off:off + w].reshape(bb, tt, w)
        off += w


def _row_blocks(bsz, tlen):
    tt = min(tlen, ROW_TILE)
    bb = ROW_TILE // tt
    assert tlen % tt == 0 and bsz % bb == 0
    return bb, tt


def _inproj_call(x, mod, norm_w, w_r, layer):
    bsz, tlen, _ = x.shape
    bb, tt = _row_blocks(bsz, tlen)
    widths = (RWKV_COLS, 4 * GW, 4 * GW, 4 * GW, LANES)

    def xmap(i, j):
        return (i, j, 0)

    return pl.pallas_call(
        _inproj_body,
        grid=(bsz // bb, tlen // tt),
        in_specs=[pl.BlockSpec((bb, tt, D_MODEL), xmap),
                  pl.BlockSpec((bb, 1, D_MODEL), lambda i, j: (i, 0, 0)),
                  pl.BlockSpec((bb, 1, D_MODEL), lambda i, j: (i, 0, 1)),
                  pl.BlockSpec((1, 1, D_MODEL), lambda i, j: (0, 0, 0)),
                  pl.BlockSpec((None, D_MODEL, U_COLS), lambda i, j: (layer, 0, 0),
                               pipeline_mode=pl.Buffered(1))],
        out_specs=[pl.BlockSpec((bb, tt, w), xmap) for w in widths],
        out_shape=[jax.ShapeDtypeStruct((bsz, tlen, w), F32) for w in widths],
        compiler_params=pltpu.CompilerParams(vmem_limit_bytes=VMEM_LIMIT,
                                             dimension_semantics=("parallel", "parallel")),
        name="in_proj",
    )(x, mod, mod, norm_w.reshape(1, 1, D_MODEL), w_r)


def _outmlp_body(final, x_ref, y_ref, gt1_ref, sh_ref, sc_ref, gt2_ref, nw_ref, fnw_ref,
                 wo_ref, wu_ref, wd_ref, o_ref):
    bb, tt, _ = x_ref.shape
    rows = bb * tt
    att = jnp.dot(y_ref[...].reshape(rows, D_MODEL).astype(MXU_DT), wo_ref[...],
                  preferred_element_type=F32)
    x1 = x_ref[...] + gt1_ref[...] * att.reshape(bb, tt, D_MODEL)
    h = _rms(x1) * nw_ref[...]
    h = h * (1.0 + sc_ref[...]) + sh_ref[...]
    a = jnp.dot(h.reshape(rows, D_MODEL).astype(MXU_DT), wu_ref[...], preferred_element_type=F32)
    a = jnp.square(jnp.maximum(a, 0.0))
    f = jnp.dot(a.astype(MXU_DT), wd_ref[...], preferred_element_type=F32)
    x2 = x1 + gt2_ref[...] * f.reshape(bb, tt, D_MODEL)
    if final:
        x2 = _rms(x2) * fnw_ref[...]
    o_ref[...] = x2


def _outmlp_call(x, y_mix, mod, norm_w, final_norm_w, w_out, w_up, w_down, layer, final):
    bsz, tlen, _ = x.shape
    bb, tt = _row_blocks(bsz, tlen)

    def xmap(i, j):
        return (i, j, 0)

    def modspec(k):
        return pl.BlockSpec((bb, 1, D_MODEL), lambda i, j: (i, 0, k))

    def wspec(shape):
        return pl.BlockSpec((None,) + shape, lambda i, j: (layer, 0, 0), pipeline_mode=pl.Buffered(1))

    vec = pl.BlockSpec((1, 1, D_MODEL), lambda i, j: (0, 0, 0))
    return pl.pallas_call(
        functools.partial(_outmlp_body, final),
        grid=(bsz // bb, tlen // tt),
        in_specs=[pl.BlockSpec((bb, tt, D_MODEL), xmap), pl.BlockSpec((bb, tt, D_MODEL), xmap),
                  modspec(2), modspec(3), modspec(4), modspec(5), vec, vec,
                  wspec((D_MODEL, D_MODEL)), wspec((D_MODEL, D_FF)), wspec((D_FF, D_MODEL))],
        out_specs=pl.BlockSpec((bb, tt, D_MODEL), xmap),
        out_shape=jax.ShapeDtypeStruct((bsz, tlen, D_MODEL), F32),
        compiler_params=pltpu.CompilerParams(vmem_limit_bytes=VMEM_LIMIT,
                                             dimension_semantics=("parallel", "parallel")),
        name="out_mlp",
    )(x, y_mix, mod, mod, mod, mod, norm_w.reshape(1, 1, D_MODEL),
      final_norm_w.reshape(1, 1, D_MODEL), w_out, w_up, w_down)


(V_RW_W0, V_RW_A0, V_RW_KK, V_RW_KA, V_RW_RK, V_RW_LNW, V_RW_LNB,
 V_GLA_B, V_GLA_NW, V_DN_NW, V_SSM_D, V_SSM_NW) = range(12)
P_DN_ALOG, P_DN_DTB, P_SSM_ALOG, P_SSM_DTB = range(4)
CV_DN_W, CV_SSM_W, CV_SSM_B = 0, 4, 8
LR_RW_W2, LR_RW_A2, LR_RW_G2, LR_GLA_GK = range(4)


def _blockdiag(a, b):
    za = jnp.zeros_like(a)
    return jnp.concatenate([jnp.concatenate([a, za], axis=1),
                            jnp.concatenate([za, b], axis=1)], axis=0)


def _mixer_body(C, n_t, G,
                u_rw_ref, u_gla_ref, u_dn_ref, u_ssm_ref, misc_ref,
                shift0_ref, wkv0_ref, gla0_ref, dnc0_ref, dn0_ref, ssc0_ref, ssm0_ref,
                mu_ref, v256_ref, v128_ref, conv_ref, lr_ref,
                y_ref, shift1_ref, wkv1_ref, gla1_ref, dnc1_ref, dn1_ref, ssc1_ref, ssm1_ref,
                xp_rw, xp_dn, xp_ss, s_wkv, s_gla, s_dn, s_ssm):
    t = pl.program_id(1)

    @pl.when(t == 0)
    def _init():
        for g in range(G):
            xp_rw[g, 0:SUBLANES, :] = shift0_ref[g]
            xp_dn[g, 0:SUBLANES, :] = dnc0_ref[g]
            xp_ss[g, 0:SUBLANES, :] = ssc0_ref[g]
            for p in range(2):
                s_wkv[g, p] = _blockdiag(wkv0_ref[g, 2 * p], wkv0_ref[g, 2 * p + 1])
                s_gla[g, p] = _blockdiag(gla0_ref[g, 2 * p], gla0_ref[g, 2 * p + 1]).T
                s_dn[g, p] = _blockdiag(dn0_ref[g, 2 * p], dn0_ref[g, 2 * p + 1])
                s_ssm[g, p] = jnp.concatenate([ssm0_ref[g, 2 * p], ssm0_ref[g, 2 * p + 1]], axis=0)

    ri = lax.broadcasted_iota(jnp.int32, (C, C), 0)
    ci = lax.broadcasted_iota(jnp.int32, (C, C), 1)
    incl = ci <= ri
    ri2 = lax.broadcasted_iota(jnp.int32, (C, 2 * C), 0)
    ci2 = lax.broadcasted_iota(jnp.int32, (C, 2 * C), 1)
    left2 = ci2 < C
    cj2 = ci2 & (C - 1)
    incl2 = cj2 <= ri2
    strict2 = cj2 < ri2
    incl4 = ((lax.broadcasted_iota(jnp.int32, (C, 4 * C), 1) & (C - 1))
             <= lax.broadcasted_iota(jnp.int32, (C, 4 * C), 0))
    tri = jnp.where(incl, 1.0, 0.0).astype(MXU_DT)
    lane = lax.broadcasted_iota(jnp.int32, (1, LANES), 1)
    hmask = (lane < HD, lane >= HD)
    r128 = lax.shift_right_logical(lax.broadcasted_iota(jnp.int32, (LANES, LANES), 0), 6)
    c128 = lax.shift_right_logical(lax.broadcasted_iota(jnp.int32, (LANES, LANES), 1), 6)
    bd = r128 == c128
    r256 = lax.broadcasted_iota(jnp.int32, (GW, GW), 0)
    c256 = lax.broadcasted_iota(jnp.int32, (GW, GW), 1)
    blk64 = jnp.where(lax.shift_right_logical(r256, 6) == lax.shift_right_logical(c256, 6),
                      1.0, 0.0).astype(MXU_DT)
    blk128 = jnp.where(lax.shift_right_logical(r256, 7) == lax.shift_right_logical(c256, 7),
                       1.0, 0.0).astype(MXU_DT)
    n_iter = 0
    while 2 * (1 << n_iter) < C:
        n_iter += 1

    def rows(*xs):
        return jnp.concatenate(xs, axis=0)

    def vrow(i):
        return v256_ref[i:i + 1, :]

    def hsum(x):
        return _mm_sel_rhs(x, blk64, n=1)

    def head(x, h):
        return jnp.where(hmask[h], x, 0.0)

    def hrows(x):
        return rows(head(x, 0), head(x, 1))

    def pair_nt(a, b):
        return _mm_nt(a, hrows(b))

    def pair_apply(q2, y):
        return _mm(q2, hrows(y))

    def bd2(q2):
        return rows(jnp.where(left2, q2, 0.0), jnp.where(left2, 0.0, q2))

    def seg2(tile, tile_t, lane0):
        col = jnp.where(left2, tile[:, lane0:lane0 + 1], tile[:, lane0 + 1:lane0 + 2])
        row = jnp.concatenate([tile_t[lane0:lane0 + 1, :], tile_t[lane0 + 1:lane0 + 2, :]], axis=1)
        return _seg_decay(col, row, incl2)

    def expand(tile, base):
        cols = [jnp.broadcast_to(tile[:, base + h:base + h + 1], (C, LANES)) for h in range(NH)]
        return [jnp.where(hmask[0], cols[0], cols[1]), jnp.where(hmask[0], cols[2], cols[3])]

    def psl(p):
        return slice(LANES * p, LANES * (p + 1))

    shared = {}

    def neumann(n_list, steps):
        rs = list(n_list)
        ps = list(n_list)
        bps = [bd2(p) for p in ps]
        for _ in range(steps):
            ps = [_mm(p, bp) for p, bp in zip(ps, bps)]
            yield None
            bps = [bd2(p) for p in ps]
            rs = [r + p + _mm(r, bp) for r, p, bp in zip(rs, ps, bps)]
        yield rs

    def rwkv(g):
        u_rw = u_rw_ref[g]
        xp_rw[g, SUBLANES:SUBLANES + C, :] = u_rw
        u_prev = xp_rw[g, SUBLANES - 1:SUBLANES - 1 + C, :]
        xp_rw[g, 0:SUBLANES, :] = xp_rw[g, C:C + SUBLANES, :]
        xs = u_rw + (u_prev - u_rw) * mu_ref[...]
        r = xs[:, 0:GW]
        k = xs[:, GW:2 * GW]
        v = xs[:, 2 * GW:3 * GW]
        x7 = xs[:, 3 * GW:RWKV_COLS]
        w_pre = _mm(jnp.tanh(x7), lr_ref[LR_RW_W2])
        a_pre = _mm(x7, lr_ref[LR_RW_A2])
        gate = _mm(jax.nn.sigmoid(x7), lr_ref[LR_RW_G2])
        kk = k * vrow(V_RW_KK)
        kk_ss = hsum(kk * kk)
        yield
        log_w = -jax.nn.softplus(-(vrow(V_RW_W0) + w_pre)) - 0.5
        lw = -jnp.exp(log_w)
        a = jax.nn.sigmoid(vrow(V_RW_A0) + a_pre)
        kk = kk * lax.rsqrt(kk_ss + EPS)
        k2 = k * (1.0 + (a - 1.0) * vrow(V_RW_KA))
        bcum = _mm_sel_lhs(tri, lw)
        bonus = hsum(r * k2 * vrow(V_RW_RK))
        yield
        e_nb = jnp.exp(-bcum)
        blast = bcum[C - 1:C, :]
        e_rem = jnp.exp(blast - bcum)
        nka = -kk * a
        r_t = r * jnp.exp(bcum)
        a_t = kk * jnp.exp(bcum - lw)
        b_t = nka * e_nb
        k_t = k2 * e_nb
        k_rem = k2 * e_rem
        b_rem = nka * e_rem
        s = [s_wkv[g, p] for p in range(2)]
        ars = [_mm_nt(rows(a_t[:, psl(p)], r_t[:, psl(p)]), s[p]) for p in range(2)]
        ar = [_mm_nt(rows(a_t[:, psl(p)], r_t[:, psl(p)]), rows(hrows(b_t[:, psl(p)]), hrows(k_t[:, psl(p)])))
              for p in range(2)]
        yield
        n_ab = [jnp.where(strict2, ar[p][0:C, 0:2 * C], 0.0) for p in range(2)]
        n_ak = [jnp.where(strict2, ar[p][0:C, 2 * C:4 * C], 0.0) for p in range(2)]
        a_r = [jnp.where(incl4, ar[p][C:2 * C], 0.0) for p in range(2)]
        rhs = [ars[p][0:C] + pair_apply(n_ak[p], v[:, psl(p)]) for p in range(2)]
        tinv = None
        for step in neumann(n_ab, n_iter):
            if step is None:
                yield
            else:
                tinv = step
        yield
        uh = [rhs[p] + pair_apply(tinv[p], rhs[p]) for p in range(2)]
        yield
        y_pairs = []
        for p in range(2):
            sl = psl(p)
            vp = v[:, sl]
            u_p = uh[p]
            yp = ars[p][C:2 * C] + _mm(a_r[p], rows(hrows(u_p), hrows(vp)))
            s_new = s[p] * jnp.exp(blast[:, sl]) + _mm_tn(rows(vp, u_p), rows(k_rem[:, sl], b_rem[:, sl]))
            s_wkv[g, p] = jnp.where(bd, s_new, 0.0)
            y_pairs.append(yp)
        yield
        y = jnp.concatenate(y_pairs, axis=1)
        mean = hsum(y) * (1.0 / HD)
        yield
        yc = y - mean
        var = hsum(yc * yc) * (1.0 / HD)
        yield
        y = yc * lax.rsqrt(var + RWKV_GN_EPS) * vrow(V_RW_LNW) + vrow(V_RW_LNB)
        y_ref[g, :, 0:GW] = (y + bonus * v) * gate

    def gla(g):
        u_gla = u_gla_ref[g]
        q = u_gla[:, 0:GW] * (HD ** -0.5)
        k = u_gla[:, GW:2 * GW]
        v = u_gla[:, 2 * GW:3 * GW]
        gz = u_gla[:, 3 * GW:4 * GW]
        gate = _mm(misc_ref[g], lr_ref[LR_GLA_GK])
        yield
        la = jax.nn.log_sigmoid(gate + vrow(V_GLA_B)) * (1.0 / GLA_GATE_NORM)
        bcum = _mm_sel_lhs(tri, la)
        yield
        blast = bcum[C - 1:C, :]
        q_in = q * jnp.exp(bcum)
        k_in = k * jnp.exp(-bcum)
        k_out = k * jnp.exp(blast - bcum)
        st = [s_gla[g, p] for p in range(2)]
        qs = [_mm_nt(q_in[:, psl(p)], st[p]) for p in range(2)]
        a_h = [pair_nt(q_in[:, psl(p)], k_in[:, psl(p)]) for p in range(2)]
        for p in range(2):
            sl = psl(p)
            st_new = st[p] * jnp.exp(blast[:, sl]) + _mm_tn(v[:, sl], k_out[:, sl])
            s_gla[g, p] = jnp.where(bd, st_new, 0.0)
        yield
        o_pairs = []
        for p in range(2):
            o_pairs.append(qs[p] + pair_apply(jnp.where(incl2, a_h[p], 0.0), v[:, psl(p)]))
        yield
        o = jnp.concatenate(o_pairs, axis=1)
        ss = hsum(o * o)
        yield
        o = o * lax.rsqrt(ss * (1.0 / HD) + EPS) * vrow(V_GLA_NW)
        y_ref[g, :, GW:2 * GW] = o * _silu(gz)

    def dn(g):
        misc = misc_ref[g]
        g_dn = -jnp.exp(v128_ref[P_DN_ALOG:P_DN_ALOG + 1, :]) * jax.nn.softplus(
            misc + v128_ref[P_DN_DTB:P_DN_DTB + 1, :])
        dt_ss = jax.nn.softplus(misc + v128_ref[P_SSM_DTB:P_SSM_DTB + 1, :])
        la_ss = dt_ss * -jnp.exp(v128_ref[P_SSM_ALOG:P_SSM_ALOG + 1, :])
        beta = jax.nn.sigmoid(misc)
        cum = _mm_sel_lhs(tri, jnp.where(lane < MISC_DN_B, g_dn, la_ss))
        u_dn = u_dn_ref[g]
        xp_dn[g, SUBLANES:SUBLANES + C, :] = u_dn[:, 0:CONV_COLS]
        qkv = None
        for i in range(4):
            term = (xp_dn[g, SUBLANES - 3 + i:SUBLANES - 3 + i + C, :]
                    * conv_ref[CV_DN_W + i:CV_DN_W + i + 1, :])
            qkv = term if qkv is None else qkv + term
        xp_dn[g, 0:SUBLANES, :] = xp_dn[g, C:C + SUBLANES, :]
        qkv = _silu(qkv)
        q = qkv[:, 0:GW]
        k = qkv[:, GW:2 * GW]
        v = qkv[:, 2 * GW:3 * GW]
        z = u_dn[:, CONV_COLS:CONV_COLS + GW]
        q_ss = hsum(q * q)
        k_ss = hsum(k * k)
        yield
        cum_t = cum.T
        shared[g] = (cum, cum_t, dt_ss)
        q = q * lax.rsqrt(q_ss + EPS) * (HD ** -0.5)
        k = k * lax.rsqrt(k_ss + EPS)
        beta_x = expand(beta, MISC_DN_B)
        cum_x = expand(cum, MISC_DN_A)
        s = [s_dn[g, p] for p in range(2)]
        ecx, kb, vb, kbe, qs, kq = [], [], [], [], [], []
        for p in range(2):
            sl = psl(p)
            ecx.append(jnp.exp(cum_x[p]))
            kb.append(k[:, sl] * beta_x[p])
            vb.append(v[:, sl] * beta_x[p])
            kbe.append(kb[p] * ecx[p])
            qs.append(_mm(q[:, sl] * ecx[p], s[p]))
            kq.append(pair_nt(rows(kb[p], q[:, sl]), k[:, sl]))
        yield
        n_m, a_l = [], []
        for p in range(2):
            dec = seg2(cum, cum_t, MISC_DN_A + 2 * p)
            n_m.append(jnp.where(strict2, -(kq[p][0:C] * dec), 0.0))
            a_l.append(kq[p][C:2 * C] * dec)
        tinv = None
        for step in neumann(n_m, n_iter - 1):
            if step is None:
                yield
            else:
                tinv = step
        yield
        err = [n_m[p] - tinv[p] + _mm3(n_m[p], bd2(tinv[p])) for p in range(2)]
        yield
        tinv = [tinv[p] + err[p] + _mm(tinv[p], bd2(err[p])) for p in range(2)]
        yield
        vk = [jnp.concatenate([vb[p], kbe[p]], axis=1) for p in range(2)]
        uw = [vk[p] + _mm(tinv[p], jnp.concatenate([hrows(vb[p]), hrows(kbe[p])], axis=1)) for p in range(2)]
        yield
        u_p = [uw[p][:, 0:LANES] for p in range(2)]
        w_p = [uw[p][:, LANES:2 * LANES] for p in range(2)]
        ws = [_mm(w_p[p], s[p]) for p in range(2)]
        yield
        o_pairs = []
        for p in range(2):
            sl = psl(p)
            cx = cum_x[p]
            last = cx[C - 1:C, :]
            v_new = u_p[p] - ws[p]
            op = qs[p] + pair_apply(a_l[p], v_new)
            s_new = s[p] * jnp.exp(last) + _mm_tn(k[:, sl] * jnp.exp(last - cx), v_new)
            s_dn[g, p] = jnp.where(bd, s_new, 0.0)
            o_pairs.append(op)
        yield
        o = jnp.concatenate(o_pairs, axis=1)
        ss = hsum(o * o)
        yield
        o = o * lax.rsqrt(ss * (1.0 / HD) + EPS) * vrow(V_DN_NW)
        y_ref[g, :, 2 * GW:3 * GW] = o * _silu(z)

    def ssd(g):
        u_ssm = u_ssm_ref[g]
        z = u_ssm[:, 0:GW]
        xp_ss[g, SUBLANES:SUBLANES + C, :] = u_ssm[:, GW:GW + CONV_COLS]
        xbc = conv_ref[CV_SSM_B:CV_SSM_B + 1, :]
        for i in range(4):
            xbc = xbc + (xp_ss[g, SUBLANES - 3 + i:SUBLANES - 3 + i + C, :]
                         * conv_ref[CV_SSM_W + i:CV_SSM_W + i + 1, :])
        xp_ss[g, 0:SUBLANES, :] = xp_ss[g, C:C + SUBLANES, :]
        xbc = _silu(xbc)
        xs_ = xbc[:, 0:GW]
        bm = xbc[:, GW:2 * GW]
        cm = xbc[:, 2 * GW:3 * GW]
        s = [s_ssm[g, p] for p in range(2)]
        gmat = [_mm_nt(cm[:, psl(p)], rows(bm[:, psl(p)], bm[:, psl(p)])) for p in range(2)]
        cs = [_mm_nt(cm[:, psl(p)], s[p]) for p in range(2)]
        yield
        yield
        cum, cum_t, dt_ss = shared[g]
        dt_x = expand(dt_ss, MISC_SSM_DT)
        cum_x = expand(cum, MISC_SSM_DT)
        y_pairs = []
        for p in range(2):
            sl = psl(p)
            xh = xs_[:, sl]
            xdt = xh * dt_x[p]
            cx = cum_x[p]
            last = cx[C - 1:C, :]
            yp = jnp.exp(cx) * cs[p] + v256_ref[V_SSM_D:V_SSM_D + 1, sl] * xh
            yp = yp + pair_apply(gmat[p] * seg2(cum, cum_t, MISC_SSM_DT + 2 * p), xdt)
            lane0 = MISC_SSM_DT + 2 * p
            dcol = rows(jnp.broadcast_to(jnp.exp(cum[C - 1:C, lane0:lane0 + 1]), (HD, SSM_STATE)),
                        jnp.broadcast_to(jnp.exp(cum[C - 1:C, lane0 + 1:lane0 + 2]), (HD, SSM_STATE)))
            s_ssm[g, p] = s[p] * dcol + _mm_tn(xdt * jnp.exp(last - cx), bm[:, sl])
            y_pairs.append(yp)
        yield
        y = jnp.concatenate(y_pairs, axis=1) * _silu(z)
        ss = _mm_sel_rhs(y * y, blk128)
        yield
        y_ref[g, :, 3 * GW:4 * GW] = y * lax.rsqrt(ss * (1.0 / (2 * HD)) + EPS) * vrow(V_SSM_NW)

    active = []
    for g in range(G):
        active += [rwkv(g), dn(g), gla(g), ssd(g)]
    while active:
        alive = []
        for gen in active:
            try:
                next(gen)
                alive.append(gen)
            except StopIteration:
                pass
        active = alive

    @pl.when(t == n_t - 1)
    def _fin():
        for g in range(G):
            shift1_ref[g] = xp_rw[g, 0:SUBLANES, :]
            dnc1_ref[g] = xp_dn[g, 0:SUBLANES, :]
            ssc1_ref[g] = xp_ss[g, 0:SUBLANES, :]
            for p in range(2):
                sw = s_wkv[g, p]
                sg = s_gla[g, p].T
                sd = s_dn[g, p]
                ss = s_ssm[g, p]
                for h in range(2):
                    hs = slice(HD * h, HD * (h + 1))
                    wkv1_ref[g, 2 * p + h] = sw[hs, hs]
                    gla1_ref[g, 2 * p + h] = sg[hs, hs]
                    dn1_ref[g, 2 * p + h] = sd[hs, hs]
                    ssm1_ref[g, 2 * p + h] = ss[hs, :]


def _mixer_call(u_parts, states, mparams, layer):
    u_rw, u_gla, u_dn, u_ssm, misc = u_parts
    shift0, wkv0, gla0, dnc0, dn0, ssc0, ssm0 = states
    bsz, tlen, _ = u_rw.shape
    C = math.gcd(tlen, PROMPT_CHUNK)
    n_t = tlen // C
    G = SEQS_PER_STEP_LONG if n_t > 1 else SEQS_PER_STEP_SHORT
    assert bsz % G == 0 and C & (C - 1) == 0

    def tmap(b, t):
        return (b, t, 0)

    def bmap3(b, t):
        return (b, 0, 0)

    def bmap4(b, t):
        return (b, 0, 0, 0)

    def full(arr):
        nd = arr.ndim
        return pl.BlockSpec(arr.shape, lambda b, t: (0,) * nd)

    def lmap3(b, t):
        return (layer, b, 0, 0)

    def lmap4(b, t):
        return (layer, b, 0, 0, 0)

    sq = (G, NH, HD, HD)
    in_state_specs = [pl.BlockSpec((None, G, SUBLANES, RWKV_COLS), lmap3), pl.BlockSpec((None,) + sq, lmap4),
                      pl.BlockSpec((None,) + sq, lmap4), pl.BlockSpec((None, G, SUBLANES, CONV_COLS), lmap3),
                      pl.BlockSpec((None,) + sq, lmap4), pl.BlockSpec((None, G, SUBLANES, CONV_COLS), lmap3),
                      pl.BlockSpec((None, G, NH, HD, SSM_STATE), lmap4)]
    state_specs = [pl.BlockSpec((G, SUBLANES, RWKV_COLS), bmap3), pl.BlockSpec(sq, bmap4),
                   pl.BlockSpec(sq, bmap4), pl.BlockSpec((G, SUBLANES, CONV_COLS), bmap3),
                   pl.BlockSpec(sq, bmap4), pl.BlockSpec((G, SUBLANES, CONV_COLS), bmap3),
                   pl.BlockSpec((G, NH, HD, SSM_STATE), bmap4)]
    state_shapes = [jax.ShapeDtypeStruct((bsz, SUBLANES, RWKV_COLS), F32),
                    jax.ShapeDtypeStruct((bsz, NH, HD, HD), F32),
                    jax.ShapeDtypeStruct((bsz, NH, HD, HD), F32),
                    jax.ShapeDtypeStruct((bsz, SUBLANES, CONV_COLS), F32),
                    jax.ShapeDtypeStruct((bsz, NH, HD, HD), F32),
                    jax.ShapeDtypeStruct((bsz, SUBLANES, CONV_COLS), F32),
                    jax.ShapeDtypeStruct((bsz, NH, HD, SSM_STATE), F32)]
    outs = pl.pallas_call(
        functools.partial(_mixer_body, C, n_t, G),
        grid=(bsz // G, n_t),
        in_specs=[pl.BlockSpec((G, C, RWKV_COLS), tmap), pl.BlockSpec((G, C, 4 * GW), tmap),
                  pl.BlockSpec((G, C, 4 * GW), tmap), pl.BlockSpec((G, C, 4 * GW), tmap),
                  pl.BlockSpec((G, C, LANES), tmap)] + in_state_specs + [full(a) for a in mparams],
        out_specs=[pl.BlockSpec((G, C, D_MODEL), tmap)] + state_specs,
        out_shape=[jax.ShapeDtypeStruct((bsz, tlen, D_MODEL), F32)] + state_shapes,
        scratch_shapes=[pltpu.VMEM((G, C + SUBLANES, RWKV_COLS), F32),
                        pltpu.VMEM((G, C + SUBLANES, CONV_COLS), F32),
                        pltpu.VMEM((G, C + SUBLANES, CONV_COLS), F32),
                        pltpu.VMEM((G, 2, LANES, LANES), F32), pltpu.VMEM((G, 2, LANES, LANES), F32),
                        pltpu.VMEM((G, 2, LANES, LANES), F32), pltpu.VMEM((G, 2, LANES, SSM_STATE), F32)],
        compiler_params=pltpu.CompilerParams(vmem_limit_bytes=VMEM_LIMIT,
                                             dimension_semantics=("parallel", "arbitrary")),
        name="mixers",
    )(u_rw, u_gla, u_dn, u_ssm, misc, shift0, wkv0, gla0, dnc0, dn0, ssc0, ssm0, *mparams)
    return outs[0], tuple(outs[1:])


def _pad_rows(m, lo, total):
    return jnp.pad(m, ((lo, total - lo - m.shape[0]), (0, 0)))


def _lane_vec(vals, lo):
    return jnp.pad(vals, (lo, LANES - lo - vals.shape[0]))


def _pack_layer(P, l):
    v256 = jnp.stack([P['rwkv_w0'][l], P['rwkv_a0'][l], P['rwkv_k_k'][l], P['rwkv_k_a'][l],
                      P['rwkv_r_k'][l], P['rwkv_ln_w'][l], P['rwkv_ln_b'][l],
                      P['gla_gk_b'][l], P['gla_norm_w'][l], P['dn_norm_w'][l],
                      jnp.repeat(P['ssm_D'][l], HD), P['ssm_norm_w'][l]])
    v256 = jnp.pad(v256, ((0, 16 - v256.shape[0]), (0, 0)))
    v128 = jnp.stack([_lane_vec(P['dn_A_log'][l], MISC_DN_A), _lane_vec(P['dn_dt_bias'][l], MISC_DN_A),
                      _lane_vec(P['ssm_A_log'][l], MISC_SSM_DT), _lane_vec(P['ssm_dt_bias'][l], MISC_SSM_DT)])
    v128 = jnp.pad(v128, ((0, 4), (0, 0)))
    conv = jnp.concatenate([P['dn_conv_w'][l], P['ssm_conv_w'][l], P['ssm_conv_b'][l][None],
                            jnp.zeros((7, CONV_COLS), F32)], axis=0)
    lr = jnp.stack([_pad_rows(P['rwkv_w2'][l], 0, LANES), _pad_rows(P['rwkv_a2'][l], 32, LANES),
                    _pad_rows(P['rwkv_g2'][l], 64, LANES),
                    _pad_rows(P['gla_gk_w2'][l], MISC_GLA_GATE, LANES)]).astype(MXU_DT)
    return (P['rwkv_mu'][l][None], v256, v128, conv, lr)


def _pad_tail_rows(a):
    return jnp.pad(a, ((0, 0), (0, 0), (SUBLANES - a.shape[2], 0), (0, 0)))


def _trunk(x, mod_rows, states, mixer_params, dense, final_norm_w):
    n_layers = len(mixer_params)
    shift0, wkv0, gla0, dnc0, dn0, ssc0, ssm0 = states
    st_in = (_pad_tail_rows(shift0[:, :, None, :]), wkv0, gla0, _pad_tail_rows(dnc0), dn0,
             _pad_tail_rows(ssc0), ssm0)
    new = []
    for l in range(n_layers):
        mod = mod_rows[l][:, None, :]
        u_parts = _inproj_call(x, mod, dense['norm1'][l], dense['w_r'], l)
        y_mix, st = _mixer_call(u_parts, st_in, mixer_params[l], l)
        x = _outmlp_call(x, y_mix, mod, dense['norm2'][l], final_norm_w, dense['w_out'],
                         dense['w_up'], dense['w_down'], l, final=(l == n_layers - 1))
        shift1, wkv1, gla1, dnc1, dn1, ssc1, ssm1 = st
        new.append((shift1[:, SUBLANES - 1], wkv1, gla1, dnc1[:, SUBLANES - 3:], dn1,
                    ssc1[:, SUBLANES - 3:], ssm1))
    return x, tuple(jnp.stack([s[i] for s in new]) for i in range(7))


def kernel(x_prompt, x_sample, state_rwkv_shift, state_rwkv_wkv, state_gla, state_dn_conv,
           state_dn, state_ssm_conv, state_ssm, c_prompt, c_sample,
           ada_w, ada_b, norm1_w, norm2_w, w_in, w_out, w_up, w_down,
           rwkv_mu, rwkv_w0, rwkv_w2, rwkv_a0, rwkv_a2, rwkv_g2, rwkv_k_k, rwkv_k_a, rwkv_r_k,
           rwkv_ln_w, rwkv_ln_b, gla_gk_w2, gla_gk_b, gla_norm_w,
           dn_conv_w, dn_A_log, dn_dt_bias, dn_norm_w,
           ssm_conv_w, ssm_conv_b, ssm_dt_bias, ssm_A_log, ssm_D, ssm_norm_w, final_norm_w):
    P = dict(rwkv_mu=rwkv_mu, rwkv_w0=rwkv_w0, rwkv_w2=rwkv_w2, rwkv_a0=rwkv_a0, rwkv_a2=rwkv_a2,
             rwkv_g2=rwkv_g2, rwkv_k_k=rwkv_k_k, rwkv_k_a=rwkv_k_a, rwkv_r_k=rwkv_r_k,
             rwkv_ln_w=rwkv_ln_w, rwkv_ln_b=rwkv_ln_b, gla_gk_w2=gla_gk_w2, gla_gk_b=gla_gk_b,
             gla_norm_w=gla_norm_w, dn_conv_w=dn_conv_w, dn_A_log=dn_A_log, dn_dt_bias=dn_dt_bias,
             dn_norm_w=dn_norm_w, ssm_conv_w=ssm_conv_w, ssm_conv_b=ssm_conv_b,
             ssm_dt_bias=ssm_dt_bias, ssm_A_log=ssm_A_log, ssm_D=ssm_D, ssm_norm_w=ssm_norm_w)
    n_layers = w_in.shape[0]
    n_prompt = x_prompt.shape[0]
    mixer_params = [_pack_layer(P, l) for l in range(n_layers)]
    dense = dict(w_r=_wprep_call(w_in), w_out=w_out.astype(MXU_DT), w_up=w_up.astype(MXU_DT),
                 w_down=w_down.astype(MXU_DT), norm1=norm1_w, norm2=norm2_w)
    mod_all = _ada_call(jnp.concatenate([c_prompt, c_sample], axis=0), ada_w, ada_b)
    sample_states = (state_rwkv_shift, state_rwkv_wkv, state_gla, state_dn_conv,
                     state_dn, state_ssm_conv, state_ssm)
    prompt_states = tuple(jnp.zeros((n_layers, n_prompt) + s.shape[2:], F32) for s in sample_states)
    y_prompt, ps = _trunk(x_prompt, mod_all[:, :n_prompt], prompt_states, mixer_params, dense, final_norm_w)
    y_sample, ss = _trunk(x_sample, mod_all[:, n_prompt:], sample_states, mixer_params, dense, final_norm_w)
    return (y_prompt, y_sample) + ps + ss
```

```python
import functools
import math

import jax
import jax.numpy as jnp
from jax import lax
from jax.experimental import pallas as pl
from jax.experimental.pallas import tpu as pltpu

F32 = jnp.float32
MXU_DT = jnp.bfloat16

D_MODEL = 1024
NH = 4
HD = 64
GW = NH * HD
D_FF = 4 * D_MODEL
SSM_STATE = 128
EPS = 1e-6
RWKV_GN_EPS = 64e-5
GLA_GATE_NORM = 16.0
RWKV_COLS = 3 * GW + 32 + 32 + 64
CONV_COLS = 3 * GW
U_COLS = RWKV_COLS + 3 * 4 * GW + 128
MISC_GLA_GATE = 0
MISC_DN_A = 16
MISC_DN_B = 20
MISC_SSM_DT = 24

LANES = 128
SUBLANES = 8
VMEM_LIMIT = 56 * 1024 * 1024
PROMPT_CHUNK = 64
ROW_TILE = 512
SEQS_PER_STEP_LONG = 4
SEQS_PER_STEP_SHORT = 8


def _mm(a, b):
    return jnp.dot(a.astype(MXU_DT), b.astype(MXU_DT), preferred_element_type=F32)


def _mm_nt(a, b):
    return lax.dot_general(a.astype(MXU_DT), b.astype(MXU_DT), (((1,), (1,)), ((), ())),
                           preferred_element_type=F32)


def _mm_tn(a, b):
    return lax.dot_general(a.astype(MXU_DT), b.astype(MXU_DT), (((0,), (0,)), ((), ())),
                           preferred_element_type=F32)


def _mm3(a, b):
    a_hi, a_lo = _split(a, 2)
    b_hi, b_lo = _split(b, 2)
    return (jnp.dot(a_hi, b_hi, preferred_element_type=F32) + jnp.dot(a_hi, b_lo, preferred_element_type=F32)
            + jnp.dot(a_lo, b_hi, preferred_element_type=F32))


def _split(x, n):
    parts = []
    r = x
    for i in range(n):
        p = r.astype(MXU_DT)
        parts.append(p)
        if i + 1 < n:
            r = r - p.astype(F32)
    return parts


def _mm_sel_lhs(sel, x, n=2):
    acc = None
    for p in _split(x, n):
        d = jnp.dot(sel, p, preferred_element_type=F32)
        acc = d if acc is None else acc + d
    return acc


def _mm_sel_rhs(x, sel, n=2):
    acc = None
    for p in _split(x, n):
        d = jnp.dot(p, sel, preferred_element_type=F32)
        acc = d if acc is None else acc + d
    return acc


def _seg_decay(cum_col, cum_row, incl):
    d = cum_col - cum_row
    return jnp.where(incl, jnp.exp(jnp.where(incl, d, 0.0)), 0.0)


def _silu(x):
    return x * jax.nn.sigmoid(x)


def _ada_body(c_ref, w_ref, b_ref, o_ref):
    c = c_ref[...]
    o_ref[0] = _mm(_silu(c), w_ref[0]) + b_ref[0]


def _ada_call(c_all, ada_w, ada_b):
    n_layers = ada_w.shape[0]
    rows = c_all.shape[0]
    tn = 1536
    return pl.pallas_call(
        _ada_body,
        grid=(n_layers, 6 * D_MODEL // tn),
        in_specs=[pl.BlockSpec((rows, D_MODEL), lambda l, j: (0, 0)),
                  pl.BlockSpec((1, D_MODEL, tn), lambda l, j: (l, 0, j)),
                  pl.BlockSpec((1, 1, tn), lambda l, j: (l, 0, j))],
        out_specs=pl.BlockSpec((1, rows, tn), lambda l, j: (l, 0, j)),
        out_shape=jax.ShapeDtypeStruct((n_layers, rows, 6 * D_MODEL), F32),
        compiler_params=pltpu.CompilerParams(vmem_limit_bytes=VMEM_LIMIT),
        name="ada_mod",
    )(c_all, ada_w, ada_b.reshape(n_layers, 1, 6 * D_MODEL))


W_IN_COLS = RWKV_COLS + (4 * GW + 16) + (4 * GW + 8) + (4 * GW + 4)


def _wprep_body(w_ref, o_ref):
    w = w_ref[0]
    o_gla = RWKV_COLS
    o_dn = o_gla + 4 * GW + 16
    o_ssm = o_dn + 4 * GW + 8
    parts = [w[:, 0:o_gla + 4 * GW],
             w[:, o_dn:o_dn + 4 * GW],
             w[:, o_ssm:o_ssm + 4 * GW],
             w[:, o_gla + 4 * GW:o_dn],
             w[:, o_dn + 4 * GW:o_ssm],
             w[:, o_ssm + 4 * GW:W_IN_COLS],
             jnp.zeros((w.shape[0], LANES - 28), F32)]
    o_ref[0] = jnp.concatenate(parts, axis=1).astype(MXU_DT)


def _wprep_call(w_in):
    n_layers = w_in.shape[0]
    tr = 128
    return pl.pallas_call(
        _wprep_body,
        grid=(n_layers, D_MODEL // tr),
        in_specs=[pl.BlockSpec((1, tr, W_IN_COLS), lambda l, i: (l, i, 0))],
        out_specs=pl.BlockSpec((1, tr, U_COLS), lambda l, i: (l, i, 0)),
        out_shape=jax.ShapeDtypeStruct((n_layers, D_MODEL, U_COLS), MXU_DT),
        compiler_params=pltpu.CompilerParams(vmem_limit_bytes=VMEM_LIMIT,
                                             dimension_semantics=("parallel", "parallel")),
        name="w_in_relayout",
    )(w_in)


def _rms(x):
    return x * lax.rsqrt(jnp.mean(x * x, -1, keepdims=True) + EPS)


def _inproj_body(x_ref, sh_ref, sc_ref, nw_ref, w_ref, o_rw, o_gla, o_dn, o_ssm, o_misc):
    bb, tt, _ = x_ref.shape
    h = _rms(x_ref[...]) * nw_ref[...]
    h = h * (1.0 + sc_ref[...]) + sh_ref[...]
    u = jnp.dot(h.reshape(bb * tt, D_MODEL).astype(MXU_DT), w_ref[...], preferred_element_type=F32)
    off = 0
    for ref in (o_rw, o_gla, o_dn, o_ssm, o_misc):
        w = ref.shape[-1]
        ref[...] = u[:, off:off + w].reshape(bb, tt, w)
        off += w


def _row_blocks(bsz, tlen):
    tt = min(tlen, ROW_TILE)
    bb = ROW_TILE // tt
    assert tlen % tt == 0 and bsz % bb == 0
    return bb, tt


def _inproj_call(x, mod, norm_w, w_r, layer):
    bsz, tlen, _ = x.shape
    bb, tt = _row_blocks(bsz, tlen)
    widths = (RWKV_COLS, 4 * GW, 4 * GW, 4 * GW, LANES)

    def xmap(i, j):
        return (i, j, 0)

    return pl.pallas_call(
        _inproj_body,
        grid=(bsz // bb, tlen // tt),
        in_specs=[pl.BlockSpec((bb, tt, D_MODEL), xmap),
                  pl.BlockSpec((bb, 1, D_MODEL), lambda i, j: (i, 0, 0)),
                  pl.BlockSpec((bb, 1, D_MODEL), lambda i, j: (i, 0, 1)),
                  pl.BlockSpec((1, 1, D_MODEL), lambda i, j: (0, 0, 0)),
                  pl.BlockSpec((None, D_MODEL, U_COLS), lambda i, j: (layer, 0, 0),
                               pipeline_mode=pl.Buffered(1))],
        out_specs=[pl.BlockSpec((bb, tt, w), xmap) for w in widths],
        out_shape=[jax.ShapeDtypeStruct((bsz, tlen, w), F32) for w in widths],
        compiler_params=pltpu.CompilerParams(vmem_limit_bytes=VMEM_LIMIT,
                                             dimension_semantics=("parallel", "parallel")),
        name="in_proj",
    )(x, mod, mod, norm_w.reshape(1, 1, D_MODEL), w_r)


def _outmlp_body(final, x_ref, y_ref, gt1_ref, sh_ref, sc_ref, gt2_ref, nw_ref, fnw_ref,
                 wo_ref, wu_ref, wd_ref, o_ref):
    bb, tt, _ = x_ref.shape
    rows = bb * tt
    att = jnp.dot(y_ref[...].reshape(rows, D_MODEL).astype(MXU_DT), wo_ref[...],
                  preferred_element_type=F32)
    x1 = x_ref[...] + gt1_ref[...] * att.reshape(bb, tt, D_MODEL)
    h = _rms(x1) * nw_ref[...]
    h = h * (1.0 + sc_ref[...]) + sh_ref[...]
    a = jnp.dot(h.reshape(rows, D_MODEL).astype(MXU_DT), wu_ref[...], preferred_element_type=F32)
    a = jnp.square(jnp.maximum(a, 0.0))
    f = jnp.dot(a.astype(MXU_DT), wd_ref[...], preferred_element_type=F32)
    x2 = x1 + gt2_ref[...] * f.reshape(bb, tt, D_MODEL)
    if final:
        x2 = _rms(x2) * fnw_ref[...]
    o_ref[...] = x2


def _outmlp_call(x, y_mix, mod, norm_w, final_norm_w, w_out, w_up, w_down, layer, final):
    bsz, tlen, _ = x.shape
    bb, tt = _row_blocks(bsz, tlen)

    def xmap(i, j):
        return (i, j, 0)

    def modspec(k):
        return pl.BlockSpec((bb, 1, D_MODEL), lambda i, j: (i, 0, k))

    def wspec(shape):
        return pl.BlockSpec((None,) + shape, lambda i, j: (layer, 0, 0), pipeline_mode=pl.Buffered(1))

    vec = pl.BlockSpec((1, 1, D_MODEL), lambda i, j: (0, 0, 0))
    return pl.pallas_call(
        functools.partial(_outmlp_body, final),
        grid=(bsz // bb, tlen // tt),
        in_specs=[pl.BlockSpec((bb, tt, D_MODEL), xmap), pl.BlockSpec((bb, tt, D_MODEL), xmap),
                  modspec(2), modspec(3), modspec(4), modspec(5), vec, vec,
                  wspec((D_MODEL, D_MODEL)), wspec((D_MODEL, D_FF)), wspec((D_FF, D_MODEL))],
        out_specs=pl.BlockSpec((bb, tt, D_MODEL), xmap),
        out_shape=jax.ShapeDtypeStruct((bsz, tlen, D_MODEL), F32),
        compiler_params=pltpu.CompilerParams(vmem_limit_bytes=VMEM_LIMIT,
                                             dimension_semantics=("parallel", "parallel")),
        name="out_mlp",
    )(x, y_mix, mod, mod, mod, mod, norm_w.reshape(1, 1, D_MODEL),
      final_norm_w.reshape(1, 1, D_MODEL), w_out, w_up, w_down)


(V_RW_W0, V_RW_A0, V_RW_KK, V_RW_KA, V_RW_RK, V_RW_LNW, V_RW_LNB,
 V_GLA_B, V_GLA_NW, V_DN_NW, V_SSM_D, V_SSM_NW) = range(12)
P_DN_ALOG, P_DN_DTB, P_SSM_ALOG, P_SSM_DTB = range(4)
CV_DN_W, CV_SSM_W, CV_SSM_B = 0, 4, 8
LR_RW_W2, LR_RW_A2, LR_RW_G2, LR_GLA_GK = range(4)


def _blockdiag(a, b):
    za = jnp.zeros_like(a)
    return jnp.concatenate([jnp.concatenate([a, za], axis=1),
                            jnp.concatenate([za, b], axis=1)], axis=0)


def _mixer_body(C, n_t, G, n_prev, *refs):
    (u_rw_ref, u_gla_ref, u_dn_ref, u_ssm_ref, misc_ref,
     shift0_ref, wkv0_ref, gla0_ref, dnc0_ref, dn0_ref, ssc0_ref, ssm0_ref,
     mu_ref, v256_ref, v128_ref, conv_ref, lr_ref) = refs[:17]
    prev_refs = refs[17:17 + 7 * n_prev]
    y_ref = refs[17 + 7 * n_prev]
    out_refs = refs[18 + 7 * n_prev:25 + 7 * n_prev]
    xp_rw, xp_dn, xp_ss, s_wkv, s_gla, s_dn, s_ssm = refs[25 + 7 * n_prev:]
    shift1_ref, wkv1_ref, gla1_ref, dnc1_ref, dn1_ref, ssc1_ref, ssm1_ref = (r.at[n_prev] for r in out_refs)
    t = pl.program_id(1)

    @pl.when(t == 0)
    def _init():
        for g in range(G):
            xp_rw[g, 0:SUBLANES, :] = shift0_ref[g]
            xp_dn[g, 0:SUBLANES, :] = dnc0_ref[g]
            xp_ss[g, 0:SUBLANES, :] = ssc0_ref[g]
            for p in range(2):
                s_wkv[g, p] = _blockdiag(wkv0_ref[g, 2 * p], wkv0_ref[g, 2 * p + 1])
                s_gla[g, p] = _blockdiag(gla0_ref[g, 2 * p], gla0_ref[g, 2 * p + 1]).T
                s_dn[g, p] = _blockdiag(dn0_ref[g, 2 * p], dn0_ref[g, 2 * p + 1])
                s_ssm[g, p] = jnp.concatenate([ssm0_ref[g, 2 * p], ssm0_ref[g, 2 * p + 1]], axis=0)

    ri = lax.broadcasted_iota(jnp.int32, (C, C), 0)
    ci = lax.broadcasted_iota(jnp.int32, (C, C), 1)
    incl = ci <= ri
    ri2 = lax.broadcasted_iota(jnp.int32, (C, 2 * C), 0)
    ci2 = lax.broadcasted_iota(jnp.int32, (C, 2 * C), 1)
    left2 = ci2 < C
    cj2 = ci2 & (C - 1)
    incl2 = cj2 <= ri2
    strict2 = cj2 < ri2
    incl4 = ((lax.broadcasted_iota(jnp.int32, (C, 4 * C), 1) & (C - 1))
             <= lax.broadcasted_iota(jnp.int32, (C, 4 * C), 0))
    tri = jnp.where(incl, 1.0, 0.0).astype(MXU_DT)
    lane = lax.broadcasted_iota(jnp.int32, (1, LANES), 1)
    hmask = (lane < HD, lane >= HD)
    r128 = lax.shift_right_logical(lax.broadcasted_iota(jnp.int32, (LANES, LANES), 0), 6)
    c128 = lax.shift_right_logical(lax.broadcasted_iota(jnp.int32, (LANES, LANES), 1), 6)
    bd = r128 == c128
    r256 = lax.broadcasted_iota(jnp.int32, (GW, GW), 0)
    c256 = lax.broadcasted_iota(jnp.int32, (GW, GW), 1)
    blk64 = jnp.where(lax.shift_right_logical(r256, 6) == lax.shift_right_logical(c256, 6),
                      1.0, 0.0).astype(MXU_DT)
    blk128 = jnp.where(lax.shift_right_logical(r256, 7) == lax.shift_right_logical(c256, 7),
                       1.0, 0.0).astype(MXU_DT)
    n_iter = 0
    while 2 * (1 << n_iter) < C:
        n_iter += 1

    def rows(*xs):
        return jnp.concatenate(xs, axis=0)

    def vrow(i):
        return v256_ref[i:i + 1, :]

    def hsum(x):
        return _mm_sel_rhs(x, blk64, n=1)

    def head(x, h):
        return jnp.where(hmask[h], x, 0.0)

    def hrows(x):
        return rows(head(x, 0), head(x, 1))

    def pair_nt(a, b):
        return _mm_nt(a, hrows(b))

    def pair_apply(q2, y):
        return _mm(q2, hrows(y))

    def bd2(q2):
        return rows(jnp.where(left2, q2, 0.0), jnp.where(left2, 0.0, q2))

    def seg2(tile, tile_t, lane0):
        col = jnp.where(left2, tile[:, lane0:lane0 + 1], tile[:, lane0 + 1:lane0 + 2])
        row = jnp.concatenate([tile_t[lane0:lane0 + 1, :], tile_t[lane0 + 1:lane0 + 2, :]], axis=1)
        return _seg_decay(col, row, incl2)

    def expand(tile, base):
        cols = [jnp.broadcast_to(tile[:, base + h:base + h + 1], (C, LANES)) for h in range(NH)]
        return [jnp.where(hmask[0], cols[0], cols[1]), jnp.where(hmask[0], cols[2], cols[3])]

    row8 = lax.broadcasted_iota(jnp.int32, (SUBLANES, 1), 0)

    def shifted(x, prev8, k):
        top = jnp.where(row8 < k, pltpu.roll(prev8, k, 0), pltpu.roll(x, k, 0)[0:SUBLANES])
        if C == SUBLANES:
            return top
        return rows(top, pltpu.roll(x, k, 0)[SUBLANES:])

    def psl(p):
        return slice(LANES * p, LANES * (p + 1))

    shared = {}

    def neumann(n_list, steps):
        rs = list(n_list)
        ps = list(n_list)
        bps = [bd2(p) for p in ps]
        for _ in range(steps):
            ps = [_mm(p, bp) for p, bp in zip(ps, bps)]
            yield None
            bps = [bd2(p) for p in ps]
            rs = [r + p + _mm(r, bp) for r, p, bp in zip(rs, ps, bps)]
        yield rs

    def rwkv(g):
        u_rw = u_rw_ref[g]
        u_prev = shifted(u_rw, xp_rw[g, 0:SUBLANES, :], 1)
        xp_rw[g, 0:SUBLANES, :] = u_rw[C - SUBLANES:C, :]
        xs = u_rw + (u_prev - u_rw) * mu_ref[...]
        r = xs[:, 0:GW]
        k = xs[:, GW:2 * GW]
        v = xs[:, 2 * GW:3 * GW]
        x7 = xs[:, 3 * GW:RWKV_COLS]
        w_pre = _mm(jnp.tanh(x7), lr_ref[LR_RW_W2])
        a_pre = _mm(x7, lr_ref[LR_RW_A2])
        gate = _mm(jax.nn.sigmoid(x7), lr_ref[LR_RW_G2])
        kk = k * vrow(V_RW_KK)
        kk_ss = hsum(kk * kk)
        yield
        log_w = -jax.nn.softplus(-(vrow(V_RW_W0) + w_pre)) - 0.5
        lw = -jnp.exp(log_w)
        a = jax.nn.sigmoid(vrow(V_RW_A0) + a_pre)
        kk = kk * lax.rsqrt(kk_ss + EPS)
        k2 = k * (1.0 + (a - 1.0) * vrow(V_RW_KA))
        bcum = _mm_sel_lhs(tri, lw)
        bonus = hsum(r * k2 * vrow(V_RW_RK))
        yield
        e_nb = jnp.exp(-bcum)
        blast = bcum[C - 1:C, :]
        e_rem = jnp.exp(blast - bcum)
        nka = -kk * a
        r_t = r * jnp.exp(bcum)
        a_t = kk * jnp.exp(bcum - lw)
        b_t = nka * e_nb
        k_t = k2 * e_nb
        k_rem = k2 * e_rem
        b_rem = nka * e_rem
        s = [s_wkv[g, p] for p in range(2)]
        ars = [_mm_nt(rows(a_t[:, psl(p)], r_t[:, psl(p)]), s[p]) for p in range(2)]
        ar = [_mm_nt(rows(a_t[:, psl(p)], r_t[:, psl(p)]), rows(hrows(b_t[:, psl(p)]), hrows(k_t[:, psl(p)])))
              for p in range(2)]
        yield
        n_ab = [jnp.where(strict2, ar[p][0:C, 0:2 * C], 0.0) for p in range(2)]
        n_ak = [jnp.where(strict2, ar[p][0:C, 2 * C:4 * C], 0.0) for p in range(2)]
        a_r = [jnp.where(incl4, ar[p][C:2 * C], 0.0) for p in range(2)]
        rhs = [ars[p][0:C] + pair_apply(n_ak[p], v[:, psl(p)]) for p in range(2)]
        tinv = None
        for step in neumann(n_ab, n_iter):
            if step is None:
                yield
            else:
                tinv = step
        yield
        uh = [rhs[p] + pair_apply(tinv[p], rhs[p]) for p in range(2)]
        yield
        y_pairs = []
        for p in range(2):
            sl = psl(p)
            vp = v[:, sl]
            u_p = uh[p]
            yp = ars[p][C:2 * C] + _mm(a_r[p], rows(hrows(u_p), hrows(vp)))
            s_new = s[p] * jnp.exp(blast[:, sl]) + _mm_tn(rows(vp, u_p), rows(k_rem[:, sl], b_rem[:, sl]))
            s_wkv[g, p] = jnp.where(bd, s_new, 0.0)
            y_pairs.append(yp)
        yield
        y = jnp.concatenate(y_pairs, axis=1)
        mean = hsum(y) * (1.0 / HD)
        yield
        yc = y - mean
        var = hsum(yc * yc) * (1.0 / HD)
        yield
        y = yc * lax.rsqrt(var + RWKV_GN_EPS) * vrow(V_RW_LNW) + vrow(V_RW_LNB)
        y_ref[g, :, 0:GW] = (y + bonus * v) * gate

    def gla(g):
        u_gla = u_gla_ref[g]
        q = u_gla[:, 0:GW] * (HD ** -0.5)
        k = u_gla[:, GW:2 * GW]
        v = u_gla[:, 2 * GW:3 * GW]
        gz = u_gla[:, 3 * GW:4 * GW]
        gate = _mm(misc_ref[g], lr_ref[LR_GLA_GK])
        yield
        la = jax.nn.log_sigmoid(gate + vrow(V_GLA_B)) * (1.0 / GLA_GATE_NORM)
        bcum = _mm_sel_lhs(tri, la)
        yield
        blast = bcum[C - 1:C, :]
        q_in = q * jnp.exp(bcum)
        k_in = k * jnp.exp(-bcum)
        k_out = k * jnp.exp(blast - bcum)
        st = [s_gla[g, p] for p in range(2)]
        qs = [_mm_nt(q_in[:, psl(p)], st[p]) for p in range(2)]
        a_h = [pair_nt(q_in[:, psl(p)], k_in[:, psl(p)]) for p in range(2)]
        for p in range(2):
            sl = psl(p)
            st_new = st[p] * jnp.exp(blast[:, sl]) + _mm_tn(v[:, sl], k_out[:, sl])
            s_gla[g, p] = jnp.where(bd, st_new, 0.0)
        yield
        o_pairs = []
        for p in range(2):
            o_pairs.append(qs[p] + pair_apply(jnp.where(incl2, a_h[p], 0.0), v[:, psl(p)]))
        yield
        o = jnp.concatenate(o_pairs, axis=1)
        ss = hsum(o * o)
        yield
        o = o * lax.rsqrt(ss * (1.0 / HD) + EPS) * vrow(V_GLA_NW)
        y_ref[g, :, GW:2 * GW] = o * _silu(gz)

    def dn(g):
        misc = misc_ref[g]
        g_dn = -jnp.exp(v128_ref[P_DN_ALOG:P_DN_ALOG + 1, :]) * jax.nn.softplus(
            misc + v128_ref[P_DN_DTB:P_DN_DTB + 1, :])
        dt_ss = jax.nn.softplus(misc + v128_ref[P_SSM_DTB:P_SSM_DTB + 1, :])
        la_ss = dt_ss * -jnp.exp(v128_ref[P_SSM_ALOG:P_SSM_ALOG + 1, :])
        beta = jax.nn.sigmoid(misc)
        cum = _mm_sel_lhs(tri, jnp.where(lane < MISC_DN_B, g_dn, la_ss))
        u_dn = u_dn_ref[g]
        raw = u_dn[:, 0:CONV_COLS]
        prev8 = xp_dn[g, 0:SUBLANES, :]
        qkv = raw * conv_ref[CV_DN_W + 3:CV_DN_W + 4, :]
        for k in (1, 2, 3):
            qkv = qkv + shifted(raw, prev8, k) * conv_ref[CV_DN_W + 3 - k:CV_DN_W + 4 - k, :]
        xp_dn[g, 0:SUBLANES, :] = raw[C - SUBLANES:C, :]
        qkv = _silu(qkv)
        q = qkv[:, 0:GW]
        k = qkv[:, GW:2 * GW]
        v = qkv[:, 2 * GW:3 * GW]
        z = u_dn[:, CONV_COLS:CONV_COLS + GW]
        q_ss = hsum(q * q)
        k_ss = hsum(k * k)
        yield
        cum_t = cum.T
        shared[g] = (cum, cum_t, dt_ss)
        q = q * lax.rsqrt(q_ss + EPS) * (HD ** -0.5)
        k = k * lax.rsqrt(k_ss + EPS)
        beta_x = expand(beta, MISC_DN_B)
        cum_x = expand(cum, MISC_DN_A)
        s = [s_dn[g, p] for p in range(2)]
        ecx, kb, vb, kbe, qs, kq = [], [], [], [], [], []
        for p in range(2):
            sl = psl(p)
            ecx.append(jnp.exp(cum_x[p]))
            kb.append(k[:, sl] * beta_x[p])
            vb.append(v[:, sl] * beta_x[p])
            kbe.append(kb[p] * ecx[p])
            qs.append(_mm(q[:, sl] * ecx[p], s[p]))
            kq.append(pair_nt(rows(kb[p], q[:, sl]), k[:, sl]))
        yield
        n_m, a_l = [], []
        for p in range(2):
            dec = seg2(cum, cum_t, MISC_DN_A + 2 * p)
            n_m.append(jnp.where(strict2, -(kq[p][0:C] * dec), 0.0))
            a_l.append(kq[p][C:2 * C] * dec)
        tinv = None
        for step in neumann(n_m, n_iter - 1):
            if step is None:
                yield
            else:
                tinv = step
        yield
        err = [n_m[p] - tinv[p] + _mm3(n_m[p], bd2(tinv[p])) for p in range(2)]
        yield
        tinv = [tinv[p] + err[p] + _mm(tinv[p], bd2(err[p])) for p in range(2)]
        yield
        vk = [jnp.concatenate([vb[p], kbe[p]], axis=1) for p in range(2)]
        uw = [vk[p] + _mm(tinv[p], jnp.concatenate([hrows(vb[p]), hrows(kbe[p])], axis=1)) for p in range(2)]
        yield
        u_p = [uw[p][:, 0:LANES] for p in range(2)]
        w_p = [uw[p][:, LANES:2 * LANES] for p in range(2)]
        ws = [_mm(w_p[p], s[p]) for p in range(2)]
        yield
        o_pairs = []
        for p in range(2):
            sl = psl(p)
            cx = cum_x[p]
            last = cx[C - 1:C, :]
            v_new = u_p[p] - ws[p]
            op = qs[p] + pair_apply(a_l[p], v_new)
            s_new = s[p] * jnp.exp(last) + _mm_tn(k[:, sl] * jnp.exp(last - cx), v_new)
            s_dn[g, p] = jnp.where(bd, s_new, 0.0)
            o_pairs.append(op)
        yield
        o = jnp.concatenate(o_pairs, axis=1)
        ss = hsum(o * o)
        yield
        o = o * lax.rsqrt(ss * (1.0 / HD) + EPS) * vrow(V_DN_NW)
        y_ref[g, :, 2 * GW:3 * GW] = o * _silu(z)

    def ssd(g):
        u_ssm = u_ssm_ref[g]
        z = u_ssm[:, 0:GW]
        raw = u_ssm[:, GW:GW + CONV_COLS]
        prev8 = xp_ss[g, 0:SUBLANES, :]
        xbc = conv_ref[CV_SSM_B:CV_SSM_B + 1, :] + raw * conv_ref[CV_SSM_W + 3:CV_SSM_W + 4, :]
        for k in (1, 2, 3):
            xbc = xbc + shifted(raw, prev8, k) * conv_ref[CV_SSM_W + 3 - k:CV_SSM_W + 4 - k, :]
        xp_ss[g, 0:SUBLANES, :] = raw[C - SUBLANES:C, :]
        xbc = _silu(xbc)
        xs_ = xbc[:, 0:GW]
        bm = xbc[:, GW:2 * GW]
        cm = xbc[:, 2 * GW:3 * GW]
        s = [s_ssm[g, p] for p in range(2)]
        gmat = [_mm_nt(cm[:, psl(p)], rows(bm[:, psl(p)], bm[:, psl(p)])) for p in range(2)]
        cs = [_mm_nt(cm[:, psl(p)], s[p]) for p in range(2)]
        yield
        yield
        cum, cum_t, dt_ss = shared[g]
        dt_x = expand(dt_ss, MISC_SSM_DT)
        cum_x = expand(cum, MISC_SSM_DT)
        y_pairs = []
        for p in range(2):
            sl = psl(p)
            xh = xs_[:, sl]
            xdt = xh * dt_x[p]
            cx = cum_x[p]
            last = cx[C - 1:C, :]
            yp = jnp.exp(cx) * cs[p] + v256_ref[V_SSM_D:V_SSM_D + 1, sl] * xh
            yp = yp + pair_apply(gmat[p] * seg2(cum, cum_t, MISC_SSM_DT + 2 * p), xdt)
            lane0 = MISC_SSM_DT + 2 * p
            dcol = rows(jnp.broadcast_to(jnp.exp(cum[C - 1:C, lane0:lane0 + 1]), (HD, SSM_STATE)),
                        jnp.broadcast_to(jnp.exp(cum[C - 1:C, lane0 + 1:lane0 + 2]), (HD, SSM_STATE)))
            s_ssm[g, p] = s[p] * dcol + _mm_tn(xdt * jnp.exp(last - cx), bm[:, sl])
            y_pairs.append(yp)
        yield
        y = jnp.concatenate(y_pairs, axis=1) * _silu(z)
        ss = _mm_sel_rhs(y * y, blk128)
        yield
        y_ref[g, :, 3 * GW:4 * GW] = y * lax.rsqrt(ss * (1.0 / (2 * HD)) + EPS) * vrow(V_SSM_NW)

    active = []
    for g in range(G):
        active += [rwkv(g), dn(g), gla(g), ssd(g)]
    while active:
        alive = []
        for gen in active:
            try:
                next(gen)
                alive.append(gen)
            except StopIteration:
                pass
        active = alive

    @pl.when(t == n_t - 1)
    def _fin():
        for j in range(n_prev):
            for i in range(7):
                out_refs[i][j] = prev_refs[7 * j + i][...]
        for g in range(G):
            shift1_ref[g] = xp_rw[g, 0:SUBLANES, :]
            dnc1_ref[g] = xp_dn[g, 0:SUBLANES, :]
            ssc1_ref[g] = xp_ss[g, 0:SUBLANES, :]
            for p in range(2):
                sw = s_wkv[g, p]
                sg = s_gla[g, p].T
                sd = s_dn[g, p]
                ss = s_ssm[g, p]
                for h in range(2):
                    hs = slice(HD * h, HD * (h + 1))
                    wkv1_ref[g, 2 * p + h] = sw[hs, hs]
                    gla1_ref[g, 2 * p + h] = sg[hs, hs]
                    dn1_ref[g, 2 * p + h] = sd[hs, hs]
                    ssm1_ref[g, 2 * p + h] = ss[hs, :]


def _mixer_call(u_parts, states, mparams, layer, prev=()):
    u_rw, u_gla, u_dn, u_ssm, misc = u_parts
    shift0, wkv0, gla0, dnc0, dn0, ssc0, ssm0 = states
    bsz, tlen, _ = u_rw.shape
    C = math.gcd(tlen, PROMPT_CHUNK)
    n_t = tlen // C
    G = SEQS_PER_STEP_LONG if n_t > 1 else SEQS_PER_STEP_SHORT
    assert bsz % G == 0 and C & (C - 1) == 0

    def tmap(b, t):
        return (b, t, 0)

    def bmap3(b, t):
        return (b, 0, 0)

    def bmap4(b, t):
        return (b, 0, 0, 0)

    def full(arr):
        nd = arr.ndim
        return pl.BlockSpec(arr.shape, lambda b, t: (0,) * nd)

    def lmap3(b, t):
        return (layer, b, 0, 0)

    def lmap4(b, t):
        return (layer, b, 0, 0, 0)

    sq = (G, NH, HD, HD)
    in_state_specs = [pl.BlockSpec((None, G, SUBLANES, RWKV_COLS), lmap3), pl.BlockSpec((None,) + sq, lmap4),
                      pl.BlockSpec((None,) + sq, lmap4), pl.BlockSpec((None, G, SUBLANES, CONV_COLS), lmap3),
                      pl.BlockSpec((None,) + sq, lmap4), pl.BlockSpec((None, G, SUBLANES, CONV_COLS), lmap3),
                      pl.BlockSpec((None, G, NH, HD, SSM_STATE), lmap4)]
    n_prev = len(prev)
    n_out = n_prev + 1
    tails = [(SUBLANES, RWKV_COLS), (NH, HD, HD), (NH, HD, HD), (SUBLANES, CONV_COLS), (NH, HD, HD),
             (SUBLANES, CONV_COLS), (NH, HD, SSM_STATE)]
    prev_specs = [pl.BlockSpec((G,) + tl, bmap3 if len(tl) == 2 else bmap4) for tl in tails] * n_prev
    state_specs = [pl.BlockSpec((n_out, G) + tl, (lambda b, t: (0, b, 0, 0)) if len(tl) == 2
                                else (lambda b, t: (0, b, 0, 0, 0))) for tl in tails]
    state_shapes = [jax.ShapeDtypeStruct((n_out, bsz) + tl, F32) for tl in tails]
    outs = pl.pallas_call(
        functools.partial(_mixer_body, C, n_t, G, n_prev),
        grid=(bsz // G, n_t),
        in_specs=[pl.BlockSpec((G, C, RWKV_COLS), tmap), pl.BlockSpec((G, C, 4 * GW), tmap),
                  pl.BlockSpec((G, C, 4 * GW), tmap), pl.BlockSpec((G, C, 4 * GW), tmap),
                  pl.BlockSpec((G, C, LANES), tmap)] + in_state_specs + [full(a) for a in mparams] + prev_specs,
        out_specs=[pl.BlockSpec((G, C, D_MODEL), tmap)] + state_specs,
        out_shape=[jax.ShapeDtypeStruct((bsz, tlen, D_MODEL), F32)] + state_shapes,
        scratch_shapes=[pltpu.VMEM((G, SUBLANES, RWKV_COLS), F32),
                        pltpu.VMEM((G, SUBLANES, CONV_COLS), F32),
                        pltpu.VMEM((G, SUBLANES, CONV_COLS), F32),
                        pltpu.VMEM((G, 2, LANES, LANES), F32), pltpu.VMEM((G, 2, LANES, LANES), F32),
                        pltpu.VMEM((G, 2, LANES, LANES), F32), pltpu.VMEM((G, 2, LANES, SSM_STATE), F32)],
        compiler_params=pltpu.CompilerParams(vmem_limit_bytes=VMEM_LIMIT,
                                             dimension_semantics=("parallel", "arbitrary")),
        name="mixers",
    )(u_rw, u_gla, u_dn, u_ssm, misc, shift0, wkv0, gla0, dnc0, dn0, ssc0, ssm0, *mparams,
      *[a for layer_states in prev for a in layer_states])
    return outs[0], tuple(outs[1:])


def _pad_rows(m, lo, total):
    return jnp.pad(m, ((lo, total - lo - m.shape[0]), (0, 0)))


def _lane_vec(vals, lo):
    return jnp.pad(vals, (lo, LANES - lo - vals.shape[0]))


def _pack_layer(P, l):
    v256 = jnp.stack([P['rwkv_w0'][l], P['rwkv_a0'][l], P['rwkv_k_k'][l], P['rwkv_k_a'][l],
                      P['rwkv_r_k'][l], P['rwkv_ln_w'][l], P['rwkv_ln_b'][l],
                      P['gla_gk_b'][l], P['gla_norm_w'][l], P['dn_norm_w'][l],
                      jnp.repeat(P['ssm_D'][l], HD), P['ssm_norm_w'][l]])
    v256 = jnp.pad(v256, ((0, 16 - v256.shape[0]), (0, 0)))
    v128 = jnp.stack([_lane_vec(P['dn_A_log'][l], MISC_DN_A), _lane_vec(P['dn_dt_bias'][l], MISC_DN_A),
                      _lane_vec(P['ssm_A_log'][l], MISC_SSM_DT), _lane_vec(P['ssm_dt_bias'][l], MISC_SSM_DT)])
    v128 = jnp.pad(v128, ((0, 4), (0, 0)))
    conv = jnp.concatenate([P['dn_conv_w'][l], P['ssm_conv_w'][l], P['ssm_conv_b'][l][None],
                            jnp.zeros((7, CONV_COLS), F32)], axis=0)
    lr = jnp.stack([_pad_rows(P['rwkv_w2'][l], 0, LANES), _pad_rows(P['rwkv_a2'][l], 32, LANES),
                    _pad_rows(P['rwkv_g2'][l], 64, LANES),
                    _pad_rows(P['gla_gk_w2'][l], MISC_GLA_GATE, LANES)]).astype(MXU_DT)
    return (P['rwkv_mu'][l][None], v256, v128, conv, lr)


def _pad_tail_rows(a):
    return jnp.pad(a, ((0, 0), (0, 0), (SUBLANES - a.shape[2], 0), (0, 0)))


def _trunk(x, mod_rows, states, mixer_params, dense, final_norm_w):
    n_layers = len(mixer_params)
    shift0, wkv0, gla0, dnc0, dn0, ssc0, ssm0 = states
    st_in = (_pad_tail_rows(shift0[:, :, None, :]), wkv0, gla0, _pad_tail_rows(dnc0), dn0,
             _pad_tail_rows(ssc0), ssm0)
    prev = []
    for l in range(n_layers):
        mod = mod_rows[l][:, None, :]
        u_parts = _inproj_call(x, mod, dense['norm1'][l], dense['w_r'], l)
        last = l == n_layers - 1
        y_mix, st = _mixer_call(u_parts, st_in, mixer_params[l], l, prev=tuple(prev) if last else ())
        x = _outmlp_call(x, y_mix, mod, dense['norm2'][l], final_norm_w, dense['w_out'],
                         dense['w_up'], dense['w_down'], l, final=last)
        if not last:
            prev.append(tuple(s[0] for s in st))
    shift1, wkv1, gla1, dnc1, dn1, ssc1, ssm1 = st
    return x, (shift1[:, :, SUBLANES - 1], wkv1, gla1, dnc1[:, :, SUBLANES - 3:], dn1,
               ssc1[:, :, SUBLANES - 3:], ssm1)


def kernel(x_prompt, x_sample, state_rwkv_shift, state_rwkv_wkv, state_gla, state_dn_conv,
           state_dn, state_ssm_conv, state_ssm, c_prompt, c_sample,
           ada_w, ada_b, norm1_w, norm2_w, w_in, w_out, w_up, w_down,
           rwkv_mu, rwkv_w0, rwkv_w2, rwkv_a0, rwkv_a2, rwkv_g2, rwkv_k_k, rwkv_k_a, rwkv_r_k,
           rwkv_ln_w, rwkv_ln_b, gla_gk_w2, gla_gk_b, gla_norm_w,
           dn_conv_w, dn_A_log, dn_dt_bias, dn_norm_w,
           ssm_conv_w, ssm_conv_b, ssm_dt_bias, ssm_A_log, ssm_D, ssm_norm_w, final_norm_w):
    P = dict(rwkv_mu=rwkv_mu, rwkv_w0=rwkv_w0, rwkv_w2=rwkv_w2, rwkv_a0=rwkv_a0, rwkv_a2=rwkv_a2,
             rwkv_g2=rwkv_g2, rwkv_k_k=rwkv_k_k, rwkv_k_a=rwkv_k_a, rwkv_r_k=rwkv_r_k,
             rwkv_ln_w=rwkv_ln_w, rwkv_ln_b=rwkv_ln_b, gla_gk_w2=gla_gk_w2, gla_gk_b=gla_gk_b,
             gla_norm_w=gla_norm_w, dn_conv_w=dn_conv_w, dn_A_log=dn_A_log, dn_dt_bias=dn_dt_bias,
             dn_norm_w=dn_norm_w, ssm_conv_w=ssm_conv_w, ssm_conv_b=ssm_conv_b,
             ssm_dt_bias=ssm_dt_bias, ssm_A_log=ssm_A_log, ssm_D=ssm_D, ssm_norm_w=ssm_norm_w)
    n_layers = w_in.shape[0]
    n_prompt = x_prompt.shape[0]
    mixer_params = [_pack_layer(P, l) for l in range(n_layers)]
    dense = dict(w_r=_wprep_call(w_in), w_out=w_out.astype(MXU_DT), w_up=w_up.astype(MXU_DT),
                 w_down=w_down.astype(MXU_DT), norm1=norm1_w, norm2=norm2_w)
    mod_all = _ada_call(jnp.concatenate([c_prompt, c_sample], axis=0), ada_w, ada_b)
    sample_states = (state_rwkv_shift, state_rwkv_wkv, state_gla, state_dn_conv,
                     state_dn, state_ssm_conv, state_ssm)
    prompt_states = tuple(jnp.zeros((n_layers, n_prompt) + s.shape[2:], F32) for s in sample_states)
    y_prompt, ps = _trunk(x_prompt, mod_all[:, :n_prompt], prompt_states, mixer_params, dense, final_norm_w)
    y_sample, ss = _trunk(x_sample, mod_all[:, n_prompt:], sample_states, mixer_params, dense, final_norm_w)
    return (y_prompt, y_sample) + ps + ss
```

```python
import functools
import math

import jax
import jax.numpy as jnp
from jax import lax
from jax.experimental import pallas as pl
from jax.experimental.pallas import tpu as pltpu

F32 = jnp.float32
MXU_DT = jnp.bfloat16

D_MODEL = 1024
NH = 4
HD = 64
GW = NH * HD
D_FF = 4 * D_MODEL
SSM_STATE = 128
EPS = 1e-6
RWKV_GN_EPS = 64e-5
GLA_GATE_NORM = 16.0
GLA_SUB = 16
RWKV_COLS = 3 * GW + 32 + 32 + 64
CONV_COLS = 3 * GW
U_COLS = RWKV_COLS + 3 * 4 * GW + 128
MISC_GLA_GATE = 0
MISC_DN_A = 16
MISC_DN_B = 20
MISC_SSM_DT = 24

LANES = 128
SUBLANES = 8
VMEM_LIMIT = 56 * 1024 * 1024
PROMPT_CHUNK = 64
ROW_TILE = 512
SEQS_PER_STEP_LONG = 4
SEQS_PER_STEP_SHORT = 8


def _mm(a, b):
    return jnp.dot(a.astype(MXU_DT), b.astype(MXU_DT), preferred_element_type=F32)


def _mm_nt(a, b):
    return lax.dot_general(a.astype(MXU_DT), b.astype(MXU_DT), (((1,), (1,)), ((), ())),
                           preferred_element_type=F32)


def _mm_tn(a, b):
    return lax.dot_general(a.astype(MXU_DT), b.astype(MXU_DT), (((0,), (0,)), ((), ())),
                           preferred_element_type=F32)


def _mm3(a, b):
    a_hi, a_lo = _split(a, 2)
    b_hi, b_lo = _split(b, 2)
    return (jnp.dot(a_hi, b_hi, preferred_element_type=F32) + jnp.dot(a_hi, b_lo, preferred_element_type=F32)
            + jnp.dot(a_lo, b_hi, preferred_element_type=F32))


def _split(x, n):
    parts = []
    r = x
    for i in range(n):
        p = r.astype(MXU_DT)
        parts.append(p)
        if i + 1 < n:
            r = r - p.astype(F32)
    return parts


def _mm_sel_lhs(sel, x, n=2):
    acc = None
    for p in _split(x, n):
        d = jnp.dot(sel, p, preferred_element_type=F32)
        acc = d if acc is None else acc + d
    return acc


def _mm_sel_rhs(x, sel, n=2):
    acc = None
    for p in _split(x, n):
        d = jnp.dot(p, sel, preferred_element_type=F32)
        acc = d if acc is None else acc + d
    return acc


def _seg_decay(cum_col, cum_row, incl):
    d = cum_col - cum_row
    return jnp.where(incl, jnp.exp(jnp.where(incl, d, 0.0)), 0.0)


def _silu(x):
    return x * jax.nn.sigmoid(x)


def _ada_body(c_ref, w_ref, b_ref, o_ref):
    c = c_ref[...]
    o_ref[0] = _mm(_silu(c), w_ref[0]) + b_ref[0]


def _ada_call(c_all, ada_w, ada_b):
    n_layers = ada_w.shape[0]
    rows = c_all.shape[0]
    tn = 1536
    return pl.pallas_call(
        _ada_body,
        grid=(n_layers, 6 * D_MODEL // tn),
        in_specs=[pl.BlockSpec((rows, D_MODEL), lambda l, j: (0, 0)),
                  pl.BlockSpec((1, D_MODEL, tn), lambda l, j: (l, 0, j)),
                  pl.BlockSpec((1, 1, tn), lambda l, j: (l, 0, j))],
        out_specs=pl.BlockSpec((1, rows, tn), lambda l, j: (l, 0, j)),
        out_shape=jax.ShapeDtypeStruct((n_layers, rows, 6 * D_MODEL), F32),
        compiler_params=pltpu.CompilerParams(vmem_limit_bytes=VMEM_LIMIT),
        name="ada_mod",
    )(c_all, ada_w, ada_b.reshape(n_layers, 1, 6 * D_MODEL))


W_IN_COLS = RWKV_COLS + (4 * GW + 16) + (4 * GW + 8) + (4 * GW + 4)


def _wprep_body(w_ref, o_ref):
    w = w_ref[0]
    o_gla = RWKV_COLS
    o_dn = o_gla + 4 * GW + 16
    o_ssm = o_dn + 4 * GW + 8
    parts = [w[:, 0:o_gla + 4 * GW],
             w[:, o_dn:o_dn + 4 * GW],
             w[:, o_ssm:o_ssm + 4 * GW],
             w[:, o_gla + 4 * GW:o_dn],
             w[:, o_dn + 4 * GW:o_ssm],
             w[:, o_ssm + 4 * GW:W_IN_COLS],
             jnp.zeros((w.shape[0], LANES - 28), F32)]
    o_ref[0] = jnp.concatenate(parts, axis=1).astype(MXU_DT)


def _wprep_call(w_in):
    n_layers = w_in.shape[0]
    tr = 128
    return pl.pallas_call(
        _wprep_body,
        grid=(n_layers, D_MODEL // tr),
        in_specs=[pl.BlockSpec((1, tr, W_IN_COLS), lambda l, i: (l, i, 0))],
        out_specs=pl.BlockSpec((1, tr, U_COLS), lambda l, i: (l, i, 0)),
        out_shape=jax.ShapeDtypeStruct((n_layers, D_MODEL, U_COLS), MXU_DT),
        compiler_params=pltpu.CompilerParams(vmem_limit_bytes=VMEM_LIMIT,
                                             dimension_semantics=("parallel", "parallel")),
        name="w_in_relayout",
    )(w_in)


def _rms(x):
    return x * lax.rsqrt(jnp.mean(x * x, -1, keepdims=True) + EPS)


def _inproj_body(x_ref, sh_ref, sc_ref, nw_ref, w_ref, o_rw, o_gla, o_dn, o_ssm, o_misc):
    bb, tt, _ = x_ref.shape
    h = _rms(x_ref[...]) * nw_ref[...]
    h = h * (1.0 + sc_ref[...]) + sh_ref[...]
    u = jnp.dot(h.reshape(bb * tt, D_MODEL).astype(MXU_DT), w_ref[...], preferred_element_type=F32)
    off = 0
    for ref in (o_rw, o_gla, o_dn, o_ssm, o_misc):
        w = ref.shape[-1]
        ref[...] = u[:, off:off + w].reshape(bb, tt, w)
        off += w


def _row_blocks(bsz, tlen):
    tt = min(tlen, ROW_TILE)
    bb = ROW_TILE // tt
    assert tlen % tt == 0 and bsz % bb == 0
    return bb, tt


def _inproj_call(x, mod, norm_w, w_r, layer):
    bsz, tlen, _ = x.shape
    bb, tt = _row_blocks(bsz, tlen)
    widths = (RWKV_COLS, 4 * GW, 4 * GW, 4 * GW, LANES)

    def xmap(i, j):
        return (i, j, 0)

    return pl.pallas_call(
        _inproj_body,
        grid=(bsz // bb, tlen // tt),
        in_specs=[pl.BlockSpec((bb, tt, D_MODEL), xmap),
                  pl.BlockSpec((bb, 1, D_MODEL), lambda i, j: (i, 0, 0)),
                  pl.BlockSpec((bb, 1, D_MODEL), lambda i, j: (i, 0, 1)),
                  pl.BlockSpec((1, 1, D_MODEL), lambda i, j: (0, 0, 0)),
                  pl.BlockSpec((None, D_MODEL, U_COLS), lambda i, j: (layer, 0, 0),
                               pipeline_mode=pl.Buffered(1))],
        out_specs=[pl.BlockSpec((bb, tt, w), xmap) for w in widths],
        out_shape=[jax.ShapeDtypeStruct((bsz, tlen, w), F32) for w in widths],
        compiler_params=pltpu.CompilerParams(vmem_limit_bytes=VMEM_LIMIT,
                                             dimension_semantics=("parallel", "parallel")),
        name="in_proj",
    )(x, mod, mod, norm_w.reshape(1, 1, D_MODEL), w_r)


def _outmlp_body(final, x_ref, y_ref, gt1_ref, sh_ref, sc_ref, gt2_ref, nw_ref, fnw_ref,
                 wo_ref, wu_ref, wd_ref, o_ref):
    bb, tt, _ = x_ref.shape
    rows = bb * tt
    att = jnp.dot(y_ref[...].reshape(rows, D_MODEL).astype(MXU_DT), wo_ref[...],
                  preferred_element_type=F32)
    x1 = x_ref[...] + gt1_ref[...] * att.reshape(bb, tt, D_MODEL)
    h = _rms(x1) * nw_ref[...]
    h = h * (1.0 + sc_ref[...]) + sh_ref[...]
    a = jnp.dot(h.reshape(rows, D_MODEL).astype(MXU_DT), wu_ref[...], preferred_element_type=F32)
    a = jnp.square(jnp.maximum(a, 0.0))
    f = jnp.dot(a.astype(MXU_DT), wd_ref[...], preferred_element_type=F32)
    x2 = x1 + gt2_ref[...] * f.reshape(bb, tt, D_MODEL)
    if final:
        x2 = _rms(x2) * fnw_ref[...]
    o_ref[...] = x2


def _outmlp_call(x, y_mix, mod, norm_w, final_norm_w, w_out, w_up, w_down, layer, final):
    bsz, tlen, _ = x.shape
    bb, tt = _row_blocks(bsz, tlen)

    def xmap(i, j):
        return (i, j, 0)

    def modspec(k):
        return pl.BlockSpec((bb, 1, D_MODEL), lambda i, j: (i, 0, k))

    def wspec(shape):
        return pl.BlockSpec((None,) + shape, lambda i, j: (layer, 0, 0), pipeline_mode=pl.Buffered(1))

    vec = pl.BlockSpec((1, 1, D_MODEL), lambda i, j: (0, 0, 0))
    return pl.pallas_call(
        functools.partial(_outmlp_body, final),
        grid=(bsz // bb, tlen // tt),
        in_specs=[pl.BlockSpec((bb, tt, D_MODEL), xmap), pl.BlockSpec((bb, tt, D_MODEL), xmap),
                  modspec(2), modspec(3), modspec(4), modspec(5), vec, vec,
                  wspec((D_MODEL, D_MODEL)), wspec((D_MODEL, D_FF)), wspec((D_FF, D_MODEL))],
        out_specs=pl.BlockSpec((bb, tt, D_MODEL), xmap),
        out_shape=jax.ShapeDtypeStruct((bsz, tlen, D_MODEL), F32),
        compiler_params=pltpu.CompilerParams(vmem_limit_bytes=VMEM_LIMIT,
                                             dimension_semantics=("parallel", "parallel")),
        name="out_mlp",
    )(x, y_mix, mod, mod, mod, mod, norm_w.reshape(1, 1, D_MODEL),
      final_norm_w.reshape(1, 1, D_MODEL), w_out, w_up, w_down)


(V_RW_W0, V_RW_A0, V_RW_KK, V_RW_KA, V_RW_RK, V_RW_LNW, V_RW_LNB,
 V_GLA_B, V_GLA_NW, V_DN_NW, V_SSM_D, V_SSM_NW) = range(12)
P_DN_ALOG, P_DN_DTB, P_SSM_ALOG, P_SSM_DTB = range(4)
CV_DN_W, CV_SSM_W, CV_SSM_B = 0, 4, 8
LR_RW_W2, LR_RW_A2, LR_RW_G2, LR_GLA_GK = range(4)


def _blockdiag(a, b):
    za = jnp.zeros_like(a)
    return jnp.concatenate([jnp.concatenate([a, za], axis=1),
                            jnp.concatenate([za, b], axis=1)], axis=0)


def _mixer_body(C, n_t, G, n_prev, *refs):
    (u_rw_ref, u_gla_ref, u_dn_ref, u_ssm_ref, misc_ref,
     shift0_ref, wkv0_ref, gla0_ref, dnc0_ref, dn0_ref, ssc0_ref, ssm0_ref,
     mu_ref, v256_ref, v128_ref, conv_ref, lr_ref) = refs[:17]
    prev_refs = refs[17:17 + 7 * n_prev]
    y_ref = refs[17 + 7 * n_prev]
    out_refs = refs[18 + 7 * n_prev:25 + 7 * n_prev]
    xp_rw, xp_dn, xp_ss, s_wkv, s_gla, s_dn, s_ssm = refs[25 + 7 * n_prev:]
    shift1_ref, wkv1_ref, gla1_ref, dnc1_ref, dn1_ref, ssc1_ref, ssm1_ref = (r.at[n_prev] for r in out_refs)
    t = pl.program_id(1)

    @pl.when(t == 0)
    def _init():
        for g in range(G):
            xp_rw[g, 0:SUBLANES, :] = shift0_ref[g]
            xp_dn[g, 0:SUBLANES, :] = dnc0_ref[g]
            xp_ss[g, 0:SUBLANES, :] = ssc0_ref[g]
            for p in range(2):
                s_wkv[g, p] = _blockdiag(wkv0_ref[g, 2 * p], wkv0_ref[g, 2 * p + 1])
                s_gla[g, p] = _blockdiag(gla0_ref[g, 2 * p], gla0_ref[g, 2 * p + 1]).T
                s_dn[g, p] = _blockdiag(dn0_ref[g, 2 * p], dn0_ref[g, 2 * p + 1])
                s_ssm[g, p] = jnp.concatenate([ssm0_ref[g, 2 * p], ssm0_ref[g, 2 * p + 1]], axis=0)

    ri = lax.broadcasted_iota(jnp.int32, (C, C), 0)
    ci = lax.broadcasted_iota(jnp.int32, (C, C), 1)
    incl = ci <= ri
    ri2 = lax.broadcasted_iota(jnp.int32, (C, 2 * C), 0)
    ci2 = lax.broadcasted_iota(jnp.int32, (C, 2 * C), 1)
    left2 = ci2 < C
    cj2 = ci2 & (C - 1)
    incl2 = cj2 <= ri2
    strict2 = cj2 < ri2
    incl4 = ((lax.broadcasted_iota(jnp.int32, (C, 4 * C), 1) & (C - 1))
             <= lax.broadcasted_iota(jnp.int32, (C, 4 * C), 0))
    tri = jnp.where(incl, 1.0, 0.0).astype(MXU_DT)
    lane = lax.broadcasted_iota(jnp.int32, (1, LANES), 1)
    hmask = (lane < HD, lane >= HD)
    r128 = lax.shift_right_logical(lax.broadcasted_iota(jnp.int32, (LANES, LANES), 0), 6)
    c128 = lax.shift_right_logical(lax.broadcasted_iota(jnp.int32, (LANES, LANES), 1), 6)
    bd = r128 == c128
    r256 = lax.broadcasted_iota(jnp.int32, (GW, GW), 0)
    c256 = lax.broadcasted_iota(jnp.int32, (GW, GW), 1)
    blk64 = jnp.where(lax.shift_right_logical(r256, 6) == lax.shift_right_logical(c256, 6),
                      1.0, 0.0).astype(MXU_DT)
    blk128 = jnp.where(lax.shift_right_logical(r256, 7) == lax.shift_right_logical(c256, 7),
                       1.0, 0.0).astype(MXU_DT)
    n_iter = 0
    while 2 * (1 << n_iter) < C:
        n_iter += 1

    def rows(*xs):
        return jnp.concatenate(xs, axis=0)

    def vrow(i):
        return v256_ref[i:i + 1, :]

    def hsum(x):
        return _mm_sel_rhs(x, blk64, n=1)

    def head(x, h):
        return jnp.where(hmask[h], x, 0.0)

    def hrows(x):
        return rows(head(x, 0), head(x, 1))

    def pair_nt(a, b):
        return _mm_nt(a, hrows(b))

    def pair_apply(q2, y):
        return _mm(q2, hrows(y))

    def bd2(q2):
        return rows(jnp.where(left2, q2, 0.0), jnp.where(left2, 0.0, q2))

    def seg2(tile, tile_t, lane0):
        col = jnp.where(left2, tile[:, lane0:lane0 + 1], tile[:, lane0 + 1:lane0 + 2])
        row = jnp.concatenate([tile_t[lane0:lane0 + 1, :], tile_t[lane0 + 1:lane0 + 2, :]], axis=1)
        return _seg_decay(col, row, incl2)

    def expand(tile, base):
        cols = [jnp.broadcast_to(tile[:, base + h:base + h + 1], (C, LANES)) for h in range(NH)]
        return [jnp.where(hmask[0], cols[0], cols[1]), jnp.where(hmask[0], cols[2], cols[3])]

    row8 = lax.broadcasted_iota(jnp.int32, (SUBLANES, 1), 0)

    def shifted(x, prev8, k):
        top = jnp.where(row8 < k, pltpu.roll(prev8, k, 0), pltpu.roll(x, k, 0)[0:SUBLANES])
        if C == SUBLANES:
            return top
        return rows(top, pltpu.roll(x, k, 0)[SUBLANES:])

    def psl(p):
        return slice(LANES * p, LANES * (p + 1))

    shared = {}

    def neumann(n_list, steps):
        rs = list(n_list)
        ps = [_mm(p, bd2(p)) for p in n_list]
        yield None
        for _ in range(steps - 1):
            both = [_mm(rows(p, r), bd2(p)) for p, r in zip(ps, rs)]
            yield None
            rs = [r + p + b[C:2 * C] for r, p, b in zip(rs, ps, both)]
            ps = [b[0:C] for b in both]
        upd = [_mm(r, bd2(p)) for p, r in zip(ps, rs)]
        yield None
        yield [r + p + u for r, p, u in zip(rs, ps, upd)]

    def rwkv(g):
        u_rw = u_rw_ref[g]
        u_prev = shifted(u_rw, xp_rw[g, 0:SUBLANES, :], 1)
        xp_rw[g, 0:SUBLANES, :] = u_rw[C - SUBLANES:C, :]
        xs = u_rw + (u_prev - u_rw) * mu_ref[...]
        r = xs[:, 0:GW]
        k = xs[:, GW:2 * GW]
        v = xs[:, 2 * GW:3 * GW]
        x7 = xs[:, 3 * GW:RWKV_COLS]
        w_pre = _mm(jnp.tanh(x7), lr_ref[LR_RW_W2])
        a_pre = _mm(x7, lr_ref[LR_RW_A2])
        gate = _mm(jax.nn.sigmoid(x7), lr_ref[LR_RW_G2])
        kk = k * vrow(V_RW_KK)
        kk_ss = hsum(kk * kk)
        yield
        log_w = -jax.nn.softplus(-(vrow(V_RW_W0) + w_pre)) - 0.5
        lw = -jnp.exp(log_w)
        a = jax.nn.sigmoid(vrow(V_RW_A0) + a_pre)
        kk = kk * lax.rsqrt(kk_ss + EPS)
        k2 = k * (1.0 + (a - 1.0) * vrow(V_RW_KA))
        bcum = _mm_sel_lhs(tri, lw)
        bonus = hsum(r * k2 * vrow(V_RW_RK))
        yield
        e_nb = jnp.exp(-bcum)
        blast = bcum[C - 1:C, :]
        e_rem = jnp.exp(blast - bcum)
        nka = -kk * a
        r_t = r * jnp.exp(bcum)
        a_t = kk * jnp.exp(bcum - lw)
        b_t = nka * e_nb
        k_t = k2 * e_nb
        k_rem = k2 * e_rem
        b_rem = nka * e_rem
        s = [s_wkv[g, p] for p in range(2)]
        ars = [_mm_nt(rows(a_t[:, psl(p)], r_t[:, psl(p)]), s[p]) for p in range(2)]
        ar = [_mm_nt(rows(a_t[:, psl(p)], r_t[:, psl(p)]), rows(hrows(b_t[:, psl(p)]), hrows(k_t[:, psl(p)])))
              for p in range(2)]
        yield
        n_ab = [jnp.where(strict2, ar[p][0:C, 0:2 * C], 0.0) for p in range(2)]
        n_ak = [jnp.where(strict2, ar[p][0:C, 2 * C:4 * C], 0.0) for p in range(2)]
        a_r = [jnp.where(incl4, ar[p][C:2 * C], 0.0) for p in range(2)]
        rhs = [ars[p][0:C] + pair_apply(n_ak[p], v[:, psl(p)]) for p in range(2)]
        tinv = None
        for step in neumann(n_ab, n_iter):
            if step is None:
                yield
            else:
                tinv = step
        yield
        uh = [rhs[p] + pair_apply(tinv[p], rhs[p]) for p in range(2)]
        yield
        y_pairs = []
        for p in range(2):
            sl = psl(p)
            vp = v[:, sl]
            u_p = uh[p]
            yp = ars[p][C:2 * C] + _mm(a_r[p], rows(hrows(u_p), hrows(vp)))
            s_new = s[p] * jnp.exp(blast[:, sl]) + _mm_tn(rows(vp, u_p), rows(k_rem[:, sl], b_rem[:, sl]))
            s_wkv[g, p] = jnp.where(bd, s_new, 0.0)
            y_pairs.append(yp)
        yield
        y = jnp.concatenate(y_pairs, axis=1)
        mean = hsum(y) * (1.0 / HD)
        yield
        yc = y - mean
        var = hsum(yc * yc) * (1.0 / HD)
        yield
        y = yc * lax.rsqrt(var + RWKV_GN_EPS) * vrow(V_RW_LNW) + vrow(V_RW_LNB)
        y_ref[g, :, 0:GW] = (y + bonus * v) * gate

    def gla(g):
        u_gla = u_gla_ref[g]
        q = u_gla[:, 0:GW] * (HD ** -0.5)
        k = u_gla[:, GW:2 * GW]
        v = u_gla[:, 2 * GW:3 * GW]
        gz = u_gla[:, 3 * GW:4 * GW]
        gate = _mm(misc_ref[g], lr_ref[LR_GLA_GK])
        yield
        la = jax.nn.log_sigmoid(gate + vrow(V_GLA_B)) * (1.0 / GLA_GATE_NORM)
        bcum = _mm_sel_lhs(tri, la)
        yield
        blast = bcum[C - 1:C, :]
        q_in = q * jnp.exp(bcum)
        k_out = k * jnp.exp(blast - bcum)
        st = [s_gla[g, p] for p in range(2)]
        qs = [_mm_nt(q_in[:, psl(p)], st[p]) for p in range(2)]
        rowc = lax.broadcasted_iota(jnp.int32, (C, 1), 0)
        a_blocks = [[], []]
        for lo in range(0, C, GLA_SUB):
            hi = min(lo + GLA_SUB, C)
            ref = bcum[lo - 1:lo, :] if lo else jnp.zeros((1, GW), F32)
            q_b = q[lo:hi] * jnp.exp(bcum[lo:hi] - ref)
            k_b = k * jnp.exp(jnp.where(rowc < hi, ref - bcum, 0.0))
            for p in range(2):
                a_blocks[p].append(pair_nt(q_b[:, psl(p)], k_b[:, psl(p)]))
        a_h = [rows(*a_blocks[p]) for p in range(2)]
        for p in range(2):
            sl = psl(p)
            st_new = st[p] * jnp.exp(blast[:, sl]) + _mm_tn(v[:, sl], k_out[:, sl])
            s_gla[g, p] = jnp.where(bd, st_new, 0.0)
        yield
        o_pairs = []
        for p in range(2):
            o_pairs.append(qs[p] + pair_apply(jnp.where(incl2, a_h[p], 0.0), v[:, psl(p)]))
        yield
        o = jnp.concatenate(o_pairs, axis=1)
        ss = hsum(o * o)
        yield
        o = o * lax.rsqrt(ss * (1.0 / HD) + EPS) * vrow(V_GLA_NW)
        y_ref[g, :, GW:2 * GW] = o * _silu(gz)

    def dn(g):
        misc = misc_ref[g]
        g_dn = -jnp.exp(v128_ref[P_DN_ALOG:P_DN_ALOG + 1, :]) * jax.nn.softplus(
            misc + v128_ref[P_DN_DTB:P_DN_DTB + 1, :])
        dt_ss = jax.nn.softplus(misc + v128_ref[P_SSM_DTB:P_SSM_DTB + 1, :])
        la_ss = dt_ss * -jnp.exp(v128_ref[P_SSM_ALOG:P_SSM_ALOG + 1, :])
        beta = jax.nn.sigmoid(misc)
        cum = _mm_sel_lhs(tri, jnp.where(lane < MISC_DN_B, g_dn, la_ss))
        u_dn = u_dn_ref[g]
        raw = u_dn[:, 0:CONV_COLS]
        prev8 = xp_dn[g, 0:SUBLANES, :]
        qkv = raw * conv_ref[CV_DN_W + 3:CV_DN_W + 4, :]
        for k in (1, 2, 3):
            qkv = qkv + shifted(raw, prev8, k) * conv_ref[CV_DN_W + 3 - k:CV_DN_W + 4 - k, :]
        xp_dn[g, 0:SUBLANES, :] = raw[C - SUBLANES:C, :]
        qkv = _silu(qkv)
        q = qkv[:, 0:GW]
        k = qkv[:, GW:2 * GW]
        v = qkv[:, 2 * GW:3 * GW]
        z = u_dn[:, CONV_COLS:CONV_COLS + GW]
        q_ss = hsum(q * q)
        k_ss = hsum(k * k)
        yield
        cum_t = cum.T
        shared[g] = (cum, cum_t, dt_ss)
        q = q * lax.rsqrt(q_ss + EPS) * (HD ** -0.5)
        k = k * lax.rsqrt(k_ss + EPS)
        beta_x = expand(beta, MISC_DN_B)
        cum_x = expand(cum, MISC_DN_A)
        s = [s_dn[g, p] for p in range(2)]
        ecx, kb, vb, kbe, qs, kq = [], [], [], [], [], []
        for p in range(2):
            sl = psl(p)
            ecx.append(jnp.exp(cum_x[p]))
            kb.append(k[:, sl] * beta_x[p])
            vb.append(v[:, sl] * beta_x[p])
            kbe.append(kb[p] * ecx[p])
            qs.append(_mm(q[:, sl] * ecx[p], s[p]))
            kq.append(pair_nt(rows(kb[p], q[:, sl]), k[:, sl]))
        yield
        n_m, a_l = [], []
        for p in range(2):
            dec = seg2(cum, cum_t, MISC_DN_A + 2 * p)
            n_m.append(jnp.where(strict2, -(kq[p][0:C] * dec), 0.0))
            a_l.append(kq[p][C:2 * C] * dec)
        tinv = None
        for step in neumann(n_m, n_iter - 1):
            if step is None:
                yield
            else:
                tinv = step
        yield
        err = [n_m[p] - tinv[p] + _mm3(n_m[p], bd2(tinv[p])) for p in range(2)]
        yield
        tinv = [tinv[p] + err[p] + _mm(tinv[p], bd2(err[p])) for p in range(2)]
        yield
        vk = [jnp.concatenate([vb[p], kbe[p]], axis=1) for p in range(2)]
        uw = [vk[p] + _mm(tinv[p], jnp.concatenate([hrows(vb[p]), hrows(kbe[p])], axis=1)) for p in range(2)]
        yield
        u_p = [uw[p][:, 0:LANES] for p in range(2)]
        w_p = [uw[p][:, LANES:2 * LANES] for p in range(2)]
        ws = [_mm(w_p[p], s[p]) for p in range(2)]
        yield
        o_pairs = []
        for p in range(2):
            sl = psl(p)
            cx = cum_x[p]
            last = cx[C - 1:C, :]
            v_new = u_p[p] - ws[p]
            op = qs[p] + pair_apply(a_l[p], v_new)
            s_new = s[p] * jnp.exp(last) + _mm_tn(k[:, sl] * jnp.exp(last - cx), v_new)
            s_dn[g, p] = jnp.where(bd, s_new, 0.0)
            o_pairs.append(op)
        yield
        o = jnp.concatenate(o_pairs, axis=1)
        ss = hsum(o * o)
        yield
        o = o * lax.rsqrt(ss * (1.0 / HD) + EPS) * vrow(V_DN_NW)
        y_ref[g, :, 2 * GW:3 * GW] = o * _silu(z)

    def ssd(g):
        u_ssm = u_ssm_ref[g]
        z = u_ssm[:, 0:GW]
        raw = u_ssm[:, GW:GW + CONV_COLS]
        prev8 = xp_ss[g, 0:SUBLANES, :]
        xbc = conv_ref[CV_SSM_B:CV_SSM_B + 1, :] + raw * conv_ref[CV_SSM_W + 3:CV_SSM_W + 4, :]
        for k in (1, 2, 3):
            xbc = xbc + shifted(raw, prev8, k) * conv_ref[CV_SSM_W + 3 - k:CV_SSM_W + 4 - k, :]
        xp_ss[g, 0:SUBLANES, :] = raw[C - SUBLANES:C, :]
        xbc = _silu(xbc)
        xs_ = xbc[:, 0:GW]
        bm = xbc[:, GW:2 * GW]
        cm = xbc[:, 2 * GW:3 * GW]
        s = [s_ssm[g, p] for p in range(2)]
        gmat = [_mm_nt(cm[:, psl(p)], rows(bm[:, psl(p)], bm[:, psl(p)])) for p in range(2)]
        cs = [_mm_nt(cm[:, psl(p)], s[p]) for p in range(2)]
        yield
        yield
        cum, cum_t, dt_ss = shared[g]
        dt_x = expand(dt_ss, MISC_SSM_DT)
        cum_x = expand(cum, MISC_SSM_DT)
        y_pairs = []
        for p in range(2):
            sl = psl(p)
            xh = xs_[:, sl]
            xdt = xh * dt_x[p]
            cx = cum_x[p]
            last = cx[C - 1:C, :]
            yp = jnp.exp(cx) * cs[p] + v256_ref[V_SSM_D:V_SSM_D + 1, sl] * xh
            yp = yp + pair_apply(gmat[p] * seg2(cum, cum_t, MISC_SSM_DT + 2 * p), xdt)
            lane0 = MISC_SSM_DT + 2 * p
            dcol = rows(jnp.broadcast_to(jnp.exp(cum[C - 1:C, lane0:lane0 + 1]), (HD, SSM_STATE)),
                        jnp.broadcast_to(jnp.exp(cum[C - 1:C, lane0 + 1:lane0 + 2]), (HD, SSM_STATE)))
            s_ssm[g, p] = s[p] * dcol + _mm_tn(xdt * jnp.exp(last - cx), bm[:, sl])
            y_pairs.append(yp)
        yield
        y = jnp.concatenate(y_pairs, axis=1) * _silu(z)
        ss = _mm_sel_rhs(y * y, blk128)
        yield
        y_ref[g, :, 3 * GW:4 * GW] = y * lax.rsqrt(ss * (1.0 / (2 * HD)) + EPS) * vrow(V_SSM_NW)

    active = []
    for g in range(G):
        active += [rwkv(g), dn(g), gla(g), ssd(g)]
    while active:
        alive = []
        for gen in active:
            try:
                next(gen)
                alive.append(gen)
            except StopIteration:
                pass
        active = alive

    @pl.when(t == n_t - 1)
    def _fin():
        for j in range(n_prev):
            for i in range(7):
                out_refs[i][j] = prev_refs[7 * j + i][...]
        for g in range(G):
            shift1_ref[g] = xp_rw[g, 0:SUBLANES, :]
            dnc1_ref[g] = xp_dn[g, 0:SUBLANES, :]
            ssc1_ref[g] = xp_ss[g, 0:SUBLANES, :]
            for p in range(2):
                sw = s_wkv[g, p]
                sg = s_gla[g, p].T
                sd = s_dn[g, p]
                ss = s_ssm[g, p]
                for h in range(2):
                    hs = slice(HD * h, HD * (h + 1))
                    wkv1_ref[g, 2 * p + h] = sw[hs, hs]
                    gla1_ref[g, 2 * p + h] = sg[hs, hs]
                    dn1_ref[g, 2 * p + h] = sd[hs, hs]
                    ssm1_ref[g, 2 * p + h] = ss[hs, :]


def _mixer_call(u_parts, states, mparams, layer, prev=()):
    u_rw, u_gla, u_dn, u_ssm, misc = u_parts
    shift0, wkv0, gla0, dnc0, dn0, ssc0, ssm0 = states
    bsz, tlen, _ = u_rw.shape
    C = math.gcd(tlen, PROMPT_CHUNK)
    n_t = tlen // C
    G = SEQS_PER_STEP_LONG if n_t > 1 else SEQS_PER_STEP_SHORT
    assert bsz % G == 0 and C & (C - 1) == 0

    def tmap(b, t):
        return (b, t, 0)

    def bmap3(b, t):
        return (b, 0, 0)

    def bmap4(b, t):
        return (b, 0, 0, 0)

    def full(arr):
        nd = arr.ndim
        return pl.BlockSpec(arr.shape, lambda b, t: (0,) * nd)

    def lmap3(b, t):
        return (layer, b, 0, 0)

    def lmap4(b, t):
        return (layer, b, 0, 0, 0)

    sq = (G, NH, HD, HD)
    in_state_specs = [pl.BlockSpec((None, G, SUBLANES, RWKV_COLS), lmap3), pl.BlockSpec((None,) + sq, lmap4),
                      pl.BlockSpec((None,) + sq, lmap4), pl.BlockSpec((None, G, SUBLANES, CONV_COLS), lmap3),
                      pl.BlockSpec((None,) + sq, lmap4), pl.BlockSpec((None, G, SUBLANES, CONV_COLS), lmap3),
                      pl.BlockSpec((None, G, NH, HD, SSM_STATE), lmap4)]
    n_prev = len(prev)
    n_out = n_prev + 1
    tails = [(SUBLANES, RWKV_COLS), (NH, HD, HD), (NH, HD, HD), (SUBLANES, CONV_COLS), (NH, HD, HD),
             (SUBLANES, CONV_COLS), (NH, HD, SSM_STATE)]
    prev_specs = [pl.BlockSpec((G,) + tl, bmap3 if len(tl) == 2 else bmap4) for tl in tails] * n_prev
    state_specs = [pl.BlockSpec((n_out, G) + tl, (lambda b, t: (0, b, 0, 0)) if len(tl) == 2
                                else (lambda b, t: (0, b, 0, 0, 0))) for tl in tails]
    state_shapes = [jax.ShapeDtypeStruct((n_out, bsz) + tl, F32) for tl in tails]
    outs = pl.pallas_call(
        functools.partial(_mixer_body, C, n_t, G, n_prev),
        grid=(bsz // G, n_t),
        in_specs=[pl.BlockSpec((G, C, RWKV_COLS), tmap), pl.BlockSpec((G, C, 4 * GW), tmap),
                  pl.BlockSpec((G, C, 4 * GW), tmap), pl.BlockSpec((G, C, 4 * GW), tmap),
                  pl.BlockSpec((G, C, LANES), tmap)] + in_state_specs + [full(a) for a in mparams] + prev_specs,
        out_specs=[pl.BlockSpec((G, C, D_MODEL), tmap)] + state_specs,
        out_shape=[jax.ShapeDtypeStruct((bsz, tlen, D_MODEL), F32)] + state_shapes,
        scratch_shapes=[pltpu.VMEM((G, SUBLANES, RWKV_COLS), F32),
                        pltpu.VMEM((G, SUBLANES, CONV_COLS), F32),
                        pltpu.VMEM((G, SUBLANES, CONV_COLS), F32),
                        pltpu.VMEM((G, 2, LANES, LANES), F32), pltpu.VMEM((G, 2, LANES, LANES), F32),
                        pltpu.VMEM((G, 2, LANES, LANES), F32), pltpu.VMEM((G, 2, LANES, SSM_STATE), F32)],
        compiler_params=pltpu.CompilerParams(vmem_limit_bytes=VMEM_LIMIT,
                                             dimension_semantics=("parallel", "arbitrary")),
        name="mixers",
    )(u_rw, u_gla, u_dn, u_ssm, misc, shift0, wkv0, gla0, dnc0, dn0, ssc0, ssm0, *mparams,
      *[a for layer_states in prev for a in layer_states])
    return outs[0], tuple(outs[1:])


def _pad_rows(m, lo, total):
    return jnp.pad(m, ((lo, total - lo - m.shape[0]), (0, 0)))


def _lane_vec(vals, lo):
    return jnp.pad(vals, (lo, LANES - lo - vals.shape[0]))


def _pack_layer(P, l):
    v256 = jnp.stack([P['rwkv_w0'][l], P['rwkv_a0'][l], P['rwkv_k_k'][l], P['rwkv_k_a'][l],
                      P['rwkv_r_k'][l], P['rwkv_ln_w'][l], P['rwkv_ln_b'][l],
                      P['gla_gk_b'][l], P['gla_norm_w'][l], P['dn_norm_w'][l],
                      jnp.repeat(P['ssm_D'][l], HD), P['ssm_norm_w'][l]])
    v256 = jnp.pad(v256, ((0, 16 - v256.shape[0]), (0, 0)))
    v128 = jnp.stack([_lane_vec(P['dn_A_log'][l], MISC_DN_A), _lane_vec(P['dn_dt_bias'][l], MISC_DN_A),
                      _lane_vec(P['ssm_A_log'][l], MISC_SSM_DT), _lane_vec(P['ssm_dt_bias'][l], MISC_SSM_DT)])
    v128 = jnp.pad(v128, ((0, 4), (0, 0)))
    conv = jnp.concatenate([P['dn_conv_w'][l], P['ssm_conv_w'][l], P['ssm_conv_b'][l][None],
                            jnp.zeros((7, CONV_COLS), F32)], axis=0)
    lr = jnp.stack([_pad_rows(P['rwkv_w2'][l], 0, LANES), _pad_rows(P['rwkv_a2'][l], 32, LANES),
                    _pad_rows(P['rwkv_g2'][l], 64, LANES),
                    _pad_rows(P['gla_gk_w2'][l], MISC_GLA_GATE, LANES)]).astype(MXU_DT)
    return (P['rwkv_mu'][l][None], v256, v128, conv, lr)


def _pad_tail_rows(a):
    return jnp.pad(a, ((0, 0), (0, 0), (SUBLANES - a.shape[2], 0), (0, 0)))


def _trunk(x, mod_rows, states, mixer_params, dense, final_norm_w):
    n_layers = len(mixer_params)
    shift0, wkv0, gla0, dnc0, dn0, ssc0, ssm0 = states
    st_in = (_pad_tail_rows(shift0[:, :, None, :]), wkv0, gla0, _pad_tail_rows(dnc0), dn0,
             _pad_tail_rows(ssc0), ssm0)
    prev = []
    for l in range(n_layers):
        mod = mod_rows[l][:, None, :]
        u_parts = _inproj_call(x, mod, dense['norm1'][l], dense['w_r'], l)
        last = l == n_layers - 1
        y_mix, st = _mixer_call(u_parts, st_in, mixer_params[l], l, prev=tuple(prev) if last else ())
        x = _outmlp_call(x, y_mix, mod, dense['norm2'][l], final_norm_w, dense['w_out'],
                         dense['w_up'], dense['w_down'], l, final=last)
        if not last:
            prev.append(tuple(s[0] for s in st))
    shift1, wkv1, gla1, dnc1, dn1, ssc1, ssm1 = st
    return x, (shift1[:, :, SUBLANES - 1], wkv1, gla1, dnc1[:, :, SUBLANES - 3:], dn1,
               ssc1[:, :, SUBLANES - 3:], ssm1)


def kernel(x_prompt, x_sample, state_rwkv_shift, state_rwkv_wkv, state_gla, state_dn_conv,
           state_dn, state_ssm_conv, state_ssm, c_prompt, c_sample,
           ada_w, ada_b, norm1_w, norm2_w, w_in, w_out, w_up, w_down,
           rwkv_mu, rwkv_w0, rwkv_w2, rwkv_a0, rwkv_a2, rwkv_g2, rwkv_k_k, rwkv_k_a, rwkv_r_k,
           rwkv_ln_w, rwkv_ln_b, gla_gk_w2, gla_gk_b, gla_norm_w,
           dn_conv_w, dn_A_log, dn_dt_bias, dn_norm_w,
           ssm_conv_w, ssm_conv_b, ssm_dt_bias, ssm_A_log, ssm_D, ssm_norm_w, final_norm_w):
    P = dict(rwkv_mu=rwkv_mu, rwkv_w0=rwkv_w0, rwkv_w2=rwkv_w2, rwkv_a0=rwkv_a0, rwkv_a2=rwkv_a2,
             rwkv_g2=rwkv_g2, rwkv_k_k=rwkv_k_k, rwkv_k_a=rwkv_k_a, rwkv_r_k=rwkv_r_k,
             rwkv_ln_w=rwkv_ln_w, rwkv_ln_b=rwkv_ln_b, gla_gk_w2=gla_gk_w2, gla_gk_b=gla_gk_b,
             gla_norm_w=gla_norm_w, dn_conv_w=dn_conv_w, dn_A_log=dn_A_log, dn_dt_bias=dn_dt_bias,
             dn_norm_w=dn_norm_w, ssm_conv_w=ssm_conv_w, ssm_conv_b=ssm_conv_b,
             ssm_dt_bias=ssm_dt_bias, ssm_A_log=ssm_A_log, ssm_D=ssm_D, ssm_norm_w=ssm_norm_w)
    n_layers = w_in.shape[0]
    n_prompt = x_prompt.shape[0]
    mixer_params = [_pack_layer(P, l) for l in range(n_layers)]
    dense = dict(w_r=_wprep_call(w_in), w_out=w_out.astype(MXU_DT), w_up=w_up.astype(MXU_DT),
                 w_down=w_down.astype(MXU_DT), norm1=norm1_w, norm2=norm2_w)
    mod_all = _ada_call(jnp.concatenate([c_prompt, c_sample], axis=0), ada_w, ada_b)
    sample_states = (state_rwkv_shift, state_rwkv_wkv, state_gla, state_dn_conv,
                     state_dn, state_ssm_conv, state_ssm)
    prompt_states = tuple(jnp.zeros((n_layers, n_prompt) + s.shape[2:], F32) for s in sample_states)
    y_prompt, ps = _trunk(x_prompt, mod_all[:, :n_prompt], prompt_states, mixer_params, dense, final_norm_w)
    y_sample, ss = _trunk(x_sample, mod_all[:, n_prompt:], sample_states, mixer_params, dense, final_norm_w)
    return (y_prompt, y_sample) + ps + ss
```

```python
import functools
import math

import jax
import jax.numpy as jnp
from jax import lax
from jax.experimental import pallas as pl
from jax.experimental.pallas import tpu as pltpu

F32 = jnp.float32
MXU_DT = jnp.bfloat16

D_MODEL = 1024
NH = 4
HD = 64
GW = NH * HD
D_FF = 4 * D_MODEL
SSM_STATE = 128
EPS = 1e-6
RWKV_GN_EPS = 64e-5
GLA_GATE_NORM = 16.0
GLA_SUB = 16
RWKV_COLS = 3 * GW + 32 + 32 + 64
CONV_COLS = 3 * GW
U_COLS = RWKV_COLS + 3 * 4 * GW + 128
MISC_GLA_GATE = 0
MISC_DN_A = 16
MISC_DN_B = 20
MISC_SSM_DT = 24

LANES = 128
SUBLANES = 8
VMEM_LIMIT = 56 * 1024 * 1024
PROMPT_CHUNK = 64
ROW_TILE = 512
SEQS_PER_STEP_LONG = 4
SEQS_PER_STEP_SHORT = 8


def _mm(a, b):
    return jnp.dot(a.astype(MXU_DT), b.astype(MXU_DT), preferred_element_type=F32)


def _mm_nt(a, b):
    return lax.dot_general(a.astype(MXU_DT), b.astype(MXU_DT), (((1,), (1,)), ((), ())),
                           preferred_element_type=F32)


def _mm_tn(a, b):
    return lax.dot_general(a.astype(MXU_DT), b.astype(MXU_DT), (((0,), (0,)), ((), ())),
                           preferred_element_type=F32)


def _mm3(a, b):
    a_hi, a_lo = _split(a, 2)
    b_hi, b_lo = _split(b, 2)
    return (jnp.dot(a_hi, b_hi, preferred_element_type=F32) + jnp.dot(a_hi, b_lo, preferred_element_type=F32)
            + jnp.dot(a_lo, b_hi, preferred_element_type=F32))


def _split(x, n):
    parts = []
    r = x
    for i in range(n):
        p = r.astype(MXU_DT)
        parts.append(p)
        if i + 1 < n:
            r = r - p.astype(F32)
    return parts


def _mm_sel_lhs(sel, x, n=2):
    acc = None
    for p in _split(x, n):
        d = jnp.dot(sel, p, preferred_element_type=F32)
        acc = d if acc is None else acc + d
    return acc


def _mm_sel_rhs(x, sel, n=2):
    acc = None
    for p in _split(x, n):
        d = jnp.dot(p, sel, preferred_element_type=F32)
        acc = d if acc is None else acc + d
    return acc


def _seg_decay(cum_col, cum_row, incl):
    d = cum_col - cum_row
    return jnp.where(incl, jnp.exp(jnp.where(incl, d, 0.0)), 0.0)


def _silu(x):
    return x * jax.nn.sigmoid(x)


def _ada_body(c_ref, w_ref, b_ref, o_ref):
    c = c_ref[...]
    o_ref[0] = _mm(_silu(c), w_ref[0]) + b_ref[0]


def _ada_call(c_all, ada_w, ada_b):
    n_layers = ada_w.shape[0]
    rows = c_all.shape[0]
    tn = 1536
    return pl.pallas_call(
        _ada_body,
        grid=(n_layers, 6 * D_MODEL // tn),
        in_specs=[pl.BlockSpec((rows, D_MODEL), lambda l, j: (0, 0)),
                  pl.BlockSpec((1, D_MODEL, tn), lambda l, j: (l, 0, j)),
                  pl.BlockSpec((1, 1, tn), lambda l, j: (l, 0, j))],
        out_specs=pl.BlockSpec((1, rows, tn), lambda l, j: (l, 0, j)),
        out_shape=jax.ShapeDtypeStruct((n_layers, rows, 6 * D_MODEL), F32),
        compiler_params=pltpu.CompilerParams(vmem_limit_bytes=VMEM_LIMIT),
        name="ada_mod",
    )(c_all, ada_w, ada_b.reshape(n_layers, 1, 6 * D_MODEL))


W_IN_COLS = RWKV_COLS + (4 * GW + 16) + (4 * GW + 8) + (4 * GW + 4)


def _wprep_body(w_ref, o_ref):
    w = w_ref[0]
    o_gla = RWKV_COLS
    o_dn = o_gla + 4 * GW + 16
    o_ssm = o_dn + 4 * GW + 8
    parts = [w[:, 0:o_gla + 4 * GW],
             w[:, o_dn:o_dn + 4 * GW],
             w[:, o_ssm:o_ssm + 4 * GW],
             w[:, o_gla + 4 * GW:o_dn],
             w[:, o_dn + 4 * GW:o_ssm],
             w[:, o_ssm + 4 * GW:W_IN_COLS],
             jnp.zeros((w.shape[0], LANES - 28), F32)]
    o_ref[0] = jnp.concatenate(parts, axis=1).astype(MXU_DT)


def _wprep_call(w_in):
    n_layers = w_in.shape[0]
    tr = 128
    return pl.pallas_call(
        _wprep_body,
        grid=(n_layers, D_MODEL // tr),
        in_specs=[pl.BlockSpec((1, tr, W_IN_COLS), lambda l, i: (l, i, 0))],
        out_specs=pl.BlockSpec((1, tr, U_COLS), lambda l, i: (l, i, 0)),
        out_shape=jax.ShapeDtypeStruct((n_layers, D_MODEL, U_COLS), MXU_DT),
        compiler_params=pltpu.CompilerParams(vmem_limit_bytes=VMEM_LIMIT,
                                             dimension_semantics=("parallel", "parallel")),
        name="w_in_relayout",
    )(w_in)


def _rms(x):
    return x * lax.rsqrt(jnp.mean(x * x, -1, keepdims=True) + EPS)


def _inproj_body(x_ref, sh_ref, sc_ref, nw_ref, w_ref, o_rw, o_gla, o_dn, o_ssm, o_misc):
    bb, tt, _ = x_ref.shape
    h = _rms(x_ref[...]) * nw_ref[...]
    h = h * (1.0 + sc_ref[...]) + sh_ref[...]
    u = jnp.dot(h.reshape(bb * tt, D_MODEL).astype(MXU_DT), w_ref[...], preferred_element_type=F32)
    off = 0
    for ref in (o_rw, o_gla, o_dn, o_ssm, o_misc):
        w = ref.shape[-1]
        ref[...] = u[:, off:off + w].reshape(bb, tt, w)
        off += w


def _row_blocks(bsz, tlen):
    tt = min(tlen, ROW_TILE)
    bb = ROW_TILE // tt
    assert tlen % tt == 0 and bsz % bb == 0
    return bb, tt


def _inproj_call(x, mod, norm_w, w_r, layer):
    bsz, tlen, _ = x.shape
    bb, tt = _row_blocks(bsz, tlen)
    widths = (RWKV_COLS, 4 * GW, 4 * GW, 4 * GW, LANES)

    def xmap(i, j):
        return (i, j, 0)

    return pl.pallas_call(
        _inproj_body,
        grid=(bsz // bb, tlen // tt),
        in_specs=[pl.BlockSpec((bb, tt, D_MODEL), xmap),
                  pl.BlockSpec((bb, 1, D_MODEL), lambda i, j: (i, 0, 0)),
                  pl.BlockSpec((bb, 1, D_MODEL), lambda i, j: (i, 0, 1)),
                  pl.BlockSpec((1, 1, D_MODEL), lambda i, j: (0, 0, 0)),
                  pl.BlockSpec((None, D_MODEL, U_COLS), lambda i, j: (layer, 0, 0),
                               pipeline_mode=pl.Buffered(1))],
        out_specs=[pl.BlockSpec((bb, tt, w), xmap) for w in widths],
        out_shape=[jax.ShapeDtypeStruct((bsz, tlen, w), F32) for w in widths],
        compiler_params=pltpu.CompilerParams(vmem_limit_bytes=VMEM_LIMIT,
                                             dimension_semantics=("parallel", "parallel")),
        name="in_proj",
    )(x, mod, mod, norm_w.reshape(1, 1, D_MODEL), w_r)


def _outmlp_body(final, x_ref, y_ref, gt1_ref, sh_ref, sc_ref, gt2_ref, nw_ref, fnw_ref,
                 wo_ref, wu_ref, wd_ref, o_ref):
    bb, tt, _ = x_ref.shape
    rows = bb * tt
    att = jnp.dot(y_ref[...].reshape(rows, D_MODEL).astype(MXU_DT), wo_ref[...],
                  preferred_element_type=F32)
    x1 = x_ref[...] + gt1_ref[...] * att.reshape(bb, tt, D_MODEL)
    h = _rms(x1) * nw_ref[...]
    h = h * (1.0 + sc_ref[...]) + sh_ref[...]
    a = jnp.dot(h.reshape(rows, D_MODEL).astype(MXU_DT), wu_ref[...], preferred_element_type=F32)
    a = jnp.square(jnp.maximum(a, 0.0))
    f = jnp.dot(a.astype(MXU_DT), wd_ref[...], preferred_element_type=F32)
    x2 = x1 + gt2_ref[...] * f.reshape(bb, tt, D_MODEL)
    if final:
        x2 = _rms(x2) * fnw_ref[...]
    o_ref[...] = x2


def _outmlp_call(x, y_mix, mod, norm_w, final_norm_w, w_out, w_up, w_down, layer, final):
    bsz, tlen, _ = x.shape
    bb, tt = _row_blocks(bsz, tlen)

    def xmap(i, j):
        return (i, j, 0)

    def modspec(k):
        return pl.BlockSpec((bb, 1, D_MODEL), lambda i, j: (i, 0, k))

    def wspec(shape):
        return pl.BlockSpec((None,) + shape, lambda i, j: (layer, 0, 0), pipeline_mode=pl.Buffered(1))

    vec = pl.BlockSpec((1, 1, D_MODEL), lambda i, j: (0, 0, 0))
    return pl.pallas_call(
        functools.partial(_outmlp_body, final),
        grid=(bsz // bb, tlen // tt),
        in_specs=[pl.BlockSpec((bb, tt, D_MODEL), xmap), pl.BlockSpec((bb, tt, D_MODEL), xmap),
                  modspec(2), modspec(3), modspec(4), modspec(5), vec, vec,
                  wspec((D_MODEL, D_MODEL)), wspec((D_MODEL, D_FF)), wspec((D_FF, D_MODEL))],
        out_specs=pl.BlockSpec((bb, tt, D_MODEL), xmap),
        out_shape=jax.ShapeDtypeStruct((bsz, tlen, D_MODEL), F32),
        compiler_params=pltpu.CompilerParams(vmem_limit_bytes=VMEM_LIMIT,
                                             dimension_semantics=("parallel", "parallel")),
        name="out_mlp",
    )(x, y_mix, mod, mod, mod, mod, norm_w.reshape(1, 1, D_MODEL),
      final_norm_w.reshape(1, 1, D_MODEL), w_out, w_up, w_down)


(V_RW_W0, V_RW_A0, V_RW_KK, V_RW_KA, V_RW_RK, V_RW_LNW, V_RW_LNB,
 V_GLA_B, V_GLA_NW, V_DN_NW, V_SSM_D, V_SSM_NW) = range(12)
P_DN_ALOG, P_DN_DTB, P_SSM_ALOG, P_SSM_DTB = range(4)
CV_DN_W, CV_SSM_W, CV_SSM_B = 0, 4, 8
LR_RW_W2, LR_RW_A2, LR_RW_G2, LR_GLA_GK = range(4)


def _blockdiag(a, b):
    za = jnp.zeros_like(a)
    return jnp.concatenate([jnp.concatenate([a, za], axis=1),
                            jnp.concatenate([za, b], axis=1)], axis=0)


def _mixer_body(C, n_t, G, n_prev, *refs):
    (u_rw_ref, u_gla_ref, u_dn_ref, u_ssm_ref, misc_ref,
     shift0_ref, wkv0_ref, gla0_ref, dnc0_ref, dn0_ref, ssc0_ref, ssm0_ref,
     mu_ref, v256_ref, v128_ref, conv_ref, lr_ref) = refs[:17]
    prev_refs = refs[17:17 + 7 * n_prev]
    y_ref = refs[17 + 7 * n_prev]
    out_refs = refs[18 + 7 * n_prev:25 + 7 * n_prev]
    xp_rw, xp_dn, xp_ss, s_wkv, s_gla, s_dn, s_ssm = refs[25 + 7 * n_prev:]
    shift1_ref, wkv1_ref, gla1_ref, dnc1_ref, dn1_ref, ssc1_ref, ssm1_ref = (r.at[n_prev] for r in out_refs)
    t = pl.program_id(1)

    @pl.when(t == 0)
    def _init():
        for g in range(G):
            xp_rw[g, 0:SUBLANES, :] = shift0_ref[g]
            xp_dn[g, 0:SUBLANES, :] = dnc0_ref[g]
            xp_ss[g, 0:SUBLANES, :] = ssc0_ref[g]
            for p in range(2):
                s_wkv[g, p] = _blockdiag(wkv0_ref[g, 2 * p], wkv0_ref[g, 2 * p + 1])
                s_gla[g, p] = _blockdiag(gla0_ref[g, 2 * p], gla0_ref[g, 2 * p + 1]).T
                s_dn[g, p] = _blockdiag(dn0_ref[g, 2 * p], dn0_ref[g, 2 * p + 1])
                s_ssm[g, p] = jnp.concatenate([ssm0_ref[g, 2 * p], ssm0_ref[g, 2 * p + 1]], axis=0)

    ri = lax.broadcasted_iota(jnp.int32, (C, C), 0)
    ci = lax.broadcasted_iota(jnp.int32, (C, C), 1)
    incl = ci <= ri
    ri2 = lax.broadcasted_iota(jnp.int32, (C, 2 * C), 0)
    ci2 = lax.broadcasted_iota(jnp.int32, (C, 2 * C), 1)
    left2 = ci2 < C
    cj2 = ci2 & (C - 1)
    incl2 = cj2 <= ri2
    strict2 = cj2 < ri2
    incl4 = ((lax.broadcasted_iota(jnp.int32, (C, 4 * C), 1) & (C - 1))
             <= lax.broadcasted_iota(jnp.int32, (C, 4 * C), 0))
    tri = jnp.where(incl, 1.0, 0.0).astype(MXU_DT)
    lane = lax.broadcasted_iota(jnp.int32, (1, LANES), 1)
    hmask = (lane < HD, lane >= HD)
    r128 = lax.shift_right_logical(lax.broadcasted_iota(jnp.int32, (LANES, LANES), 0), 6)
    c128 = lax.shift_right_logical(lax.broadcasted_iota(jnp.int32, (LANES, LANES), 1), 6)
    bd = r128 == c128
    r256 = lax.broadcasted_iota(jnp.int32, (GW, GW), 0)
    c256 = lax.broadcasted_iota(jnp.int32, (GW, GW), 1)
    blk64 = jnp.where(lax.shift_right_logical(r256, 6) == lax.shift_right_logical(c256, 6),
                      1.0, 0.0).astype(MXU_DT)
    blk128 = jnp.where(lax.shift_right_logical(r256, 7) == lax.shift_right_logical(c256, 7),
                       1.0, 0.0).astype(MXU_DT)
    n_iter = 0
    while 2 * (1 << n_iter) < C:
        n_iter += 1

    def rows(*xs):
        return jnp.concatenate(xs, axis=0)

    def vrow(i):
        return v256_ref[i:i + 1, :]

    def hsum(x):
        return _mm_sel_rhs(x, blk64, n=1)

    def head(x, h):
        return jnp.where(hmask[h], x, 0.0)

    def hrows(x):
        return rows(head(x, 0), head(x, 1))

    def pair_nt(a, b):
        return _mm_nt(a, hrows(b))

    def pair_apply(q2, y):
        return _mm(q2, hrows(y))

    def bd2(q2):
        return rows(jnp.where(left2, q2, 0.0), jnp.where(left2, 0.0, q2))

    def seg2(tile, tile_t, lane0):
        col = jnp.where(left2, tile[:, lane0:lane0 + 1], tile[:, lane0 + 1:lane0 + 2])
        row = jnp.concatenate([tile_t[lane0:lane0 + 1, :], tile_t[lane0 + 1:lane0 + 2, :]], axis=1)
        return _seg_decay(col, row, incl2)

    def expand(tile, base):
        cols = [jnp.broadcast_to(tile[:, base + h:base + h + 1], (C, LANES)) for h in range(NH)]
        return [jnp.where(hmask[0], cols[0], cols[1]), jnp.where(hmask[0], cols[2], cols[3])]

    row8 = lax.broadcasted_iota(jnp.int32, (SUBLANES, 1), 0)

    def shifted(x, prev8, k):
        top = jnp.where(row8 < k, pltpu.roll(prev8, k, 0), pltpu.roll(x, k, 0)[0:SUBLANES])
        if C == SUBLANES:
            return top
        return rows(top, pltpu.roll(x, k, 0)[SUBLANES:])

    def psl(p):
        return slice(LANES * p, LANES * (p + 1))

    shared = {}

    def neumann(n_list, steps):
        rs = list(n_list)
        ps = [_mm(p, bd2(p)) for p in n_list]
        yield None
        for _ in range(steps - 1):
            both = [_mm(rows(p, r), bd2(p)) for p, r in zip(ps, rs)]
            yield None
            rs = [r + p + b[C:2 * C] for r, p, b in zip(rs, ps, both)]
            ps = [b[0:C] for b in both]
        upd = [_mm(r, bd2(p)) for p, r in zip(ps, rs)]
        yield None
        yield [r + p + u for r, p, u in zip(rs, ps, upd)]

    def rwkv(g):
        u_rw = u_rw_ref[g]
        u_prev = shifted(u_rw, xp_rw[g, 0:SUBLANES, :], 1)
        xp_rw[g, 0:SUBLANES, :] = u_rw[C - SUBLANES:C, :]
        xs = u_rw + (u_prev - u_rw) * mu_ref[...]
        r = xs[:, 0:GW]
        k = xs[:, GW:2 * GW]
        v = xs[:, 2 * GW:3 * GW]
        x7 = xs[:, 3 * GW:RWKV_COLS]
        yield
        w_pre = _mm(jnp.tanh(x7), lr_ref[LR_RW_W2])
        a_pre = _mm(x7, lr_ref[LR_RW_A2])
        gate = _mm(jax.nn.sigmoid(x7), lr_ref[LR_RW_G2])
        yield
        kk = k * vrow(V_RW_KK)
        kk_ss = hsum(kk * kk)
        yield
        log_w = -jax.nn.softplus(-(vrow(V_RW_W0) + w_pre)) - 0.5
        lw = -jnp.exp(log_w)
        yield
        a = jax.nn.sigmoid(vrow(V_RW_A0) + a_pre)
        yield
        kk = kk * lax.rsqrt(kk_ss + EPS)
        k2 = k * (1.0 + (a - 1.0) * vrow(V_RW_KA))
        bcum = _mm_sel_lhs(tri, lw)
        bonus = hsum(r * k2 * vrow(V_RW_RK))
        yield
        e_nb = jnp.exp(-bcum)
        blast = bcum[C - 1:C, :]
        e_rem = jnp.exp(blast - bcum)
        yield
        nka = -kk * a
        r_t = r * jnp.exp(bcum)
        yield
        a_t = kk * jnp.exp(bcum - lw)
        b_t = nka * e_nb
        yield
        k_t = k2 * e_nb
        k_rem = k2 * e_rem
        b_rem = nka * e_rem
        yield
        s = [s_wkv[g, p] for p in range(2)]
        ars = [_mm_nt(rows(a_t[:, psl(p)], r_t[:, psl(p)]), s[p]) for p in range(2)]
        ar = [_mm_nt(rows(a_t[:, psl(p)], r_t[:, psl(p)]), rows(hrows(b_t[:, psl(p)]), hrows(k_t[:, psl(p)])))
              for p in range(2)]
        yield
        n_ab = [jnp.where(strict2, ar[p][0:C, 0:2 * C], 0.0) for p in range(2)]
        n_ak = [jnp.where(strict2, ar[p][0:C, 2 * C:4 * C], 0.0) for p in range(2)]
        a_r = [jnp.where(incl4, ar[p][C:2 * C], 0.0) for p in range(2)]
        rhs = [ars[p][0:C] + pair_apply(n_ak[p], v[:, psl(p)]) for p in range(2)]
        tinv = None
        for step in neumann(n_ab, n_iter):
            if step is None:
                yield
            else:
                tinv = step
        yield
        uh = [rhs[p] + pair_apply(tinv[p], rhs[p]) for p in range(2)]
        yield
        y_pairs = []
        for p in range(2):
            sl = psl(p)
            vp = v[:, sl]
            u_p = uh[p]
            yp = ars[p][C:2 * C] + _mm(a_r[p], rows(hrows(u_p), hrows(vp)))
            s_new = s[p] * jnp.exp(blast[:, sl]) + _mm_tn(rows(vp, u_p), rows(k_rem[:, sl], b_rem[:, sl]))
            s_wkv[g, p] = jnp.where(bd, s_new, 0.0)
            y_pairs.append(yp)
        yield
        y = jnp.concatenate(y_pairs, axis=1)
        mean = hsum(y) * (1.0 / HD)
        yield
        yc = y - mean
        var = hsum(yc * yc) * (1.0 / HD)
        yield
        y = yc * lax.rsqrt(var + RWKV_GN_EPS) * vrow(V_RW_LNW) + vrow(V_RW_LNB)
        y_ref[g, :, 0:GW] = (y + bonus * v) * gate

    def gla(g):
        u_gla = u_gla_ref[g]
        q = u_gla[:, 0:GW] * (HD ** -0.5)
        k = u_gla[:, GW:2 * GW]
        v = u_gla[:, 2 * GW:3 * GW]
        gz = u_gla[:, 3 * GW:4 * GW]
        gate = _mm(misc_ref[g], lr_ref[LR_GLA_GK])
        yield
        la = jax.nn.log_sigmoid(gate + vrow(V_GLA_B)) * (1.0 / GLA_GATE_NORM)
        bcum = _mm_sel_lhs(tri, la)
        yield
        blast = bcum[C - 1:C, :]
        q_in = q * jnp.exp(bcum)
        yield
        k_out = k * jnp.exp(blast - bcum)
        st = [s_gla[g, p] for p in range(2)]
        qs = [_mm_nt(q_in[:, psl(p)], st[p]) for p in range(2)]
        rowc = lax.broadcasted_iota(jnp.int32, (C, 1), 0)
        a_blocks = [[], []]
        for lo in range(0, C, GLA_SUB):
            hi = min(lo + GLA_SUB, C)
            ref = bcum[lo - 1:lo, :] if lo else jnp.zeros((1, GW), F32)
            q_b = q[lo:hi] * jnp.exp(bcum[lo:hi] - ref)
            k_b = k * jnp.exp(jnp.where(rowc < hi, ref - bcum, 0.0))
            for p in range(2):
                a_blocks[p].append(pair_nt(q_b[:, psl(p)], k_b[:, psl(p)]))
            yield
        a_h = [rows(*a_blocks[p]) for p in range(2)]
        for p in range(2):
            sl = psl(p)
            st_new = st[p] * jnp.exp(blast[:, sl]) + _mm_tn(v[:, sl], k_out[:, sl])
            s_gla[g, p] = jnp.where(bd, st_new, 0.0)
        yield
        o_pairs = []
        for p in range(2):
            o_pairs.append(qs[p] + pair_apply(jnp.where(incl2, a_h[p], 0.0), v[:, psl(p)]))
        yield
        o = jnp.concatenate(o_pairs, axis=1)
        ss = hsum(o * o)
        yield
        o = o * lax.rsqrt(ss * (1.0 / HD) + EPS) * vrow(V_GLA_NW)
        y_ref[g, :, GW:2 * GW] = o * _silu(gz)

    def dn(g):
        misc = misc_ref[g]
        g_dn = -jnp.exp(v128_ref[P_DN_ALOG:P_DN_ALOG + 1, :]) * jax.nn.softplus(
            misc + v128_ref[P_DN_DTB:P_DN_DTB + 1, :])
        dt_ss = jax.nn.softplus(misc + v128_ref[P_SSM_DTB:P_SSM_DTB + 1, :])
        la_ss = dt_ss * -jnp.exp(v128_ref[P_SSM_ALOG:P_SSM_ALOG + 1, :])
        beta = jax.nn.sigmoid(misc)
        cum = _mm_sel_lhs(tri, jnp.where(lane < MISC_DN_B, g_dn, la_ss))
        u_dn = u_dn_ref[g]
        raw = u_dn[:, 0:CONV_COLS]
        prev8 = xp_dn[g, 0:SUBLANES, :]
        qkv = raw * conv_ref[CV_DN_W + 3:CV_DN_W + 4, :]
        for k in (1, 2, 3):
            qkv = qkv + shifted(raw, prev8, k) * conv_ref[CV_DN_W + 3 - k:CV_DN_W + 4 - k, :]
        xp_dn[g, 0:SUBLANES, :] = raw[C - SUBLANES:C, :]
        yield
        qkv = _silu(qkv)
        q = qkv[:, 0:GW]
        k = qkv[:, GW:2 * GW]
        v = qkv[:, 2 * GW:3 * GW]
        z = u_dn[:, CONV_COLS:CONV_COLS + GW]
        q_ss = hsum(q * q)
        k_ss = hsum(k * k)
        yield
        cum_t = cum.T
        shared[g] = (cum, cum_t, dt_ss)
        q = q * lax.rsqrt(q_ss + EPS) * (HD ** -0.5)
        k = k * lax.rsqrt(k_ss + EPS)
        yield
        beta_x = expand(beta, MISC_DN_B)
        cum_x = expand(cum, MISC_DN_A)
        yield
        s = [s_dn[g, p] for p in range(2)]
        ecx, kb, vb, kbe, qs, kq = [], [], [], [], [], []
        for p in range(2):
            sl = psl(p)
            ecx.append(jnp.exp(cum_x[p]))
            kb.append(k[:, sl] * beta_x[p])
            vb.append(v[:, sl] * beta_x[p])
            kbe.append(kb[p] * ecx[p])
            yield
            qs.append(_mm(q[:, sl] * ecx[p], s[p]))
            kq.append(pair_nt(rows(kb[p], q[:, sl]), k[:, sl]))
        yield
        n_m, a_l = [], []
        for p in range(2):
            dec = seg2(cum, cum_t, MISC_DN_A + 2 * p)
            n_m.append(jnp.where(strict2, -(kq[p][0:C] * dec), 0.0))
            a_l.append(kq[p][C:2 * C] * dec)
        tinv = None
        for step in neumann(n_m, n_iter - 1):
            if step is None:
                yield
            else:
                tinv = step
        yield
        err = [n_m[p] - tinv[p] + _mm3(n_m[p], bd2(tinv[p])) for p in range(2)]
        yield
        tinv = [tinv[p] + err[p] + _mm(tinv[p], bd2(err[p])) for p in range(2)]
        yield
        vk = [jnp.concatenate([vb[p], kbe[p]], axis=1) for p in range(2)]
        uw = [vk[p] + _mm(tinv[p], jnp.concatenate([hrows(vb[p]), hrows(kbe[p])], axis=1)) for p in range(2)]
        yield
        u_p = [uw[p][:, 0:LANES] for p in range(2)]
        w_p = [uw[p][:, LANES:2 * LANES] for p in range(2)]
        ws = [_mm(w_p[p], s[p]) for p in range(2)]
        yield
        o_pairs = []
        for p in range(2):
            sl = psl(p)
            cx = cum_x[p]
            last = cx[C - 1:C, :]
            v_new = u_p[p] - ws[p]
            op = qs[p] + pair_apply(a_l[p], v_new)
            s_new = s[p] * jnp.exp(last) + _mm_tn(k[:, sl] * jnp.exp(last - cx), v_new)
            s_dn[g, p] = jnp.where(bd, s_new, 0.0)
            o_pairs.append(op)
        yield
        o = jnp.concatenate(o_pairs, axis=1)
        ss = hsum(o * o)
        yield
        o = o * lax.rsqrt(ss * (1.0 / HD) + EPS) * vrow(V_DN_NW)
        y_ref[g, :, 2 * GW:3 * GW] = o * _silu(z)

    def ssd(g):
        u_ssm = u_ssm_ref[g]
        z = u_ssm[:, 0:GW]
        raw = u_ssm[:, GW:GW + CONV_COLS]
        prev8 = xp_ss[g, 0:SUBLANES, :]
        xbc = conv_ref[CV_SSM_B:CV_SSM_B + 1, :] + raw * conv_ref[CV_SSM_W + 3:CV_SSM_W + 4, :]
        for k in (1, 2, 3):
            xbc = xbc + shifted(raw, prev8, k) * conv_ref[CV_SSM_W + 3 - k:CV_SSM_W + 4 - k, :]
        xp_ss[g, 0:SUBLANES, :] = raw[C - SUBLANES:C, :]
        yield
        xbc = _silu(xbc)
        yield
        xs_ = xbc[:, 0:GW]
        bm = xbc[:, GW:2 * GW]
        cm = xbc[:, 2 * GW:3 * GW]
        s = [s_ssm[g, p] for p in range(2)]
        gmat = [_mm_nt(cm[:, psl(p)], rows(bm[:, psl(p)], bm[:, psl(p)])) for p in range(2)]
        cs = [_mm_nt(cm[:, psl(p)], s[p]) for p in range(2)]
        yield
        while g not in shared:
            yield
        cum, cum_t, dt_ss = shared[g]
        dt_x = expand(dt_ss, MISC_SSM_DT)
        cum_x = expand(cum, MISC_SSM_DT)
        yield
        y_pairs = []
        for p in range(2):
            sl = psl(p)
            xh = xs_[:, sl]
            xdt = xh * dt_x[p]
            cx = cum_x[p]
            last = cx[C - 1:C, :]
            yp = jnp.exp(cx) * cs[p] + v256_ref[V_SSM_D:V_SSM_D + 1, sl] * xh
            yp = yp + pair_apply(gmat[p] * seg2(cum, cum_t, MISC_SSM_DT + 2 * p), xdt)
            lane0 = MISC_SSM_DT + 2 * p
            dcol = rows(jnp.broadcast_to(jnp.exp(cum[C - 1:C, lane0:lane0 + 1]), (HD, SSM_STATE)),
                        jnp.broadcast_to(jnp.exp(cum[C - 1:C, lane0 + 1:lane0 + 2]), (HD, SSM_STATE)))
            s_ssm[g, p] = s[p] * dcol + _mm_tn(xdt * jnp.exp(last - cx), bm[:, sl])
            y_pairs.append(yp)
            yield
        yield
        y = jnp.concatenate(y_pairs, axis=1) * _silu(z)
        ss = _mm_sel_rhs(y * y, blk128)
        yield
        y_ref[g, :, 3 * GW:4 * GW] = y * lax.rsqrt(ss * (1.0 / (2 * HD)) + EPS) * vrow(V_SSM_NW)

    active = []
    for g in range(G):
        active += [rwkv(g), dn(g), gla(g), ssd(g)]
    while active:
        alive = []
        for gen in active:
            try:
                next(gen)
                alive.append(gen)
            except StopIteration:
                pass
        active = alive

    @pl.when(t == n_t - 1)
    def _fin():
        for j in range(n_prev):
            for i in range(7):
                out_refs[i][j] = prev_refs[7 * j + i][...]
        for g in range(G):
            shift1_ref[g] = xp_rw[g, 0:SUBLANES, :]
            dnc1_ref[g] = xp_dn[g, 0:SUBLANES, :]
            ssc1_ref[g] = xp_ss[g, 0:SUBLANES, :]
            for p in range(2):
                sw = s_wkv[g, p]
                sg = s_gla[g, p].T
                sd = s_dn[g, p]
                ss = s_ssm[g, p]
                for h in range(2):
                    hs = slice(HD * h, HD * (h + 1))
                    wkv1_ref[g, 2 * p + h] = sw[hs, hs]
                    gla1_ref[g, 2 * p + h] = sg[hs, hs]
                    dn1_ref[g, 2 * p + h] = sd[hs, hs]
                    ssm1_ref[g, 2 * p + h] = ss[hs, :]


def _mixer_call(u_parts, states, mparams, layer, prev=()):
    u_rw, u_gla, u_dn, u_ssm, misc = u_parts
    shift0, wkv0, gla0, dnc0, dn0, ssc0, ssm0 = states
    bsz, tlen, _ = u_rw.shape
    C = math.gcd(tlen, PROMPT_CHUNK)
    n_t = tlen // C
    G = SEQS_PER_STEP_LONG if n_t > 1 else SEQS_PER_STEP_SHORT
    assert bsz % G == 0 and C & (C - 1) == 0

    def tmap(b, t):
        return (b, t, 0)

    def bmap3(b, t):
        return (b, 0, 0)

    def bmap4(b, t):
        return (b, 0, 0, 0)

    def full(arr):
        nd = arr.ndim
        return pl.BlockSpec(arr.shape, lambda b, t: (0,) * nd)

    def lmap3(b, t):
        return (layer, b, 0, 0)

    def lmap4(b, t):
        return (layer, b, 0, 0, 0)

    sq = (G, NH, HD, HD)
    in_state_specs = [pl.BlockSpec((None, G, SUBLANES, RWKV_COLS), lmap3), pl.BlockSpec((None,) + sq, lmap4),
                      pl.BlockSpec((None,) + sq, lmap4), pl.BlockSpec((None, G, SUBLANES, CONV_COLS), lmap3),
                      pl.BlockSpec((None,) + sq, lmap4), pl.BlockSpec((None, G, SUBLANES, CONV_COLS), lmap3),
                      pl.BlockSpec((None, G, NH, HD, SSM_STATE), lmap4)]
    n_prev = len(prev)
    n_out = n_prev + 1
    tails = [(SUBLANES, RWKV_COLS), (NH, HD, HD), (NH, HD, HD), (SUBLANES, CONV_COLS), (NH, HD, HD),
             (SUBLANES, CONV_COLS), (NH, HD, SSM_STATE)]
    prev_specs = [pl.BlockSpec((G,) + tl, bmap3 if len(tl) == 2 else bmap4) for tl in tails] * n_prev
    state_specs = [pl.BlockSpec((n_out, G) + tl, (lambda b, t: (0, b, 0, 0)) if len(tl) == 2
                                else (lambda b, t: (0, b, 0, 0, 0))) for tl in tails]
    state_shapes = [jax.ShapeDtypeStruct((n_out, bsz) + tl, F32) for tl in tails]
    outs = pl.pallas_call(
        functools.partial(_mixer_body, C, n_t, G, n_prev),
        grid=(bsz // G, n_t),
        in_specs=[pl.BlockSpec((G, C, RWKV_COLS), tmap), pl.BlockSpec((G, C, 4 * GW), tmap),
                  pl.BlockSpec((G, C, 4 * GW), tmap), pl.BlockSpec((G, C, 4 * GW), tmap),
                  pl.BlockSpec((G, C, LANES), tmap)] + in_state_specs + [full(a) for a in mparams] + prev_specs,
        out_specs=[pl.BlockSpec((G, C, D_MODEL), tmap)] + state_specs,
        out_shape=[jax.ShapeDtypeStruct((bsz, tlen, D_MODEL), F32)] + state_shapes,
        scratch_shapes=[pltpu.VMEM((G, SUBLANES, RWKV_COLS), F32),
                        pltpu.VMEM((G, SUBLANES, CONV_COLS), F32),
                        pltpu.VMEM((G, SUBLANES, CONV_COLS), F32),
                        pltpu.VMEM((G, 2, LANES, LANES), F32), pltpu.VMEM((G, 2, LANES, LANES), F32),
                        pltpu.VMEM((G, 2, LANES, LANES), F32), pltpu.VMEM((G, 2, LANES, SSM_STATE), F32)],
        compiler_params=pltpu.CompilerParams(vmem_limit_bytes=VMEM_LIMIT,
                                             dimension_semantics=("parallel", "arbitrary")),
        name="mixers",
    )(u_rw, u_gla, u_dn, u_ssm, misc, shift0, wkv0, gla0, dnc0, dn0, ssc0, ssm0, *mparams,
      *[a for layer_states in prev for a in layer_states])
    return outs[0], tuple(outs[1:])


def _pad_rows(m, lo, total):
    return jnp.pad(m, ((lo, total - lo - m.shape[0]), (0, 0)))


def _lane_vec(vals, lo):
    return jnp.pad(vals, (lo, LANES - lo - vals.shape[0]))


def _pack_layer(P, l):
    v256 = jnp.stack([P['rwkv_w0'][l], P['rwkv_a0'][l], P['rwkv_k_k'][l], P['rwkv_k_a'][l],
                      P['rwkv_r_k'][l], P['rwkv_ln_w'][l], P['rwkv_ln_b'][l],
                      P['gla_gk_b'][l], P['gla_norm_w'][l], P['dn_norm_w'][l],
                      jnp.repeat(P['ssm_D'][l], HD), P['ssm_norm_w'][l]])
    v256 = jnp.pad(v256, ((0, 16 - v256.shape[0]), (0, 0)))
    v128 = jnp.stack([_lane_vec(P['dn_A_log'][l], MISC_DN_A), _lane_vec(P['dn_dt_bias'][l], MISC_DN_A),
                      _lane_vec(P['ssm_A_log'][l], MISC_SSM_DT), _lane_vec(P['ssm_dt_bias'][l], MISC_SSM_DT)])
    v128 = jnp.pad(v128, ((0, 4), (0, 0)))
    conv = jnp.concatenate([P['dn_conv_w'][l], P['ssm_conv_w'][l], P['ssm_conv_b'][l][None],
                            jnp.zeros((7, CONV_COLS), F32)], axis=0)
    lr = jnp.stack([_pad_rows(P['rwkv_w2'][l], 0, LANES), _pad_rows(P['rwkv_a2'][l], 32, LANES),
                    _pad_rows(P['rwkv_g2'][l], 64, LANES),
                    _pad_rows(P['gla_gk_w2'][l], MISC_GLA_GATE, LANES)]).astype(MXU_DT)
    return (P['rwkv_mu'][l][None], v256, v128, conv, lr)


def _pad_tail_rows(a):
    return jnp.pad(a, ((0, 0), (0, 0), (SUBLANES - a.shape[2], 0), (0, 0)))


def _trunk(x, mod_rows, states, mixer_params, dense, final_norm_w):
    n_layers = len(mixer_params)
    shift0, wkv0, gla0, dnc0, dn0, ssc0, ssm0 = states
    st_in = (_pad_tail_rows(shift0[:, :, None, :]), wkv0, gla0, _pad_tail_rows(dnc0), dn0,
             _pad_tail_rows(ssc0), ssm0)
    prev = []
    for l in range(n_layers):
        mod = mod_rows[l][:, None, :]
        u_parts = _inproj_call(x, mod, dense['norm1'][l], dense['w_r'], l)
        last = l == n_layers - 1
        y_mix, st = _mixer_call(u_parts, st_in, mixer_params[l], l, prev=tuple(prev) if last else ())
        x = _outmlp_call(x, y_mix, mod, dense['norm2'][l], final_norm_w, dense['w_out'],
                         dense['w_up'], dense['w_down'], l, final=last)
        if not last:
            prev.append(tuple(s[0] for s in st))
    shift1, wkv1, gla1, dnc1, dn1, ssc1, ssm1 = st
    return x, (shift1[:, :, SUBLANES - 1], wkv1, gla1, dnc1[:, :, SUBLANES - 3:], dn1,
               ssc1[:, :, SUBLANES - 3:], ssm1)


def kernel(x_prompt, x_sample, state_rwkv_shift, state_rwkv_wkv, state_gla, state_dn_conv,
           state_dn, state_ssm_conv, state_ssm, c_prompt, c_sample,
           ada_w, ada_b, norm1_w, norm2_w, w_in, w_out, w_up, w_down,
           rwkv_mu, rwkv_w0, rwkv_w2, rwkv_a0, rwkv_a2, rwkv_g2, rwkv_k_k, rwkv_k_a, rwkv_r_k,
           rwkv_ln_w, rwkv_ln_b, gla_gk_w2, gla_gk_b, gla_norm_w,
           dn_conv_w, dn_A_log, dn_dt_bias, dn_norm_w,
           ssm_conv_w, ssm_conv_b, ssm_dt_bias, ssm_A_log, ssm_D, ssm_norm_w, final_norm_w):
    P = dict(rwkv_mu=rwkv_mu, rwkv_w0=rwkv_w0, rwkv_w2=rwkv_w2, rwkv_a0=rwkv_a0, rwkv_a2=rwkv_a2,
             rwkv_g2=rwkv_g2, rwkv_k_k=rwkv_k_k, rwkv_k_a=rwkv_k_a, rwkv_r_k=rwkv_r_k,
             rwkv_ln_w=rwkv_ln_w, rwkv_ln_b=rwkv_ln_b, gla_gk_w2=gla_gk_w2, gla_gk_b=gla_gk_b,
             gla_norm_w=gla_norm_w, dn_conv_w=dn_conv_w, dn_A_log=dn_A_log, dn_dt_bias=dn_dt_bias,
             dn_norm_w=dn_norm_w, ssm_conv_w=ssm_conv_w, ssm_conv_b=ssm_conv_b,
             ssm_dt_bias=ssm_dt_bias, ssm_A_log=ssm_A_log, ssm_D=ssm_D, ssm_norm_w=ssm_norm_w)
    n_layers = w_in.shape[0]
    n_prompt = x_prompt.shape[0]
    mixer_params = [_pack_layer(P, l) for l in range(n_layers)]
    dense = dict(w_r=_wprep_call(w_in), w_out=w_out.astype(MXU_DT), w_up=w_up.astype(MXU_DT),
                 w_down=w_down.astype(MXU_DT), norm1=norm1_w, norm2=norm2_w)
    mod_all = _ada_call(jnp.concatenate([c_prompt, c_sample], axis=0), ada_w, ada_b)
    sample_states = (state_rwkv_shift, state_rwkv_wkv, state_gla, state_dn_conv,
                     state_dn, state_ssm_conv, state_ssm)
    prompt_states = tuple(jnp.zeros((n_layers, n_prompt) + s.shape[2:], F32) for s in sample_states)
    y_prompt, ps = _trunk(x_prompt, mod_all[:, :n_prompt], prompt_states, mixer_params, dense, final_norm_w)
    y_sample, ss = _trunk(x_sample, mod_all[:, n_prompt:], sample_states, mixer_params, dense, final_norm_w)
    return (y_prompt, y_sample) + ps + ss
```

```python
import functools
import math

import jax
import jax.numpy as jnp
from jax import lax
from jax.experimental import pallas as pl
from jax.experimental.pallas import tpu as pltpu

F32 = jnp.float32
MXU_DT = jnp.bfloat16

D_MODEL = 1024
NH = 4
HD = 64
GW = NH * HD
D_FF = 4 * D_MODEL
SSM_STATE = 128
EPS = 1e-6
RWKV_GN_EPS = 64e-5
GLA_GATE_NORM = 16.0
GLA_SUB = 16
RWKV_COLS = 3 * GW + 32 + 32 + 64
CONV_COLS = 3 * GW
U_COLS = RWKV_COLS + 3 * 4 * GW + 128
MISC_GLA_GATE = 0
MISC_DN_A = 16
MISC_DN_B = 20
MISC_SSM_DT = 24

LANES = 128
SUBLANES = 8
VMEM_LIMIT = 56 * 1024 * 1024
PROMPT_CHUNK = 64
ROW_TILE = 512
SEQS_PER_STEP_LONG = 4
SEQS_PER_STEP_SHORT = 8


def _mm(a, b):
    return jnp.dot(a.astype(MXU_DT), b.astype(MXU_DT), preferred_element_type=F32)


def _mm_nt(a, b):
    return lax.dot_general(a.astype(MXU_DT), b.astype(MXU_DT), (((1,), (1,)), ((), ())),
                           preferred_element_type=F32)


def _mm_tn(a, b):
    return lax.dot_general(a.astype(MXU_DT), b.astype(MXU_DT), (((0,), (0,)), ((), ())),
                           preferred_element_type=F32)


def _mm3(a, b):
    a_hi, a_lo = _split(a, 2)
    b_hi, b_lo = _split(b, 2)
    return (jnp.dot(a_hi, b_hi, preferred_element_type=F32) + jnp.dot(a_hi, b_lo, preferred_element_type=F32)
            + jnp.dot(a_lo, b_hi, preferred_element_type=F32))


def _split(x, n):
    parts = []
    r = x
    for i in range(n):
        p = r.astype(MXU_DT)
        parts.append(p)
        if i + 1 < n:
            r = r - p.astype(F32)
    return parts


def _mm_sel_lhs(sel, x, n=2):
    acc = None
    for p in _split(x, n):
        d = jnp.dot(sel, p, preferred_element_type=F32)
        acc = d if acc is None else acc + d
    return acc


def _mm_sel_rhs(x, sel, n=2):
    acc = None
    for p in _split(x, n):
        d = jnp.dot(p, sel, preferred_element_type=F32)
        acc = d if acc is None else acc + d
    return acc


def _seg_decay(cum_col, cum_row, incl):
    d = cum_col - cum_row
    return jnp.where(incl, jnp.exp(jnp.where(incl, d, 0.0)), 0.0)


def _silu(x):
    return x * jax.nn.sigmoid(x)


def _ada_body(c_ref, w_ref, b_ref, o_ref):
    c = c_ref[...]
    o_ref[0] = _mm(_silu(c), w_ref[0]) + b_ref[0]


def _ada_call(c_all, ada_w, ada_b):
    n_layers = ada_w.shape[0]
    rows = c_all.shape[0]
    tn = 1536
    return pl.pallas_call(
        _ada_body,
        grid=(n_layers, 6 * D_MODEL // tn),
        in_specs=[pl.BlockSpec((rows, D_MODEL), lambda l, j: (0, 0)),
                  pl.BlockSpec((1, D_MODEL, tn), lambda l, j: (l, 0, j)),
                  pl.BlockSpec((1, 1, tn), lambda l, j: (l, 0, j))],
        out_specs=pl.BlockSpec((1, rows, tn), lambda l, j: (l, 0, j)),
        out_shape=jax.ShapeDtypeStruct((n_layers, rows, 6 * D_MODEL), F32),
        compiler_params=pltpu.CompilerParams(vmem_limit_bytes=VMEM_LIMIT),
        name="ada_mod",
    )(c_all, ada_w, ada_b.reshape(n_layers, 1, 6 * D_MODEL))


W_IN_COLS = RWKV_COLS + (4 * GW + 16) + (4 * GW + 8) + (4 * GW + 4)


def _wprep_body(w_ref, o_ref):
    w = w_ref[0]
    o_gla = RWKV_COLS
    o_dn = o_gla + 4 * GW + 16
    o_ssm = o_dn + 4 * GW + 8
    parts = [w[:, 0:o_gla + 4 * GW],
             w[:, o_dn:o_dn + 4 * GW],
             w[:, o_ssm:o_ssm + 4 * GW],
             w[:, o_gla + 4 * GW:o_dn],
             w[:, o_dn + 4 * GW:o_ssm],
             w[:, o_ssm + 4 * GW:W_IN_COLS],
             jnp.zeros((w.shape[0], LANES - 28), F32)]
    o_ref[0] = jnp.concatenate(parts, axis=1).astype(MXU_DT)


def _wprep_call(w_in):
    n_layers = w_in.shape[0]
    tr = 128
    return pl.pallas_call(
        _wprep_body,
        grid=(n_layers, D_MODEL // tr),
        in_specs=[pl.BlockSpec((1, tr, W_IN_COLS), lambda l, i: (l, i, 0))],
        out_specs=pl.BlockSpec((1, tr, U_COLS), lambda l, i: (l, i, 0)),
        out_shape=jax.ShapeDtypeStruct((n_layers, D_MODEL, U_COLS), MXU_DT),
        compiler_params=pltpu.CompilerParams(vmem_limit_bytes=VMEM_LIMIT,
                                             dimension_semantics=("parallel", "parallel")),
        name="w_in_relayout",
    )(w_in)


def _rms(x):
    return x * lax.rsqrt(jnp.mean(x * x, -1, keepdims=True) + EPS)


def _inproj_body(x_ref, sh_ref, sc_ref, nw_ref, w_ref, o_rw, o_gla, o_dn, o_ssm, o_misc):
    bb, tt, _ = x_ref.shape
    h = _rms(x_ref[...]) * nw_ref[...]
    h = h * (1.0 + sc_ref[...]) + sh_ref[...]
    u = jnp.dot(h.reshape(bb * tt, D_MODEL).astype(MXU_DT), w_ref[...], preferred_element_type=F32)
    off = 0
    for ref in (o_rw, o_gla, o_dn, o_ssm, o_misc):
        w = ref.shape[-1]
        ref[...] = u[:, off:off + w].reshape(bb, tt, w)
        off += w


def _row_blocks(bsz, tlen):
    tt = min(tlen, ROW_TILE)
    bb = ROW_TILE // tt
    assert tlen % tt == 0 and bsz % bb == 0
    return bb, tt


def _inproj_call(x, mod, norm_w, w_r, layer):
    bsz, tlen, _ = x.shape
    bb, tt = _row_blocks(bsz, tlen)
    widths = (RWKV_COLS, 4 * GW, 4 * GW, 4 * GW, LANES)

    def xmap(i, j):
        return (i, j, 0)

    return pl.pallas_call(
        _inproj_body,
        grid=(bsz // bb, tlen // tt),
        in_specs=[pl.BlockSpec((bb, tt, D_MODEL), xmap),
                  pl.BlockSpec((bb, 1, D_MODEL), lambda i, j: (i, 0, 0)),
                  pl.BlockSpec((bb, 1, D_MODEL), lambda i, j: (i, 0, 1)),
                  pl.BlockSpec((1, 1, D_MODEL), lambda i, j: (0, 0, 0)),
                  pl.BlockSpec((None, D_MODEL, U_COLS), lambda i, j: (layer, 0, 0),
                               pipeline_mode=pl.Buffered(1))],
        out_specs=[pl.BlockSpec((bb, tt, w), xmap) for w in widths],
        out_shape=[jax.ShapeDtypeStruct((bsz, tlen, w), F32) for w in widths],
        compiler_params=pltpu.CompilerParams(vmem_limit_bytes=VMEM_LIMIT,
                                             dimension_semantics=("parallel", "parallel")),
        name="in_proj",
    )(x, mod, mod, norm_w.reshape(1, 1, D_MODEL), w_r)


def _outmlp_body(final, x_ref, y_ref, gt1_ref, sh_ref, sc_ref, gt2_ref, nw_ref, fnw_ref,
                 wo_ref, wu_ref, wd_ref, o_ref):
    bb, tt, _ = x_ref.shape
    rows = bb * tt
    att = jnp.dot(y_ref[...].reshape(rows, D_MODEL).astype(MXU_DT), wo_ref[...],
                  preferred_element_type=F32)
    x1 = x_ref[...] + gt1_ref[...] * att.reshape(bb, tt, D_MODEL)
    h = _rms(x1) * nw_ref[...]
    h = h * (1.0 + sc_ref[...]) + sh_ref[...]
    a = jnp.dot(h.reshape(rows, D_MODEL).astype(MXU_DT), wu_ref[...], preferred_element_type=F32)
    a = jnp.square(jnp.maximum(a, 0.0))
    f = jnp.dot(a.astype(MXU_DT), wd_ref[...], preferred_element_type=F32)
    x2 = x1 + gt2_ref[...] * f.reshape(bb, tt, D_MODEL)
    if final:
        x2 = _rms(x2) * fnw_ref[...]
    o_ref[...] = x2


def _outmlp_call(x, y_mix, mod, norm_w, final_norm_w, w_out, w_up, w_down, layer, final):
    bsz, tlen, _ = x.shape
    bb, tt = _row_blocks(bsz, tlen)

    def xmap(i, j):
        return (i, j, 0)

    def modspec(k):
        return pl.BlockSpec((bb, 1, D_MODEL), lambda i, j: (i, 0, k))

    def wspec(shape):
        return pl.BlockSpec((None,) + shape, lambda i, j: (layer, 0, 0), pipeline_mode=pl.Buffered(1))

    vec = pl.BlockSpec((1, 1, D_MODEL), lambda i, j: (0, 0, 0))
    return pl.pallas_call(
        functools.partial(_outmlp_body, final),
        grid=(bsz // bb, tlen // tt),
        in_specs=[pl.BlockSpec((bb, tt, D_MODEL), xmap), pl.BlockSpec((bb, tt, D_MODEL), xmap),
                  modspec(2), modspec(3), modspec(4), modspec(5), vec, vec,
                  wspec((D_MODEL, D_MODEL)), wspec((D_MODEL, D_FF)), wspec((D_FF, D_MODEL))],
        out_specs=pl.BlockSpec((bb, tt, D_MODEL), xmap),
        out_shape=jax.ShapeDtypeStruct((bsz, tlen, D_MODEL), F32),
        compiler_params=pltpu.CompilerParams(vmem_limit_bytes=VMEM_LIMIT,
                                             dimension_semantics=("parallel", "parallel")),
        name="out_mlp",
    )(x, y_mix, mod, mod, mod, mod, norm_w.reshape(1, 1, D_MODEL),
      final_norm_w.reshape(1, 1, D_MODEL), w_out, w_up, w_down)


(V_RW_W0, V_RW_A0, V_RW_KK, V_RW_KA, V_RW_RK, V_RW_LNW, V_RW_LNB,
 V_GLA_B, V_GLA_NW, V_DN_NW, V_SSM_D, V_SSM_NW) = range(12)
P_DN_ALOG, P_DN_DTB, P_SSM_ALOG, P_SSM_DTB = range(4)
CV_DN_W, CV_SSM_W, CV_SSM_B = 0, 4, 8
LR_RW_W2, LR_RW_A2, LR_RW_G2, LR_GLA_GK = range(4)


def _blockdiag(a, b):
    za = jnp.zeros_like(a)
    return jnp.concatenate([jnp.concatenate([a, za], axis=1),
                            jnp.concatenate([za, b], axis=1)], axis=0)


def _mixer_body(C, n_t, G, NS, n_prev, *refs):
    (u_rw_ref, u_gla_ref, u_dn_ref, u_ssm_ref, misc_ref,
     shift0_ref, wkv0_ref, gla0_ref, dnc0_ref, dn0_ref, ssc0_ref, ssm0_ref,
     mu_ref, v256_ref, v128_ref, conv_ref, lr_ref) = refs[:17]
    prev_refs = refs[17:17 + 7 * n_prev]
    y_ref = refs[17 + 7 * n_prev]
    out_refs = refs[18 + 7 * n_prev:25 + 7 * n_prev]
    xp_rw, xp_dn, xp_ss, s_wkv, s_gla, s_dn, s_ssm = refs[25 + 7 * n_prev:]
    shift1_ref, wkv1_ref, gla1_ref, dnc1_ref, dn1_ref, ssc1_ref, ssm1_ref = (r.at[n_prev] for r in out_refs)
    t = pl.program_id(1)

    @pl.when(t == 0)
    def _init():
        for g in range(G * NS):
            xp_rw[g] = shift0_ref[g]
            xp_dn[g] = dnc0_ref[g]
            xp_ss[g] = ssc0_ref[g]
            for p in range(2):
                s_wkv[g, p] = _blockdiag(wkv0_ref[g, 2 * p], wkv0_ref[g, 2 * p + 1])
                s_gla[g, p] = _blockdiag(gla0_ref[g, 2 * p], gla0_ref[g, 2 * p + 1]).T
                s_dn[g, p] = _blockdiag(dn0_ref[g, 2 * p], dn0_ref[g, 2 * p + 1])
                s_ssm[g, p] = jnp.concatenate([ssm0_ref[g, 2 * p], ssm0_ref[g, 2 * p + 1]], axis=0)

    ri = lax.broadcasted_iota(jnp.int32, (C, C), 0)
    ci = lax.broadcasted_iota(jnp.int32, (C, C), 1)
    SEG = C // NS
    lg = SEG.bit_length() - 1

    def same_seg(a, b):
        return lax.shift_right_logical(a, lg) == lax.shift_right_logical(b, lg)

    incl = (ci <= ri) & same_seg(ci, ri)
    ri2 = lax.broadcasted_iota(jnp.int32, (C, 2 * C), 0)
    ci2 = lax.broadcasted_iota(jnp.int32, (C, 2 * C), 1)
    left2 = ci2 < C
    cj2 = ci2 & (C - 1)
    incl2 = (cj2 <= ri2) & same_seg(cj2, ri2)
    strict2 = (cj2 < ri2) & same_seg(cj2, ri2)
    cj4 = lax.broadcasted_iota(jnp.int32, (C, 4 * C), 1) & (C - 1)
    ri4 = lax.broadcasted_iota(jnp.int32, (C, 4 * C), 0)
    incl4 = (cj4 <= ri4) & same_seg(cj4, ri4)
    tri = jnp.where(incl, 1.0, 0.0).astype(MXU_DT)
    lane = lax.broadcasted_iota(jnp.int32, (1, LANES), 1)
    hmask = (lane < HD, lane >= HD)
    r128 = lax.shift_right_logical(lax.broadcasted_iota(jnp.int32, (LANES, LANES), 0), 6)
    c128 = lax.shift_right_logical(lax.broadcasted_iota(jnp.int32, (LANES, LANES), 1), 6)
    bd = r128 == c128
    r256 = lax.broadcasted_iota(jnp.int32, (GW, GW), 0)
    c256 = lax.broadcasted_iota(jnp.int32, (GW, GW), 1)
    blk64 = jnp.where(lax.shift_right_logical(r256, 6) == lax.shift_right_logical(c256, 6),
                      1.0, 0.0).astype(MXU_DT)
    blk128 = jnp.where(lax.shift_right_logical(r256, 7) == lax.shift_right_logical(c256, 7),
                       1.0, 0.0).astype(MXU_DT)
    n_iter = 0
    while 2 * (1 << n_iter) < SEG:
        n_iter += 1

    def rows(*xs):
        return jnp.concatenate(xs, axis=0)

    def vrow(i):
        return v256_ref[i:i + 1, :]

    def load(ref, g):
        blk = ref[g * NS:(g + 1) * NS]
        return blk.reshape(C, blk.shape[-1])

    def store(ref, g, lo, val):
        w = val.shape[-1]
        ref[g * NS:(g + 1) * NS, :, lo:lo + w] = val.reshape(NS, SEG, w)

    def sq(x, q):
        return x[q * SEG:(q + 1) * SEG]

    def seg_last(x):
        if NS == 1:
            return x[C - 1:C]
        x3 = x.reshape(NS, SEG, x.shape[-1])
        return jnp.broadcast_to(x3[:, SEG - 1:SEG], x3.shape).reshape(x.shape)

    def last_row(x, q):
        return x[(q + 1) * SEG - 1:(q + 1) * SEG]

    def stack(parts):
        return parts[0] if NS == 1 else rows(*parts)

    def with_state(mm, lhs_list, ref, g, p):
        outs = [[] for _ in lhs_list]
        for q in range(NS):
            res = mm(rows(*[sq(a, q) for a in lhs_list]) if len(lhs_list) > 1 else sq(lhs_list[0], q),
                     ref[g * NS + q, p])
            for i in range(len(lhs_list)):
                outs[i].append(res[i * SEG:(i + 1) * SEG])
        return [stack(o) for o in outs]

    def update_state(ref, g, p, decay, lhs_list, rhs_list, masked):
        for q in range(NS):
            a = rows(*[sq(x, q) for x in lhs_list]) if len(lhs_list) > 1 else sq(lhs_list[0], q)
            b = rows(*[sq(x, q) for x in rhs_list]) if len(rhs_list) > 1 else sq(rhs_list[0], q)
            new = ref[g * NS + q, p] * decay(q) + _mm_tn(a, b)
            ref[g * NS + q, p] = jnp.where(bd, new, 0.0) if masked else new

    def hsum(x):
        return _mm_sel_rhs(x, blk64, n=1)

    def head(x, h):
        return jnp.where(hmask[h], x, 0.0)

    def hrows(x):
        return rows(head(x, 0), head(x, 1))

    def pair_nt(a, b):
        return _mm_nt(a, hrows(b))

    def pair_apply(q2, y):
        return _mm(q2, hrows(y))

    def bd2(q2):
        return rows(jnp.where(left2, q2, 0.0), jnp.where(left2, 0.0, q2))

    def seg2(tile, tile_t, lane0):
        col = jnp.where(left2, tile[:, lane0:lane0 + 1], tile[:, lane0 + 1:lane0 + 2])
        row = jnp.concatenate([tile_t[lane0:lane0 + 1, :], tile_t[lane0 + 1:lane0 + 2, :]], axis=1)
        return _seg_decay(col, row, incl2)

    def expand(tile, base):
        cols = [jnp.broadcast_to(tile[:, base + h:base + h + 1], (C, LANES)) for h in range(NH)]
        return [jnp.where(hmask[0], cols[0], cols[1]), jnp.where(hmask[0], cols[2], cols[3])]

    row8 = lax.broadcasted_iota(jnp.int32, (SUBLANES, 1), 0)

    def shifted(x, prev8, k):
        if NS > 1:
            x3 = x.reshape(NS, SEG, x.shape[-1])
            r3 = lax.broadcasted_iota(jnp.int32, (1, SEG, 1), 1)
            return jnp.where(r3 < k, pltpu.roll(prev8, k, 1), pltpu.roll(x3, k, 1)).reshape(x.shape)
        prev8 = prev8[0]
        top = jnp.where(row8 < k, pltpu.roll(prev8, k, 0), pltpu.roll(x, k, 0)[0:SUBLANES])
        if C == SUBLANES:
            return top
        return rows(top, pltpu.roll(x, k, 0)[SUBLANES:])

    def psl(p):
        return slice(LANES * p, LANES * (p + 1))

    shared = {}

    def neumann(n_list, steps):
        rs = list(n_list)
        ps = [_mm(p, bd2(p)) for p in n_list]
        yield None
        for _ in range(steps - 1):
            both = [_mm(rows(p, r), bd2(p)) for p, r in zip(ps, rs)]
            yield None
            rs = [r + p + b[C:2 * C] for r, p, b in zip(rs, ps, both)]
            ps = [b[0:C] for b in both]
        upd = [_mm(r, bd2(p)) for p, r in zip(ps, rs)]
        yield None
        yield [r + p + u for r, p, u in zip(rs, ps, upd)]

    def rwkv(g):
        u_rw = load(u_rw_ref, g)
        u_prev = shifted(u_rw, xp_rw[g * NS:(g + 1) * NS], 1)
        xp_rw[g * NS:(g + 1) * NS] = u_rw.reshape(NS, SEG, RWKV_COLS)[:, SEG - SUBLANES:SEG]
        xs = u_rw + (u_prev - u_rw) * mu_ref[...]
        r = xs[:, 0:GW]
        k = xs[:, GW:2 * GW]
        v = xs[:, 2 * GW:3 * GW]
        x7 = xs[:, 3 * GW:RWKV_COLS]
        yield
        w_pre = _mm(jnp.tanh(x7), lr_ref[LR_RW_W2])
        a_pre = _mm(x7, lr_ref[LR_RW_A2])
        gate = _mm(jax.nn.sigmoid(x7), lr_ref[LR_RW_G2])
        yield
        kk = k * vrow(V_RW_KK)
        kk_ss = hsum(kk * kk)
        yield
        log_w = -jax.nn.softplus(-(vrow(V_RW_W0) + w_pre)) - 0.5
        lw = -jnp.exp(log_w)
        yield
        a = jax.nn.sigmoid(vrow(V_RW_A0) + a_pre)
        yield
        kk = kk * lax.rsqrt(kk_ss + EPS)
        k2 = k * (1.0 + (a - 1.0) * vrow(V_RW_KA))
        bcum = _mm_sel_lhs(tri, lw)
        bonus = hsum(r * k2 * vrow(V_RW_RK))
        yield
        e_nb = jnp.exp(-bcum)
        blast = seg_last(bcum)
        e_rem = jnp.exp(blast - bcum)
        yield
        nka = -kk * a
        r_t = r * jnp.exp(bcum)
        yield
        a_t = kk * jnp.exp(bcum - lw)
        b_t = nka * e_nb
        yield
        k_t = k2 * e_nb
        k_rem = k2 * e_rem
        b_rem = nka * e_rem
        yield
        ars = [with_state(_mm_nt, [a_t[:, psl(p)], r_t[:, psl(p)]], s_wkv, g, p) for p in range(2)]
        ar = [_mm_nt(rows(a_t[:, psl(p)], r_t[:, psl(p)]), rows(hrows(b_t[:, psl(p)]), hrows(k_t[:, psl(p)])))
              for p in range(2)]
        yield
        n_ab = [jnp.where(strict2, ar[p][0:C, 0:2 * C], 0.0) for p in range(2)]
        n_ak = [jnp.where(strict2, ar[p][0:C, 2 * C:4 * C], 0.0) for p in range(2)]
        a_r = [jnp.where(incl4, ar[p][C:2 * C], 0.0) for p in range(2)]
        rhs = [ars[p][0] + pair_apply(n_ak[p], v[:, psl(p)]) for p in range(2)]
        tinv = None
        for step in neumann(n_ab, n_iter):
            if step is None:
                yield
            else:
                tinv = step
        yield
        uh = [rhs[p] + pair_apply(tinv[p], rhs[p]) for p in range(2)]
        yield
        y_pairs = []
        for p in range(2):
            sl = psl(p)
            vp = v[:, sl]
            u_p = uh[p]
            yp = ars[p][1] + _mm(a_r[p], rows(hrows(u_p), hrows(vp)))
            update_state(s_wkv, g, p, lambda q: jnp.exp(last_row(bcum, q)[:, sl]),
                         [vp, u_p], [k_rem[:, sl], b_rem[:, sl]], True)
            y_pairs.append(yp)
        yield
        y = jnp.concatenate(y_pairs, axis=1)
        mean = hsum(y) * (1.0 / HD)
        yield
        yc = y - mean
        var = hsum(yc * yc) * (1.0 / HD)
        yield
        y = yc * lax.rsqrt(var + RWKV_GN_EPS) * vrow(V_RW_LNW) + vrow(V_RW_LNB)
        store(y_ref, g, 0, (y + bonus * v) * gate)

    def gla(g):
        u_gla = load(u_gla_ref, g)
        q = u_gla[:, 0:GW] * (HD ** -0.5)
        k = u_gla[:, GW:2 * GW]
        v = u_gla[:, 2 * GW:3 * GW]
        gz = u_gla[:, 3 * GW:4 * GW]
        gate = _mm(load(misc_ref, g), lr_ref[LR_GLA_GK])
        yield
        la = jax.nn.log_sigmoid(gate + vrow(V_GLA_B)) * (1.0 / GLA_GATE_NORM)
        bcum = _mm_sel_lhs(tri, la)
        yield
        blast = seg_last(bcum)
        q_in = q * jnp.exp(bcum)
        yield
        k_out = k * jnp.exp(blast - bcum)
        qs = [with_state(_mm_nt, [q_in[:, psl(p)]], s_gla, g, p)[0] for p in range(2)]
        a_blocks = [[], []]
        k_b = None
        ref_prev = None
        for lo in range(0, C, GLA_SUB):
            hi = min(lo + GLA_SUB, C)
            ref = bcum[lo - 1:lo, :] if lo else jnp.zeros((1, GW), F32)
            q_b = q[lo:hi] * jnp.exp(bcum[lo:hi] - ref)
            k_new = k[lo:hi] * jnp.exp(ref - bcum[lo:hi])
            if k_b is None:
                k_b = rows(k_new, jnp.zeros((C - hi, GW), F32)) if hi < C else k_new
            else:
                old = k_b[0:lo] * jnp.exp(ref - ref_prev)
                k_b = rows(old, k_new, jnp.zeros((C - hi, GW), F32)) if hi < C else rows(old, k_new)
            ref_prev = ref
            for p in range(2):
                a_blocks[p].append(pair_nt(q_b[:, psl(p)], k_b[:, psl(p)]))
            yield
        a_h = [rows(*a_blocks[p]) for p in range(2)]
        for p in range(2):
            sl = psl(p)
            update_state(s_gla, g, p, lambda q: jnp.exp(last_row(bcum, q)[:, sl]),
                         [v[:, sl]], [k_out[:, sl]], True)
        yield
        o_pairs = []
        for p in range(2):
            o_pairs.append(qs[p] + pair_apply(jnp.where(incl2, a_h[p], 0.0), v[:, psl(p)]))
        yield
        o = jnp.concatenate(o_pairs, axis=1)
        ss = hsum(o * o)
        yield
        o = o * lax.rsqrt(ss * (1.0 / HD) + EPS) * vrow(V_GLA_NW)
        store(y_ref, g, GW, o * _silu(gz))

    def dn(g):
        misc = load(misc_ref, g)
        g_dn = -jnp.exp(v128_ref[P_DN_ALOG:P_DN_ALOG + 1, :]) * jax.nn.softplus(
            misc + v128_ref[P_DN_DTB:P_DN_DTB + 1, :])
        dt_ss = jax.nn.softplus(misc + v128_ref[P_SSM_DTB:P_SSM_DTB + 1, :])
        la_ss = dt_ss * -jnp.exp(v128_ref[P_SSM_ALOG:P_SSM_ALOG + 1, :])
        beta = jax.nn.sigmoid(misc)
        cum = _mm_sel_lhs(tri, jnp.where(lane < MISC_DN_B, g_dn, la_ss))
        u_dn = load(u_dn_ref, g)
        raw = u_dn[:, 0:CONV_COLS]
        prev8 = xp_dn[g * NS:(g + 1) * NS]
        qkv = raw * conv_ref[CV_DN_W + 3:CV_DN_W + 4, :]
        for k in (1, 2, 3):
            qkv = qkv + shifted(raw, prev8, k) * conv_ref[CV_DN_W + 3 - k:CV_DN_W + 4 - k, :]
        xp_dn[g * NS:(g + 1) * NS] = raw.reshape(NS, SEG, CONV_COLS)[:, SEG - SUBLANES:SEG]
        yield
        qkv = _silu(qkv)
        q = qkv[:, 0:GW]
        k = qkv[:, GW:2 * GW]
        v = qkv[:, 2 * GW:3 * GW]
        z = u_dn[:, CONV_COLS:CONV_COLS + GW]
        q_ss = hsum(q * q)
        k_ss = hsum(k * k)
        yield
        cum_t = cum.T
        shared[g] = (cum, cum_t, dt_ss)
        q = q * lax.rsqrt(q_ss + EPS) * (HD ** -0.5)
        k = k * lax.rsqrt(k_ss + EPS)
        yield
        beta_x = expand(beta, MISC_DN_B)
        cum_x = expand(cum, MISC_DN_A)
        yield
        ecx, kb, vb, kbe, qs, kq = [], [], [], [], [], []
        for p in range(2):
            sl = psl(p)
            ecx.append(jnp.exp(cum_x[p]))
            kb.append(k[:, sl] * beta_x[p])
            vb.append(v[:, sl] * beta_x[p])
            kbe.append(kb[p] * ecx[p])
            yield
            qs.append(with_state(_mm, [q[:, sl] * ecx[p]], s_dn, g, p)[0])
            kq.append(pair_nt(rows(kb[p], q[:, sl]), k[:, sl]))
        yield
        n_m, a_l = [], []
        for p in range(2):
            dec = seg2(cum, cum_t, MISC_DN_A + 2 * p)
            n_m.append(jnp.where(strict2, -(kq[p][0:C] * dec), 0.0))
            a_l.append(kq[p][C:2 * C] * dec)
        tinv = None
        for step in neumann(n_m, n_iter - 1):
            if step is None:
                yield
            else:
                tinv = step
        yield
        err = [n_m[p] - tinv[p] + _mm3(n_m[p], bd2(tinv[p])) for p in range(2)]
        yield
        tinv = [tinv[p] + err[p] + _mm(tinv[p], bd2(err[p])) for p in range(2)]
        yield
        vk = [jnp.concatenate([vb[p], kbe[p]], axis=1) for p in range(2)]
        uw = [vk[p] + _mm(tinv[p], jnp.concatenate([hrows(vb[p]), hrows(kbe[p])], axis=1)) for p in range(2)]
        yield
        u_p = [uw[p][:, 0:LANES] for p in range(2)]
        w_p = [uw[p][:, LANES:2 * LANES] for p in range(2)]
        ws = [with_state(_mm, [w_p[p]], s_dn, g, p)[0] for p in range(2)]
        yield
        o_pairs = []
        for p in range(2):
            sl = psl(p)
            cx = cum_x[p]
            last = seg_last(cx)
            v_new = u_p[p] - ws[p]
            op = qs[p] + pair_apply(a_l[p], v_new)
            update_state(s_dn, g, p, lambda q: jnp.exp(last_row(cx, q)),
                         [k[:, sl] * jnp.exp(last - cx)], [v_new], True)
            o_pairs.append(op)
        yield
        o = jnp.concatenate(o_pairs, axis=1)
        ss = hsum(o * o)
        yield
        o = o * lax.rsqrt(ss * (1.0 / HD) + EPS) * vrow(V_DN_NW)
        store(y_ref, g, 2 * GW, o * _silu(z))

    def ssd(g):
        u_ssm = load(u_ssm_ref, g)
        z = u_ssm[:, 0:GW]
        raw = u_ssm[:, GW:GW + CONV_COLS]
        prev8 = xp_ss[g * NS:(g + 1) * NS]
        xbc = conv_ref[CV_SSM_B:CV_SSM_B + 1, :] + raw * conv_ref[CV_SSM_W + 3:CV_SSM_W + 4, :]
        for k in (1, 2, 3):
            xbc = xbc + shifted(raw, prev8, k) * conv_ref[CV_SSM_W + 3 - k:CV_SSM_W + 4 - k, :]
        xp_ss[g * NS:(g + 1) * NS] = raw.reshape(NS, SEG, CONV_COLS)[:, SEG - SUBLANES:SEG]
        yield
        xbc = _silu(xbc)
        yield
        xs_ = xbc[:, 0:GW]
        bm = xbc[:, GW:2 * GW]
        cm = xbc[:, 2 * GW:3 * GW]
        gmat = [_mm_nt(cm[:, psl(p)], rows(bm[:, psl(p)], bm[:, psl(p)])) for p in range(2)]
        cs = [with_state(_mm_nt, [cm[:, psl(p)]], s_ssm, g, p)[0] for p in range(2)]
        yield
        while g not in shared:
            yield
        cum, cum_t, dt_ss = shared[g]
        dt_x = expand(dt_ss, MISC_SSM_DT)
        cum_x = expand(cum, MISC_SSM_DT)
        yield
        y_pairs = []
        for p in range(2):
            sl = psl(p)
            xh = xs_[:, sl]
            xdt = xh * dt_x[p]
            cx = cum_x[p]
            last = seg_last(cx)
            yp = jnp.exp(cx) * cs[p] + v256_ref[V_SSM_D:V_SSM_D + 1, sl] * xh
            yp = yp + pair_apply(gmat[p] * seg2(cum, cum_t, MISC_SSM_DT + 2 * p), xdt)
            lane0 = MISC_SSM_DT + 2 * p

            def dcol(q, lane0=lane0):
                lr = last_row(cum, q)
                return rows(jnp.broadcast_to(jnp.exp(lr[:, lane0:lane0 + 1]), (HD, SSM_STATE)),
                            jnp.broadcast_to(jnp.exp(lr[:, lane0 + 1:lane0 + 2]), (HD, SSM_STATE)))

            update_state(s_ssm, g, p, dcol, [xdt * jnp.exp(last - cx)], [bm[:, sl]], False)
            y_pairs.append(yp)
            yield
        yield
        y = jnp.concatenate(y_pairs, axis=1) * _silu(z)
        ss = _mm_sel_rhs(y * y, blk128)
        yield
        store(y_ref, g, 3 * GW, y * lax.rsqrt(ss * (1.0 / (2 * HD)) + EPS) * vrow(V_SSM_NW))

    active = []
    for g in range(G):
        active += [rwkv(g), dn(g), gla(g), ssd(g)]
    while active:
        alive = []
        for gen in active:
            try:
                next(gen)
                alive.append(gen)
            except StopIteration:
                pass
        active = alive

    @pl.when(t == n_t - 1)
    def _fin():
        for j in range(n_prev):
            for i in range(7):
                out_refs[i][j] = prev_refs[7 * j + i][...]
        for g in range(G * NS):
            shift1_ref[g] = xp_rw[g]
            dnc1_ref[g] = xp_dn[g]
            ssc1_ref[g] = xp_ss[g]
            for p in range(2):
                sw = s_wkv[g, p]
                sg = s_gla[g, p].T
                sd = s_dn[g, p]
                ss = s_ssm[g, p]
                for h in range(2):
                    hs = slice(HD * h, HD * (h + 1))
                    wkv1_ref[g, 2 * p + h] = sw[hs, hs]
                    gla1_ref[g, 2 * p + h] = sg[hs, hs]
                    dn1_ref[g, 2 * p + h] = sd[hs, hs]
                    ssm1_ref[g, 2 * p + h] = ss[hs, :]


def _mixer_call(u_parts, states, mparams, layer, prev=()):
    u_rw, u_gla, u_dn, u_ssm, misc = u_parts
    shift0, wkv0, gla0, dnc0, dn0, ssc0, ssm0 = states
    bsz, tlen, _ = u_rw.shape
    tok = math.gcd(tlen, PROMPT_CHUNK)
    n_t = tlen // tok
    if n_t > 1:
        NS, R = 1, SEQS_PER_STEP_LONG
    else:
        NS = max(1, min(PROMPT_CHUNK // tok, SEQS_PER_STEP_SHORT))
        R = SEQS_PER_STEP_SHORT // NS
    C = NS * tok
    G = R * NS
    assert bsz % G == 0 and tok & (tok - 1) == 0 and tok % SUBLANES == 0

    def tmap(b, t):
        return (b, t, 0)

    def bmap3(b, t):
        return (b, 0, 0)

    def bmap4(b, t):
        return (b, 0, 0, 0)

    def full(arr):
        nd = arr.ndim
        return pl.BlockSpec(arr.shape, lambda b, t: (0,) * nd)

    def lmap3(b, t):
        return (layer, b, 0, 0)

    def lmap4(b, t):
        return (layer, b, 0, 0, 0)

    sq = (G, NH, HD, HD)
    once = dict(pipeline_mode=pl.Buffered(1)) if bsz == G else {}
    in_state_specs = [pl.BlockSpec((None, G, SUBLANES, RWKV_COLS), lmap3, **once),
                      pl.BlockSpec((None,) + sq, lmap4, **once), pl.BlockSpec((None,) + sq, lmap4, **once),
                      pl.BlockSpec((None, G, SUBLANES, CONV_COLS), lmap3, **once),
                      pl.BlockSpec((None,) + sq, lmap4, **once),
                      pl.BlockSpec((None, G, SUBLANES, CONV_COLS), lmap3, **once),
                      pl.BlockSpec((None, G, NH, HD, SSM_STATE), lmap4, **once)]
    n_prev = len(prev)
    n_out = n_prev + 1
    tails = [(SUBLANES, RWKV_COLS), (NH, HD, HD), (NH, HD, HD), (SUBLANES, CONV_COLS), (NH, HD, HD),
             (SUBLANES, CONV_COLS), (NH, HD, SSM_STATE)]
    prev_specs = [pl.BlockSpec((G,) + tl, bmap3 if len(tl) == 2 else bmap4, **once) for tl in tails] * n_prev
    state_specs = [pl.BlockSpec((n_out, G) + tl, (lambda b, t: (0, b, 0, 0)) if len(tl) == 2
                                else (lambda b, t: (0, b, 0, 0, 0))) for tl in tails]
    state_shapes = [jax.ShapeDtypeStruct((n_out, bsz) + tl, F32) for tl in tails]
    outs = pl.pallas_call(
        functools.partial(_mixer_body, C, n_t, R, NS, n_prev),
        grid=(bsz // G, n_t),
        in_specs=[pl.BlockSpec((G, tok, RWKV_COLS), tmap), pl.BlockSpec((G, tok, 4 * GW), tmap),
                  pl.BlockSpec((G, tok, 4 * GW), tmap), pl.BlockSpec((G, tok, 4 * GW), tmap),
                  pl.BlockSpec((G, tok, LANES), tmap)] + in_state_specs + [full(a) for a in mparams] + prev_specs,
        out_specs=[pl.BlockSpec((G, tok, D_MODEL), tmap)] + state_specs,
        out_shape=[jax.ShapeDtypeStruct((bsz, tlen, D_MODEL), F32)] + state_shapes,
        scratch_shapes=[pltpu.VMEM((G, SUBLANES, RWKV_COLS), F32),
                        pltpu.VMEM((G, SUBLANES, CONV_COLS), F32),
                        pltpu.VMEM((G, SUBLANES, CONV_COLS), F32),
                        pltpu.VMEM((G, 2, LANES, LANES), F32), pltpu.VMEM((G, 2, LANES, LANES), F32),
                        pltpu.VMEM((G, 2, LANES, LANES), F32), pltpu.VMEM((G, 2, LANES, SSM_STATE), F32)],
        compiler_params=pltpu.CompilerParams(vmem_limit_bytes=VMEM_LIMIT,
                                             dimension_semantics=("parallel", "arbitrary")),
        name="mixers",
    )(u_rw, u_gla, u_dn, u_ssm, misc, shift0, wkv0, gla0, dnc0, dn0, ssc0, ssm0, *mparams,
      *[a for layer_states in prev for a in layer_states])
    return outs[0], tuple(outs[1:])


def _pad_rows(m, lo, total):
    return jnp.pad(m, ((lo, total - lo - m.shape[0]), (0, 0)))


def _lane_vec(vals, lo):
    return jnp.pad(vals, (lo, LANES - lo - vals.shape[0]))


def _pack_layer(P, l):
    v256 = jnp.stack([P['rwkv_w0'][l], P['rwkv_a0'][l], P['rwkv_k_k'][l], P['rwkv_k_a'][l],
                      P['rwkv_r_k'][l], P['rwkv_ln_w'][l], P['rwkv_ln_b'][l],
                      P['gla_gk_b'][l], P['gla_norm_w'][l], P['dn_norm_w'][l],
                      jnp.repeat(P['ssm_D'][l], HD), P['ssm_norm_w'][l]])
    v256 = jnp.pad(v256, ((0, 16 - v256.shape[0]), (0, 0)))
    v128 = jnp.stack([_lane_vec(P['dn_A_log'][l], MISC_DN_A), _lane_vec(P['dn_dt_bias'][l], MISC_DN_A),
                      _lane_vec(P['ssm_A_log'][l], MISC_SSM_DT), _lane_vec(P['ssm_dt_bias'][l], MISC_SSM_DT)])
    v128 = jnp.pad(v128, ((0, 4), (0, 0)))
    conv = jnp.concatenate([P['dn_conv_w'][l], P['ssm_conv_w'][l], P['ssm_conv_b'][l][None],
                            jnp.zeros((7, CONV_COLS), F32)], axis=0)
    lr = jnp.stack([_pad_rows(P['rwkv_w2'][l], 0, LANES), _pad_rows(P['rwkv_a2'][l], 32, LANES),
                    _pad_rows(P['rwkv_g2'][l], 64, LANES),
                    _pad_rows(P['gla_gk_w2'][l], MISC_GLA_GATE, LANES)]).astype(MXU_DT)
    return (P['rwkv_mu'][l][None], v256, v128, conv, lr)


def _pad_tail_rows(a):
    return jnp.pad(a, ((0, 0), (0, 0), (SUBLANES - a.shape[2], 0), (0, 0)))


def _trunk(x, mod_rows, states, mixer_params, dense, final_norm_w):
    n_layers = len(mixer_params)
    shift0, wkv0, gla0, dnc0, dn0, ssc0, ssm0 = states
    st_in = (_pad_tail_rows(shift0[:, :, None, :]), wkv0, gla0, _pad_tail_rows(dnc0), dn0,
             _pad_tail_rows(ssc0), ssm0)
    prev = []
    for l in range(n_layers):
        mod = mod_rows[l][:, None, :]
        u_parts = _inproj_call(x, mod, dense['norm1'][l], dense['w_r'], l)
        last = l == n_layers - 1
        y_mix, st = _mixer_call(u_parts, st_in, mixer_params[l], l, prev=tuple(prev) if last else ())
        x = _outmlp_call(x, y_mix, mod, dense['norm2'][l], final_norm_w, dense['w_out'],
                         dense['w_up'], dense['w_down'], l, final=last)
        if not last:
            prev.append(tuple(s[0] for s in st))
    shift1, wkv1, gla1, dnc1, dn1, ssc1, ssm1 = st
    return x, (shift1[:, :, SUBLANES - 1], wkv1, gla1, dnc1[:, :, SUBLANES - 3:], dn1,
               ssc1[:, :, SUBLANES - 3:], ssm1)


def kernel(x_prompt, x_sample, state_rwkv_shift, state_rwkv_wkv, state_gla, state_dn_conv,
           state_dn, state_ssm_conv, state_ssm, c_prompt, c_sample,
           ada_w, ada_b, norm1_w, norm2_w, w_in, w_out, w_up, w_down,
           rwkv_mu, rwkv_w0, rwkv_w2, rwkv_a0, rwkv_a2, rwkv_g2, rwkv_k_k, rwkv_k_a, rwkv_r_k,
           rwkv_ln_w, rwkv_ln_b, gla_gk_w2, gla_gk_b, gla_norm_w,
           dn_conv_w, dn_A_log, dn_dt_bias, dn_norm_w,
           ssm_conv_w, ssm_conv_b, ssm_dt_bias, ssm_A_log, ssm_D, ssm_norm_w, final_norm_w):
    P = dict(rwkv_mu=rwkv_mu, rwkv_w0=rwkv_w0, rwkv_w2=rwkv_w2, rwkv_a0=rwkv_a0, rwkv_a2=rwkv_a2,
             rwkv_g2=rwkv_g2, rwkv_k_k=rwkv_k_k, rwkv_k_a=rwkv_k_a, rwkv_r_k=rwkv_r_k,
             rwkv_ln_w=rwkv_ln_w, rwkv_ln_b=rwkv_ln_b, gla_gk_w2=gla_gk_w2, gla_gk_b=gla_gk_b,
             gla_norm_w=gla_norm_w, dn_conv_w=dn_conv_w, dn_A_log=dn_A_log, dn_dt_bias=dn_dt_bias,
             dn_norm_w=dn_norm_w, ssm_conv_w=ssm_conv_w, ssm_conv_b=ssm_conv_b,
             ssm_dt_bias=ssm_dt_bias, ssm_A_log=ssm_A_log, ssm_D=ssm_D, ssm_norm_w=ssm_norm_w)
    n_layers = w_in.shape[0]
    n_prompt = x_prompt.shape[0]
    mixer_params = [_pack_layer(P, l) for l in range(n_layers)]
    dense = dict(w_r=_wprep_call(w_in), w_out=w_out.astype(MXU_DT), w_up=w_up.astype(MXU_DT),
                 w_down=w_down.astype(MXU_DT), norm1=norm1_w, norm2=norm2_w)
    mod_all = _ada_call(jnp.concatenate([c_prompt, c_sample], axis=0), ada_w, ada_b)
    sample_states = (state_rwkv_shift, state_rwkv_wkv, state_gla, state_dn_conv,
                     state_dn, state_ssm_conv, state_ssm)
    prompt_states = tuple(jnp.zeros((n_layers, n_prompt) + s.shape[2:], F32) for s in sample_states)
    y_prompt, ps = _trunk(x_prompt, mod_all[:, :n_prompt], prompt_states, mixer_params, dense, final_norm_w)
    y_sample, ss = _trunk(x_sample, mod_all[:, n_prompt:], sample_states, mixer_params, dense, final_norm_w)
    return (y_prompt, y_sample) + ps + ss
```

```python
import functools
import math

import jax
import jax.numpy as jnp
from jax import lax
from jax.experimental import pallas as pl
from jax.experimental.pallas import tpu as pltpu

F32 = jnp.float32
MXU_DT = jnp.bfloat16

D_MODEL = 1024
NH = 4
HD = 64
GW = NH * HD
D_FF = 4 * D_MODEL
SSM_STATE = 128
EPS = 1e-6
RWKV_GN_EPS = 64e-5
GLA_GATE_NORM = 16.0
GLA_SUB = 16
RWKV_COLS = 3 * GW + 32 + 32 + 64
CONV_COLS = 3 * GW
U_COLS = RWKV_COLS + 3 * 4 * GW + 128
MISC_GLA_GATE = 0
MISC_DN_A = 16
MISC_DN_B = 20
MISC_SSM_DT = 24

LANES = 128
SUBLANES = 8
VMEM_LIMIT = 56 * 1024 * 1024
PROMPT_CHUNK = 64
ROW_TILE = 512
SEQS_PER_STEP_LONG = 4
SEQS_PER_STEP_SHORT = 8


def _mm(a, b):
    return jnp.dot(a.astype(MXU_DT), b.astype(MXU_DT), preferred_element_type=F32)


def _mm_nt(a, b):
    return lax.dot_general(a.astype(MXU_DT), b.astype(MXU_DT), (((1,), (1,)), ((), ())),
                           preferred_element_type=F32)


def _mm_tn(a, b):
    return lax.dot_general(a.astype(MXU_DT), b.astype(MXU_DT), (((0,), (0,)), ((), ())),
                           preferred_element_type=F32)


def _mm3(a, b):
    a_hi, a_lo = _split(a, 2)
    b_hi, b_lo = _split(b, 2)
    return (jnp.dot(a_hi, b_hi, preferred_element_type=F32) + jnp.dot(a_hi, b_lo, preferred_element_type=F32)
            + jnp.dot(a_lo, b_hi, preferred_element_type=F32))


def _split(x, n):
    parts = []
    r = x
    for i in range(n):
        p = r.astype(MXU_DT)
        parts.append(p)
        if i + 1 < n:
            r = r - p.astype(F32)
    return parts


def _mm_sel_lhs(sel, x, n=2):
    acc = None
    for p in _split(x, n):
        d = jnp.dot(sel, p, preferred_element_type=F32)
        acc = d if acc is None else acc + d
    return acc


def _mm_sel_rhs(x, sel, n=2):
    acc = None
    for p in _split(x, n):
        d = jnp.dot(p, sel, preferred_element_type=F32)
        acc = d if acc is None else acc + d
    return acc


def _seg_decay(cum_col, cum_row, incl):
    d = cum_col - cum_row
    return jnp.where(incl, jnp.exp(jnp.where(incl, d, 0.0)), 0.0)


def _silu(x):
    return x * jax.nn.sigmoid(x)


def _ada_body(c_ref, w_ref, b_ref, o_ref):
    c = c_ref[...]
    o_ref[0] = _mm(_silu(c), w_ref[0]) + b_ref[0]


def _ada_call(c_all, ada_w, ada_b):
    n_layers = ada_w.shape[0]
    rows = c_all.shape[0]
    tn = 1536
    return pl.pallas_call(
        _ada_body,
        grid=(n_layers, 6 * D_MODEL // tn),
        in_specs=[pl.BlockSpec((rows, D_MODEL), lambda l, j: (0, 0)),
                  pl.BlockSpec((1, D_MODEL, tn), lambda l, j: (l, 0, j)),
                  pl.BlockSpec((1, 1, tn), lambda l, j: (l, 0, j))],
        out_specs=pl.BlockSpec((1, rows, tn), lambda l, j: (l, 0, j)),
        out_shape=jax.ShapeDtypeStruct((n_layers, rows, 6 * D_MODEL), F32),
        compiler_params=pltpu.CompilerParams(vmem_limit_bytes=VMEM_LIMIT),
        name="ada_mod",
    )(c_all, ada_w, ada_b.reshape(n_layers, 1, 6 * D_MODEL))


W_IN_COLS = RWKV_COLS + (4 * GW + 16) + (4 * GW + 8) + (4 * GW + 4)


def _wprep_body(w_ref, o_ref):
    w = w_ref[0]
    o_gla = RWKV_COLS
    o_dn = o_gla + 4 * GW + 16
    o_ssm = o_dn + 4 * GW + 8
    parts = [w[:, 0:o_gla + 4 * GW],
             w[:, o_dn:o_dn + 4 * GW],
             w[:, o_ssm:o_ssm + 4 * GW],
             w[:, o_gla + 4 * GW:o_dn],
             w[:, o_dn + 4 * GW:o_ssm],
             w[:, o_ssm + 4 * GW:W_IN_COLS],
             jnp.zeros((w.shape[0], LANES - 28), F32)]
    o_ref[0] = jnp.concatenate(parts, axis=1).astype(MXU_DT)


def _wprep_call(w_in):
    n_layers = w_in.shape[0]
    tr = 128
    return pl.pallas_call(
        _wprep_body,
        grid=(n_layers, D_MODEL // tr),
        in_specs=[pl.BlockSpec((1, tr, W_IN_COLS), lambda l, i: (l, i, 0))],
        out_specs=pl.BlockSpec((1, tr, U_COLS), lambda l, i: (l, i, 0)),
        out_shape=jax.ShapeDtypeStruct((n_layers, D_MODEL, U_COLS), MXU_DT),
        compiler_params=pltpu.CompilerParams(vmem_limit_bytes=VMEM_LIMIT,
                                             dimension_semantics=("parallel", "parallel")),
        name="w_in_relayout",
    )(w_in)


def _rms(x):
    return x * lax.rsqrt(jnp.mean(x * x, -1, keepdims=True) + EPS)


def _inproj_body(x_ref, sh_ref, sc_ref, nw_ref, w_ref, o_rw, o_gla, o_dn, o_ssm, o_misc):
    bb, tt, _ = x_ref.shape
    h = _rms(x_ref[...]) * nw_ref[...]
    h = h * (1.0 + sc_ref[...]) + sh_ref[...]
    u = jnp.dot(h.reshape(bb * tt, D_MODEL).astype(MXU_DT), w_ref[...], preferred_element_type=F32)
    off = 0
    for ref in (o_rw, o_gla, o_dn, o_ssm, o_misc):
        w = ref.shape[-1]
        ref[...] = u[:, off:off + w].reshape(bb, tt, w)
        off += w


def _row_blocks(bsz, tlen):
    tt = min(tlen, ROW_TILE)
    bb = ROW_TILE // tt
    assert tlen % tt == 0 and bsz % bb == 0
    return bb, tt


def _inproj_call(x, mod, norm_w, w_r, layer):
    bsz, tlen, _ = x.shape
    bb, tt = _row_blocks(bsz, tlen)
    widths = (RWKV_COLS, 4 * GW, 4 * GW, 4 * GW, LANES)

    def xmap(i, j):
        return (i, j, 0)

    return pl.pallas_call(
        _inproj_body,
        grid=(bsz // bb, tlen // tt),
        in_specs=[pl.BlockSpec((bb, tt, D_MODEL), xmap),
                  pl.BlockSpec((bb, 1, D_MODEL), lambda i, j: (i, 0, 0)),
                  pl.BlockSpec((bb, 1, D_MODEL), lambda i, j: (i, 0, 1)),
                  pl.BlockSpec((1, 1, D_MODEL), lambda i, j: (0, 0, 0)),
                  pl.BlockSpec((None, D_MODEL, U_COLS), lambda i, j: (layer, 0, 0),
                               pipeline_mode=pl.Buffered(1))],
        out_specs=[pl.BlockSpec((bb, tt, w), xmap) for w in widths],
        out_shape=[jax.ShapeDtypeStruct((bsz, tlen, w), F32) for w in widths],
        compiler_params=pltpu.CompilerParams(vmem_limit_bytes=VMEM_LIMIT,
                                             dimension_semantics=("parallel", "parallel")),
        name="in_proj",
    )(x, mod, mod, norm_w.reshape(1, 1, D_MODEL), w_r)


def _outmlp_body(final, x_ref, y_ref, gt1_ref, sh_ref, sc_ref, gt2_ref, nw_ref, fnw_ref,
                 wo_ref, wu_ref, wd_ref, o_ref):
    bb, tt, _ = x_ref.shape
    rows = bb * tt
    att = jnp.dot(y_ref[...].reshape(rows, D_MODEL).astype(MXU_DT), wo_ref[...],
                  preferred_element_type=F32)
    x1 = x_ref[...] + gt1_ref[...] * att.reshape(bb, tt, D_MODEL)
    h = _rms(x1) * nw_ref[...]
    h = h * (1.0 + sc_ref[...]) + sh_ref[...]
    a = jnp.dot(h.reshape(rows, D_MODEL).astype(MXU_DT), wu_ref[...], preferred_element_type=F32)
    a = jnp.square(jnp.maximum(a, 0.0))
    f = jnp.dot(a.astype(MXU_DT), wd_ref[...], preferred_element_type=F32)
    x2 = x1 + gt2_ref[...] * f.reshape(bb, tt, D_MODEL)
    if final:
        x2 = _rms(x2) * fnw_ref[...]
    o_ref[...] = x2


def _outmlp_call(x, y_mix, mod, norm_w, final_norm_w, w_out, w_up, w_down, layer, final):
    bsz, tlen, _ = x.shape
    bb, tt = _row_blocks(bsz, tlen)

    def xmap(i, j):
        return (i, j, 0)

    def modspec(k):
        return pl.BlockSpec((bb, 1, D_MODEL), lambda i, j: (i, 0, k))

    def wspec(shape):
        return pl.BlockSpec((None,) + shape, lambda i, j: (layer, 0, 0), pipeline_mode=pl.Buffered(1))

    vec = pl.BlockSpec((1, 1, D_MODEL), lambda i, j: (0, 0, 0))
    return pl.pallas_call(
        functools.partial(_outmlp_body, final),
        grid=(bsz // bb, tlen // tt),
        in_specs=[pl.BlockSpec((bb, tt, D_MODEL), xmap), pl.BlockSpec((bb, tt, D_MODEL), xmap),
                  modspec(2), modspec(3), modspec(4), modspec(5), vec, vec,
                  wspec((D_MODEL, D_MODEL)), wspec((D_MODEL, D_FF)), wspec((D_FF, D_MODEL))],
        out_specs=pl.BlockSpec((bb, tt, D_MODEL), xmap),
        out_shape=jax.ShapeDtypeStruct((bsz, tlen, D_MODEL), F32),
        compiler_params=pltpu.CompilerParams(vmem_limit_bytes=VMEM_LIMIT,
                                             dimension_semantics=("parallel", "parallel")),
        name="out_mlp",
    )(x, y_mix, mod, mod, mod, mod, norm_w.reshape(1, 1, D_MODEL),
      final_norm_w.reshape(1, 1, D_MODEL), w_out, w_up, w_down)


(V_RW_W0, V_RW_A0, V_RW_KK, V_RW_KA, V_RW_RK, V_RW_LNW, V_RW_LNB,
 V_GLA_B, V_GLA_NW, V_DN_NW, V_SSM_D, V_SSM_NW) = range(12)
P_DN_ALOG, P_DN_DTB, P_SSM_ALOG, P_SSM_DTB = range(4)
CV_DN_W, CV_SSM_W, CV_SSM_B = 0, 4, 8
LR_RW_W2, LR_RW_A2, LR_RW_G2, LR_GLA_GK = range(4)


def _blockdiag(a, b):
    za = jnp.zeros_like(a)
    return jnp.concatenate([jnp.concatenate([a, za], axis=1),
                            jnp.concatenate([za, b], axis=1)], axis=0)


def _mixer_body(C, n_t, G, NS, n_prev, *refs):
    (u_rw_ref, u_gla_ref, u_dn_ref, u_ssm_ref, misc_ref,
     shift0_ref, wkv0_ref, gla0_ref, dnc0_ref, dn0_ref, ssc0_ref, ssm0_ref,
     mu_ref, v256_ref, v128_ref, conv_ref, lr_ref) = refs[:17]
    prev_refs = refs[17:17 + 7 * n_prev]
    y_ref = refs[17 + 7 * n_prev]
    out_refs = refs[18 + 7 * n_prev:25 + 7 * n_prev]
    xp_rw, xp_dn, xp_ss, s_wkv, s_gla, s_dn, s_ssm = refs[25 + 7 * n_prev:]
    shift1_ref, wkv1_ref, gla1_ref, dnc1_ref, dn1_ref, ssc1_ref, ssm1_ref = (r.at[n_prev] for r in out_refs)
    t = pl.program_id(1)

    @pl.when(t == 0)
    def _init():
        for g in range(G * NS):
            xp_rw[g] = shift0_ref[g]
            xp_dn[g] = dnc0_ref[g]
            xp_ss[g] = ssc0_ref[g]
            for p in range(2):
                s_wkv[g, p] = _blockdiag(wkv0_ref[g, 2 * p], wkv0_ref[g, 2 * p + 1])
                s_gla[g, p] = _blockdiag(gla0_ref[g, 2 * p], gla0_ref[g, 2 * p + 1])
                s_dn[g, p] = _blockdiag(dn0_ref[g, 2 * p], dn0_ref[g, 2 * p + 1])
                s_ssm[g, p] = jnp.concatenate([ssm0_ref[g, 2 * p], ssm0_ref[g, 2 * p + 1]], axis=0)

    ri = lax.broadcasted_iota(jnp.int32, (C, C), 0)
    ci = lax.broadcasted_iota(jnp.int32, (C, C), 1)
    SEG = C // NS
    lg = SEG.bit_length() - 1

    def same_seg(a, b):
        return lax.shift_right_logical(a, lg) == lax.shift_right_logical(b, lg)

    incl = (ci <= ri) & same_seg(ci, ri)
    ri2 = lax.broadcasted_iota(jnp.int32, (C, 2 * C), 0)
    ci2 = lax.broadcasted_iota(jnp.int32, (C, 2 * C), 1)
    left2 = ci2 < C
    cj2 = ci2 & (C - 1)
    incl2 = (cj2 <= ri2) & same_seg(cj2, ri2)
    strict2 = (cj2 < ri2) & same_seg(cj2, ri2)
    cj4 = lax.broadcasted_iota(jnp.int32, (C, 4 * C), 1) & (C - 1)
    ri4 = lax.broadcasted_iota(jnp.int32, (C, 4 * C), 0)
    incl4 = (cj4 <= ri4) & same_seg(cj4, ri4)
    tri = jnp.where(incl, 1.0, 0.0).astype(MXU_DT)
    lane = lax.broadcasted_iota(jnp.int32, (1, LANES), 1)
    hmask = (lane < HD, lane >= HD)
    r128 = lax.shift_right_logical(lax.broadcasted_iota(jnp.int32, (LANES, LANES), 0), 6)
    c128 = lax.shift_right_logical(lax.broadcasted_iota(jnp.int32, (LANES, LANES), 1), 6)
    bd = r128 == c128
    r256 = lax.broadcasted_iota(jnp.int32, (GW, GW), 0)
    c256 = lax.broadcasted_iota(jnp.int32, (GW, GW), 1)
    blk64 = jnp.where(lax.shift_right_logical(r256, 6) == lax.shift_right_logical(c256, 6),
                      1.0, 0.0).astype(MXU_DT)
    blk128 = jnp.where(lax.shift_right_logical(r256, 7) == lax.shift_right_logical(c256, 7),
                       1.0, 0.0).astype(MXU_DT)
    n_iter = 0
    while 2 * (1 << n_iter) < SEG:
        n_iter += 1

    def rows(*xs):
        return jnp.concatenate(xs, axis=0)

    def vrow(i):
        return v256_ref[i:i + 1, :]

    def load(ref, g):
        blk = ref[g * NS:(g + 1) * NS]
        return blk.reshape(C, blk.shape[-1])

    def store(ref, g, lo, val):
        w = val.shape[-1]
        ref[g * NS:(g + 1) * NS, :, lo:lo + w] = val.reshape(NS, SEG, w)

    def sq(x, q):
        return x[q * SEG:(q + 1) * SEG]

    def seg_last(x):
        if NS == 1:
            return x[C - 1:C]
        x3 = x.reshape(NS, SEG, x.shape[-1])
        return jnp.broadcast_to(x3[:, SEG - 1:SEG], x3.shape).reshape(x.shape)

    def last_row(x, q):
        return x[(q + 1) * SEG - 1:(q + 1) * SEG]

    def stack(parts):
        return parts[0] if NS == 1 else rows(*parts)

    def with_state(mm, lhs_list, ref, g, p):
        outs = [[] for _ in lhs_list]
        for q in range(NS):
            res = mm(rows(*[sq(a, q) for a in lhs_list]) if len(lhs_list) > 1 else sq(lhs_list[0], q),
                     ref[g * NS + q, p])
            for i in range(len(lhs_list)):
                outs[i].append(res[i * SEG:(i + 1) * SEG])
        return [stack(o) for o in outs]

    def update_state(ref, g, p, decay, lhs_list, rhs_list, masked):
        for q in range(NS):
            a = rows(*[sq(x, q) for x in lhs_list]) if len(lhs_list) > 1 else sq(lhs_list[0], q)
            b = rows(*[sq(x, q) for x in rhs_list]) if len(rhs_list) > 1 else sq(rhs_list[0], q)
            new = ref[g * NS + q, p] * decay(q) + _mm_tn(a, b)
            ref[g * NS + q, p] = jnp.where(bd, new, 0.0) if masked else new

    def hsum(x):
        return _mm_sel_rhs(x, blk64, n=1)

    def head(x, h):
        return jnp.where(hmask[h], x, 0.0)

    def hrows(x):
        return rows(head(x, 0), head(x, 1))

    def pair_nt(a, b):
        return _mm_nt(a, hrows(b))

    def pair_apply(q2, y):
        return _mm(q2, hrows(y))

    def bd2(q2):
        return rows(jnp.where(left2, q2, 0.0), jnp.where(left2, 0.0, q2))

    def seg2(tile, tile_t, lane0):
        col = jnp.where(left2, tile[:, lane0:lane0 + 1], tile[:, lane0 + 1:lane0 + 2])
        row = jnp.concatenate([tile_t[lane0:lane0 + 1, :], tile_t[lane0 + 1:lane0 + 2, :]], axis=1)
        return _seg_decay(col, row, incl2)

    def expand(tile, base):
        cols = [jnp.broadcast_to(tile[:, base + h:base + h + 1], (C, LANES)) for h in range(NH)]
        return [jnp.where(hmask[0], cols[0], cols[1]), jnp.where(hmask[0], cols[2], cols[3])]

    row8 = lax.broadcasted_iota(jnp.int32, (SUBLANES, 1), 0)

    def shifted(x, prev8, k):
        if NS > 1:
            x3 = x.reshape(NS, SEG, x.shape[-1])
            r3 = lax.broadcasted_iota(jnp.int32, (1, SEG, 1), 1)
            return jnp.where(r3 < k, pltpu.roll(prev8, k, 1), pltpu.roll(x3, k, 1)).reshape(x.shape)
        prev8 = prev8[0]
        top = jnp.where(row8 < k, pltpu.roll(prev8, k, 0), pltpu.roll(x, k, 0)[0:SUBLANES])
        if C == SUBLANES:
            return top
        return rows(top, pltpu.roll(x, k, 0)[SUBLANES:])

    def psl(p):
        return slice(LANES * p, LANES * (p + 1))

    shared = {}

    def neumann(n_list, steps):
        rs = list(n_list)
        ps = [_mm(p, bd2(p)) for p in n_list]
        yield None
        for _ in range(steps - 1):
            both = [_mm(rows(p, r), bd2(p)) for p, r in zip(ps, rs)]
            yield None
            rs = [r + p + b[C:2 * C] for r, p, b in zip(rs, ps, both)]
            ps = [b[0:C] for b in both]
        upd = [_mm(r, bd2(p)) for p, r in zip(ps, rs)]
        yield None
        yield [r + p + u for r, p, u in zip(rs, ps, upd)]

    def rwkv(g):
        u_rw = load(u_rw_ref, g)
        u_prev = shifted(u_rw, xp_rw[g * NS:(g + 1) * NS], 1)
        xp_rw[g * NS:(g + 1) * NS] = u_rw.reshape(NS, SEG, RWKV_COLS)[:, SEG - SUBLANES:SEG]
        xs = u_rw + (u_prev - u_rw) * mu_ref[...]
        r = xs[:, 0:GW]
        k = xs[:, GW:2 * GW]
        v = xs[:, 2 * GW:3 * GW]
        x7 = xs[:, 3 * GW:RWKV_COLS]
        yield
        w_pre = _mm(jnp.tanh(x7), lr_ref[LR_RW_W2])
        a_pre = _mm(x7, lr_ref[LR_RW_A2])
        gate = _mm(jax.nn.sigmoid(x7), lr_ref[LR_RW_G2])
        yield
        kk = k * vrow(V_RW_KK)
        kk_ss = hsum(kk * kk)
        yield
        log_w = -jax.nn.softplus(-(vrow(V_RW_W0) + w_pre)) - 0.5
        lw = -jnp.exp(log_w)
        yield
        a = jax.nn.sigmoid(vrow(V_RW_A0) + a_pre)
        yield
        kk = kk * lax.rsqrt(kk_ss + EPS)
        k2 = k * (1.0 + (a - 1.0) * vrow(V_RW_KA))
        bcum = _mm_sel_lhs(tri, lw)
        bonus = hsum(r * k2 * vrow(V_RW_RK))
        yield
        e_nb = jnp.exp(-bcum)
        blast = seg_last(bcum)
        e_rem = jnp.exp(blast - bcum)
        yield
        nka = -kk * a
        r_t = r * jnp.exp(bcum)
        yield
        a_t = kk * jnp.exp(bcum - lw)
        b_t = nka * e_nb
        yield
        k_t = k2 * e_nb
        k_rem = k2 * e_rem
        b_rem = nka * e_rem
        yield
        ars = [with_state(_mm_nt, [a_t[:, psl(p)], r_t[:, psl(p)]], s_wkv, g, p) for p in range(2)]
        ar = [_mm_nt(rows(a_t[:, psl(p)], r_t[:, psl(p)]), rows(hrows(b_t[:, psl(p)]), hrows(k_t[:, psl(p)])))
              for p in range(2)]
        yield
        n_ab = [jnp.where(strict2, ar[p][0:C, 0:2 * C], 0.0) for p in range(2)]
        n_ak = [jnp.where(strict2, ar[p][0:C, 2 * C:4 * C], 0.0) for p in range(2)]
        a_r = [jnp.where(incl4, ar[p][C:2 * C], 0.0) for p in range(2)]
        rhs = [ars[p][0] + pair_apply(n_ak[p], v[:, psl(p)]) for p in range(2)]
        tinv = None
        for step in neumann(n_ab, n_iter):
            if step is None:
                yield
            else:
                tinv = step
        yield
        uh = [rhs[p] + pair_apply(tinv[p], rhs[p]) for p in range(2)]
        yield
        y_pairs = []
        for p in range(2):
            sl = psl(p)
            vp = v[:, sl]
            u_p = uh[p]
            yp = ars[p][1] + _mm(a_r[p], rows(hrows(u_p), hrows(vp)))
            update_state(s_wkv, g, p, lambda q: jnp.exp(last_row(bcum, q)[:, sl]),
                         [vp, u_p], [k_rem[:, sl], b_rem[:, sl]], True)
            y_pairs.append(yp)
        yield
        y = jnp.concatenate(y_pairs, axis=1)
        mean = hsum(y) * (1.0 / HD)
        yield
        yc = y - mean
        var = hsum(yc * yc) * (1.0 / HD)
        yield
        y = yc * lax.rsqrt(var + RWKV_GN_EPS) * vrow(V_RW_LNW) + vrow(V_RW_LNB)
        store(y_ref, g, 0, (y + bonus * v) * gate)

    def gla(g):
        u_gla = load(u_gla_ref, g)
        q = u_gla[:, 0:GW] * (HD ** -0.5)
        k = u_gla[:, GW:2 * GW]
        v = u_gla[:, 2 * GW:3 * GW]
        gz = u_gla[:, 3 * GW:4 * GW]
        gate = _mm(load(misc_ref, g), lr_ref[LR_GLA_GK])
        yield
        la = jax.nn.log_sigmoid(gate + vrow(V_GLA_B)) * (1.0 / GLA_GATE_NORM)
        bcum = _mm_sel_lhs(tri, la)
        yield
        blast = seg_last(bcum)
        q_in = q * jnp.exp(bcum)
        yield
        k_out = k * jnp.exp(blast - bcum)
        qs = [with_state(_mm, [q_in[:, psl(p)]], s_gla, g, p)[0] for p in range(2)]
        a_blocks = [[], []]
        k_b = None
        ref_prev = None
        for lo in range(0, C, GLA_SUB):
            hi = min(lo + GLA_SUB, C)
            ref = bcum[lo - 1:lo, :] if lo else jnp.zeros((1, GW), F32)
            q_b = q[lo:hi] * jnp.exp(bcum[lo:hi] - ref)
            k_new = k[lo:hi] * jnp.exp(ref - bcum[lo:hi])
            if k_b is None:
                k_b = rows(k_new, jnp.zeros((C - hi, GW), F32)) if hi < C else k_new
            else:
                old = k_b[0:lo] * jnp.exp(ref - ref_prev)
                k_b = rows(old, k_new, jnp.zeros((C - hi, GW), F32)) if hi < C else rows(old, k_new)
            ref_prev = ref
            for p in range(2):
                a_blocks[p].append(pair_nt(q_b[:, psl(p)], k_b[:, psl(p)]))
            yield
        a_h = [rows(*a_blocks[p]) for p in range(2)]
        for p in range(2):
            sl = psl(p)
            def kdecay(q, sl=sl):
                row = jnp.exp(last_row(bcum, q)[:, sl])
                return jnp.broadcast_to(row, (SUBLANES, LANES)).T[:, 0:1]

            update_state(s_gla, g, p, kdecay, [k_out[:, sl]], [v[:, sl]], True)
        yield
        o_pairs = []
        for p in range(2):
            o_pairs.append(qs[p] + pair_apply(jnp.where(incl2, a_h[p], 0.0), v[:, psl(p)]))
        yield
        o = jnp.concatenate(o_pairs, axis=1)
        ss = hsum(o * o)
        yield
        o = o * lax.rsqrt(ss * (1.0 / HD) + EPS) * vrow(V_GLA_NW)
        store(y_ref, g, GW, o * _silu(gz))

    def dn(g):
        misc = load(misc_ref, g)
        g_dn = -jnp.exp(v128_ref[P_DN_ALOG:P_DN_ALOG + 1, :]) * jax.nn.softplus(
            misc + v128_ref[P_DN_DTB:P_DN_DTB + 1, :])
        dt_ss = jax.nn.softplus(misc + v128_ref[P_SSM_DTB:P_SSM_DTB + 1, :])
        la_ss = dt_ss * -jnp.exp(v128_ref[P_SSM_ALOG:P_SSM_ALOG + 1, :])
        beta = jax.nn.sigmoid(misc)
        cum = _mm_sel_lhs(tri, jnp.where(lane < MISC_DN_B, g_dn, la_ss))
        u_dn = load(u_dn_ref, g)
        raw = u_dn[:, 0:CONV_COLS]
        prev8 = xp_dn[g * NS:(g + 1) * NS]
        qkv = raw * conv_ref[CV_DN_W + 3:CV_DN_W + 4, :]
        for k in (1, 2, 3):
            qkv = qkv + shifted(raw, prev8, k) * conv_ref[CV_DN_W + 3 - k:CV_DN_W + 4 - k, :]
        xp_dn[g * NS:(g + 1) * NS] = raw.reshape(NS, SEG, CONV_COLS)[:, SEG - SUBLANES:SEG]
        yield
        qkv = _silu(qkv)
        q = qkv[:, 0:GW]
        k = qkv[:, GW:2 * GW]
        v = qkv[:, 2 * GW:3 * GW]
        z = u_dn[:, CONV_COLS:CONV_COLS + GW]
        q_ss = hsum(q * q)
        k_ss = hsum(k * k)
        yield
        cum_t = cum.T
        shared[g] = (cum, cum_t, dt_ss)
        q = q * lax.rsqrt(q_ss + EPS) * (HD ** -0.5)
        k = k * lax.rsqrt(k_ss + EPS)
        yield
        beta_x = expand(beta, MISC_DN_B)
        cum_x = expand(cum, MISC_DN_A)
        yield
        ecx, kb, vb, kbe, qs, kq = [], [], [], [], [], []
        for p in range(2):
            sl = psl(p)
            ecx.append(jnp.exp(cum_x[p]))
            kb.append(k[:, sl] * beta_x[p])
            vb.append(v[:, sl] * beta_x[p])
            kbe.append(kb[p] * ecx[p])
            yield
            qs.append(with_state(_mm, [q[:, sl] * ecx[p]], s_dn, g, p)[0])
            kq.append(pair_nt(rows(kb[p], q[:, sl]), k[:, sl]))
        yield
        n_m, a_l = [], []
        for p in range(2):
            dec = seg2(cum, cum_t, MISC_DN_A + 2 * p)
            n_m.append(jnp.where(strict2, -(kq[p][0:C] * dec), 0.0))
            a_l.append(kq[p][C:2 * C] * dec)
        tinv = None
        for step in neumann(n_m, n_iter - 1):
            if step is None:
                yield
            else:
                tinv = step
        yield
        err = [n_m[p] - tinv[p] + _mm3(n_m[p], bd2(tinv[p])) for p in range(2)]
        yield
        tinv = [tinv[p] + err[p] + _mm(tinv[p], bd2(err[p])) for p in range(2)]
        yield
        vk = [jnp.concatenate([vb[p], kbe[p]], axis=1) for p in range(2)]
        uw = [vk[p] + _mm(tinv[p], jnp.concatenate([hrows(vb[p]), hrows(kbe[p])], axis=1)) for p in range(2)]
        yield
        u_p = [uw[p][:, 0:LANES] for p in range(2)]
        w_p = [uw[p][:, LANES:2 * LANES] for p in range(2)]
        ws = [with_state(_mm, [w_p[p]], s_dn, g, p)[0] for p in range(2)]
        yield
        o_pairs = []
        for p in range(2):
            sl = psl(p)
            cx = cum_x[p]
            last = seg_last(cx)
            v_new = u_p[p] - ws[p]
            op = qs[p] + pair_apply(a_l[p], v_new)
            update_state(s_dn, g, p, lambda q: jnp.exp(last_row(cx, q)),
                         [k[:, sl] * jnp.exp(last - cx)], [v_new], True)
            o_pairs.append(op)
        yield
        o = jnp.concatenate(o_pairs, axis=1)
        ss = hsum(o * o)
        yield
        o = o * lax.rsqrt(ss * (1.0 / HD) + EPS) * vrow(V_DN_NW)
        store(y_ref, g, 2 * GW, o * _silu(z))

    def ssd(g):
        u_ssm = load(u_ssm_ref, g)
        z = u_ssm[:, 0:GW]
        raw = u_ssm[:, GW:GW + CONV_COLS]
        prev8 = xp_ss[g * NS:(g + 1) * NS]
        xbc = conv_ref[CV_SSM_B:CV_SSM_B + 1, :] + raw * conv_ref[CV_SSM_W + 3:CV_SSM_W + 4, :]
        for k in (1, 2, 3):
            xbc = xbc + shifted(raw, prev8, k) * conv_ref[CV_SSM_W + 3 - k:CV_SSM_W + 4 - k, :]
        xp_ss[g * NS:(g + 1) * NS] = raw.reshape(NS, SEG, CONV_COLS)[:, SEG - SUBLANES:SEG]
        yield
        xbc = _silu(xbc)
        yield
        xs_ = xbc[:, 0:GW]
        bm = xbc[:, GW:2 * GW]
        cm = xbc[:, 2 * GW:3 * GW]
        gmat = [_mm_nt(cm[:, psl(p)], rows(bm[:, psl(p)], bm[:, psl(p)])) for p in range(2)]
        cs = [with_state(_mm_nt, [cm[:, psl(p)]], s_ssm, g, p)[0] for p in range(2)]
        yield
        while g not in shared:
            yield
        cum, cum_t, dt_ss = shared[g]
        dt_x = expand(dt_ss, MISC_SSM_DT)
        cum_x = expand(cum, MISC_SSM_DT)
        yield
        y_pairs = []
        for p in range(2):
            sl = psl(p)
            xh = xs_[:, sl]
            xdt = xh * dt_x[p]
            cx = cum_x[p]
            last = seg_last(cx)
            yp = jnp.exp(cx) * cs[p] + v256_ref[V_SSM_D:V_SSM_D + 1, sl] * xh
            yp = yp + pair_apply(gmat[p] * seg2(cum, cum_t, MISC_SSM_DT + 2 * p), xdt)
            lane0 = MISC_SSM_DT + 2 * p

            def dcol(q, lane0=lane0):
                lr = last_row(cum, q)
                return rows(jnp.broadcast_to(jnp.exp(lr[:, lane0:lane0 + 1]), (HD, SSM_STATE)),
                            jnp.broadcast_to(jnp.exp(lr[:, lane0 + 1:lane0 + 2]), (HD, SSM_STATE)))

            update_state(s_ssm, g, p, dcol, [xdt * jnp.exp(last - cx)], [bm[:, sl]], False)
            y_pairs.append(yp)
            yield
        yield
        y = jnp.concatenate(y_pairs, axis=1) * _silu(z)
        ss = _mm_sel_rhs(y * y, blk128)
        yield
        store(y_ref, g, 3 * GW, y * lax.rsqrt(ss * (1.0 / (2 * HD)) + EPS) * vrow(V_SSM_NW))

    active = []
    for g in range(G):
        active += [rwkv(g), dn(g), gla(g), ssd(g)]
    while active:
        alive = []
        for gen in active:
            try:
                next(gen)
                alive.append(gen)
            except StopIteration:
                pass
        active = alive

    @pl.when(t == n_t - 1)
    def _fin():
        for j in range(n_prev):
            for i in range(7):
                out_refs[i][j] = prev_refs[7 * j + i][...]
        for g in range(G * NS):
            shift1_ref[g] = xp_rw[g]
            dnc1_ref[g] = xp_dn[g]
            ssc1_ref[g] = xp_ss[g]
            for p in range(2):
                sw = s_wkv[g, p]
                sg = s_gla[g, p]
                sd = s_dn[g, p]
                ss = s_ssm[g, p]
                for h in range(2):
                    hs = slice(HD * h, HD * (h + 1))
                    wkv1_ref[g, 2 * p + h] = sw[hs, hs]
                    gla1_ref[g, 2 * p + h] = sg[hs, hs]
                    dn1_ref[g, 2 * p + h] = sd[hs, hs]
                    ssm1_ref[g, 2 * p + h] = ss[hs, :]


def _mixer_call(u_parts, states, mparams, layer, prev=()):
    u_rw, u_gla, u_dn, u_ssm, misc = u_parts
    shift0, wkv0, gla0, dnc0, dn0, ssc0, ssm0 = states
    bsz, tlen, _ = u_rw.shape
    tok = math.gcd(tlen, PROMPT_CHUNK)
    n_t = tlen // tok
    if n_t > 1:
        NS, R = 1, SEQS_PER_STEP_LONG
    else:
        NS = max(1, min(PROMPT_CHUNK // tok, SEQS_PER_STEP_SHORT))
        R = SEQS_PER_STEP_SHORT // NS * (1 if prev else 2)
    C = NS * tok
    G = R * NS
    assert bsz % G == 0 and tok & (tok - 1) == 0 and tok % SUBLANES == 0

    def tmap(b, t):
        return (b, t, 0)

    def bmap3(b, t):
        return (b, 0, 0)

    def bmap4(b, t):
        return (b, 0, 0, 0)

    def full(arr):
        nd = arr.ndim
        return pl.BlockSpec(arr.shape, lambda b, t: (0,) * nd)

    def lmap3(b, t):
        return (layer, b, 0, 0)

    def lmap4(b, t):
        return (layer, b, 0, 0, 0)

    sq = (G, NH, HD, HD)
    once = dict(pipeline_mode=pl.Buffered(1)) if bsz == G else {}
    in_state_specs = [pl.BlockSpec((None, G, SUBLANES, RWKV_COLS), lmap3, **once),
                      pl.BlockSpec((None,) + sq, lmap4, **once), pl.BlockSpec((None,) + sq, lmap4, **once),
                      pl.BlockSpec((None, G, SUBLANES, CONV_COLS), lmap3, **once),
                      pl.BlockSpec((None,) + sq, lmap4, **once),
                      pl.BlockSpec((None, G, SUBLANES, CONV_COLS), lmap3, **once),
                      pl.BlockSpec((None, G, NH, HD, SSM_STATE), lmap4, **once)]
    n_prev = len(prev)
    n_out = n_prev + 1
    tails = [(SUBLANES, RWKV_COLS), (NH, HD, HD), (NH, HD, HD), (SUBLANES, CONV_COLS), (NH, HD, HD),
             (SUBLANES, CONV_COLS), (NH, HD, SSM_STATE)]
    prev_specs = [pl.BlockSpec((G,) + tl, bmap3 if len(tl) == 2 else bmap4, **once) for tl in tails] * n_prev
    state_specs = [pl.BlockSpec((n_out, G) + tl, (lambda b, t: (0, b, 0, 0)) if len(tl) == 2
                                else (lambda b, t: (0, b, 0, 0, 0))) for tl in tails]
    state_shapes = [jax.ShapeDtypeStruct((n_out, bsz) + tl, F32) for tl in tails]
    outs = pl.pallas_call(
        functools.partial(_mixer_body, C, n_t, R, NS, n_prev),
        grid=(bsz // G, n_t),
        in_specs=[pl.BlockSpec((G, tok, RWKV_COLS), tmap), pl.BlockSpec((G, tok, 4 * GW), tmap),
                  pl.BlockSpec((G, tok, 4 * GW), tmap), pl.BlockSpec((G, tok, 4 * GW), tmap),
                  pl.BlockSpec((G, tok, LANES), tmap)] + in_state_specs + [full(a) for a in mparams] + prev_specs,
        out_specs=[pl.BlockSpec((G, tok, D_MODEL), tmap)] + state_specs,
        out_shape=[jax.ShapeDtypeStruct((bsz, tlen, D_MODEL), F32)] + state_shapes,
        scratch_shapes=[pltpu.VMEM((G, SUBLANES, RWKV_COLS), F32),
                        pltpu.VMEM((G, SUBLANES, CONV_COLS), F32),
                        pltpu.VMEM((G, SUBLANES, CONV_COLS), F32),
                        pltpu.VMEM((G, 2, LANES, LANES), F32), pltpu.VMEM((G, 2, LANES, LANES), F32),
                        pltpu.VMEM((G, 2, LANES, LANES), F32), pltpu.VMEM((G, 2, LANES, SSM_STATE), F32)],
        compiler_params=pltpu.CompilerParams(vmem_limit_bytes=VMEM_LIMIT,
                                             dimension_semantics=("parallel", "arbitrary")),
        name="mixers",
    )(u_rw, u_gla, u_dn, u_ssm, misc, shift0, wkv0, gla0, dnc0, dn0, ssc0, ssm0, *mparams,
      *[a for layer_states in prev for a in layer_states])
    return outs[0], tuple(outs[1:])


def _pad_rows(m, lo, total):
    return jnp.pad(m, ((lo, total - lo - m.shape[0]), (0, 0)))


def _lane_vec(vals, lo):
    return jnp.pad(vals, (lo, LANES - lo - vals.shape[0]))


def _pack_layer(P, l):
    v256 = jnp.stack([P['rwkv_w0'][l], P['rwkv_a0'][l], P['rwkv_k_k'][l], P['rwkv_k_a'][l],
                      P['rwkv_r_k'][l], P['rwkv_ln_w'][l], P['rwkv_ln_b'][l],
                      P['gla_gk_b'][l], P['gla_norm_w'][l], P['dn_norm_w'][l],
                      jnp.repeat(P['ssm_D'][l], HD), P['ssm_norm_w'][l]])
    v256 = jnp.pad(v256, ((0, 16 - v256.shape[0]), (0, 0)))
    v128 = jnp.stack([_lane_vec(P['dn_A_log'][l], MISC_DN_A), _lane_vec(P['dn_dt_bias'][l], MISC_DN_A),
                      _lane_vec(P['ssm_A_log'][l], MISC_SSM_DT), _lane_vec(P['ssm_dt_bias'][l], MISC_SSM_DT)])
    v128 = jnp.pad(v128, ((0, 4), (0, 0)))
    conv = jnp.concatenate([P['dn_conv_w'][l], P['ssm_conv_w'][l], P['ssm_conv_b'][l][None],
                            jnp.zeros((7, CONV_COLS), F32)], axis=0)
    lr = jnp.stack([_pad_rows(P['rwkv_w2'][l], 0, LANES), _pad_rows(P['rwkv_a2'][l], 32, LANES),
                    _pad_rows(P['rwkv_g2'][l], 64, LANES),
                    _pad_rows(P['gla_gk_w2'][l], MISC_GLA_GATE, LANES)]).astype(MXU_DT)
    return (P['rwkv_mu'][l][None], v256, v128, conv, lr)


def _pad_tail_rows(a):
    return jnp.pad(a, ((0, 0), (0, 0), (SUBLANES - a.shape[2], 0), (0, 0)))


def _trunk(x, mod_rows, states, mixer_params, dense, final_norm_w):
    n_layers = len(mixer_params)
    shift0, wkv0, gla0, dnc0, dn0, ssc0, ssm0 = states
    st_in = (_pad_tail_rows(shift0[:, :, None, :]), wkv0, gla0, _pad_tail_rows(dnc0), dn0,
             _pad_tail_rows(ssc0), ssm0)
    prev = []
    for l in range(n_layers):
        mod = mod_rows[l][:, None, :]
        u_parts = _inproj_call(x, mod, dense['norm1'][l], dense['w_r'], l)
        last = l == n_layers - 1
        y_mix, st = _mixer_call(u_parts, st_in, mixer_params[l], l, prev=tuple(prev) if last else ())
        x = _outmlp_call(x, y_mix, mod, dense['norm2'][l], final_norm_w, dense['w_out'],
                         dense['w_up'], dense['w_down'], l, final=last)
        if not last:
            prev.append(tuple(s[0] for s in st))
    shift1, wkv1, gla1, dnc1, dn1, ssc1, ssm1 = st
    return x, (shift1[:, :, SUBLANES - 1], wkv1, gla1, dnc1[:, :, SUBLANES - 3:], dn1,
               ssc1[:, :, SUBLANES - 3:], ssm1)


def kernel(x_prompt, x_sample, state_rwkv_shift, state_rwkv_wkv, state_gla, state_dn_conv,
           state_dn, state_ssm_conv, state_ssm, c_prompt, c_sample,
           ada_w, ada_b, norm1_w, norm2_w, w_in, w_out, w_up, w_down,
           rwkv_mu, rwkv_w0, rwkv_w2, rwkv_a0, rwkv_a2, rwkv_g2, rwkv_k_k, rwkv_k_a, rwkv_r_k,
           rwkv_ln_w, rwkv_ln_b, gla_gk_w2, gla_gk_b, gla_norm_w,
           dn_conv_w, dn_A_log, dn_dt_bias, dn_norm_w,
           ssm_conv_w, ssm_conv_b, ssm_dt_bias, ssm_A_log, ssm_D, ssm_norm_w, final_norm_w):
    P = dict(rwkv_mu=rwkv_mu, rwkv_w0=rwkv_w0, rwkv_w2=rwkv_w2, rwkv_a0=rwkv_a0, rwkv_a2=rwkv_a2,
             rwkv_g2=rwkv_g2, rwkv_k_k=rwkv_k_k, rwkv_k_a=rwkv_k_a, rwkv_r_k=rwkv_r_k,
             rwkv_ln_w=rwkv_ln_w, rwkv_ln_b=rwkv_ln_b, gla_gk_w2=gla_gk_w2, gla_gk_b=gla_gk_b,
             gla_norm_w=gla_norm_w, dn_conv_w=dn_conv_w, dn_A_log=dn_A_log, dn_dt_bias=dn_dt_bias,
             dn_norm_w=dn_norm_w, ssm_conv_w=ssm_conv_w, ssm_conv_b=ssm_conv_b,
             ssm_dt_bias=ssm_dt_bias, ssm_A_log=ssm_A_log, ssm_D=ssm_D, ssm_norm_w=ssm_norm_w)
    n_layers = w_in.shape[0]
    n_prompt = x_prompt.shape[0]
    mixer_params = [_pack_layer(P, l) for l in range(n_layers)]
    dense = dict(w_r=_wprep_call(w_in), w_out=w_out.astype(MXU_DT), w_up=w_up.astype(MXU_DT),
                 w_down=w_down.astype(MXU_DT), norm1=norm1_w, norm2=norm2_w)
    mod_all = _ada_call(jnp.concatenate([c_prompt, c_sample], axis=0), ada_w, ada_b)
    sample_states = (state_rwkv_shift, state_rwkv_wkv, state_gla, state_dn_conv,
                     state_dn, state_ssm_conv, state_ssm)
    prompt_states = tuple(jnp.zeros((n_layers, n_prompt) + s.shape[2:], F32) for s in sample_states)
    y_prompt, ps = _trunk(x_prompt, mod_all[:, :n_prompt], prompt_states, mixer_params, dense, final_norm_w)
    y_sample, ss = _trunk(x_sample, mod_all[:, n_prompt:], sample_states, mixer_params, dense, final_norm_w)
    return (y_prompt, y_sample) + ps + ss
```

```python
import functools
import math

import jax
import jax.numpy as jnp
from jax import lax
from jax.experimental import pallas as pl
from jax.experimental.pallas import tpu as pltpu

F32 = jnp.float32
MXU_DT = jnp.bfloat16

D_MODEL = 1024
NH = 4
HD = 64
GW = NH * HD
D_FF = 4 * D_MODEL
SSM_STATE = 128
EPS = 1e-6
RWKV_GN_EPS = 64e-5
GLA_GATE_NORM = 16.0
GLA_SUB = 16
RWKV_COLS = 3 * GW + 32 + 32 + 64
CONV_COLS = 3 * GW
U_COLS = RWKV_COLS + 3 * 4 * GW + 128
MISC_GLA_GATE = 0
MISC_DN_A = 16
MISC_DN_B = 20
MISC_SSM_DT = 24

LANES = 128
SUBLANES = 8
VMEM_LIMIT = 56 * 1024 * 1024
PROMPT_CHUNK = 64
ROW_TILE = 512
SEQS_PER_STEP_LONG = 4
SEQS_PER_STEP_SHORT = 8


def _mm(a, b):
    return jnp.dot(a.astype(MXU_DT), b.astype(MXU_DT), preferred_element_type=F32)


def _mm_nt(a, b):
    return lax.dot_general(a.astype(MXU_DT), b.astype(MXU_DT), (((1,), (1,)), ((), ())),
                           preferred_element_type=F32)


def _mm_tn(a, b):
    return lax.dot_general(a.astype(MXU_DT), b.astype(MXU_DT), (((0,), (0,)), ((), ())),
                           preferred_element_type=F32)


def _mm3(a, b):
    a_hi, a_lo = _split(a, 2)
    b_hi, b_lo = _split(b, 2)
    return (jnp.dot(a_hi, b_hi, preferred_element_type=F32) + jnp.dot(a_hi, b_lo, preferred_element_type=F32)
            + jnp.dot(a_lo, b_hi, preferred_element_type=F32))


def _split(x, n):
    parts = []
    r = x
    for i in range(n):
        p = r.astype(MXU_DT)
        parts.append(p)
        if i + 1 < n:
            r = r - p.astype(F32)
    return parts


def _mm_sel_lhs(sel, x, n=2):
    acc = None
    for p in _split(x, n):
        d = jnp.dot(sel, p, preferred_element_type=F32)
        acc = d if acc is None else acc + d
    return acc


def _mm_sel_rhs(x, sel, n=2):
    acc = None
    for p in _split(x, n):
        d = jnp.dot(p, sel, preferred_element_type=F32)
        acc = d if acc is None else acc + d
    return acc


def _seg_decay(cum_col, cum_row, incl):
    d = cum_col - cum_row
    return jnp.where(incl, jnp.exp(jnp.where(incl, d, 0.0)), 0.0)


def _silu(x):
    return x * jax.nn.sigmoid(x)


def _ada_body(c_ref, w_ref, b_ref, o_ref):
    c = c_ref[...]
    o_ref[0] = _mm(_silu(c), w_ref[0]) + b_ref[0]


def _ada_call(c_all, ada_w, ada_b):
    n_layers = ada_w.shape[0]
    rows = c_all.shape[0]
    tn = 1536
    return pl.pallas_call(
        _ada_body,
        grid=(n_layers, 6 * D_MODEL // tn),
        in_specs=[pl.BlockSpec((rows, D_MODEL), lambda l, j: (0, 0)),
                  pl.BlockSpec((1, D_MODEL, tn), lambda l, j: (l, 0, j)),
                  pl.BlockSpec((1, 1, tn), lambda l, j: (l, 0, j))],
        out_specs=pl.BlockSpec((1, rows, tn), lambda l, j: (l, 0, j)),
        out_shape=jax.ShapeDtypeStruct((n_layers, rows, 6 * D_MODEL), F32),
        compiler_params=pltpu.CompilerParams(vmem_limit_bytes=VMEM_LIMIT),
        name="ada_mod",
    )(c_all, ada_w, ada_b.reshape(n_layers, 1, 6 * D_MODEL))


W_IN_COLS = RWKV_COLS + (4 * GW + 16) + (4 * GW + 8) + (4 * GW + 4)


def _wprep_body(w_ref, o_ref):
    w = w_ref[0]
    o_gla = RWKV_COLS
    o_dn = o_gla + 4 * GW + 16
    o_ssm = o_dn + 4 * GW + 8
    parts = [w[:, 0:o_gla + 4 * GW],
             w[:, o_dn:o_dn + 4 * GW],
             w[:, o_ssm:o_ssm + 4 * GW],
             w[:, o_gla + 4 * GW:o_dn],
             w[:, o_dn + 4 * GW:o_ssm],
             w[:, o_ssm + 4 * GW:W_IN_COLS],
             jnp.zeros((w.shape[0], LANES - 28), F32)]
    o_ref[0] = jnp.concatenate(parts, axis=1).astype(MXU_DT)


def _wprep_call(w_in):
    n_layers = w_in.shape[0]
    tr = 128
    return pl.pallas_call(
        _wprep_body,
        grid=(n_layers, D_MODEL // tr),
        in_specs=[pl.BlockSpec((1, tr, W_IN_COLS), lambda l, i: (l, i, 0))],
        out_specs=pl.BlockSpec((1, tr, U_COLS), lambda l, i: (l, i, 0)),
        out_shape=jax.ShapeDtypeStruct((n_layers, D_MODEL, U_COLS), MXU_DT),
        compiler_params=pltpu.CompilerParams(vmem_limit_bytes=VMEM_LIMIT,
                                             dimension_semantics=("parallel", "parallel")),
        name="w_in_relayout",
    )(w_in)


def _rms(x):
    return x * lax.rsqrt(jnp.mean(x * x, -1, keepdims=True) + EPS)


def _inproj_body(x_ref, sh_ref, sc_ref, nw_ref, w_ref, o_rw, o_gla, o_dn, o_ssm, o_misc):
    bb, tt, _ = x_ref.shape
    h = _rms(x_ref[...]) * nw_ref[...]
    h = h * (1.0 + sc_ref[...]) + sh_ref[...]
    u = jnp.dot(h.reshape(bb * tt, D_MODEL).astype(MXU_DT), w_ref[...], preferred_element_type=F32)
    off = 0
    for ref in (o_rw, o_gla, o_dn, o_ssm, o_misc):
        w = ref.shape[-1]
        ref[...] = u[:, off:off + w].reshape(bb, tt, w)
        off += w


def _row_blocks(bsz, tlen):
    tt = min(tlen, ROW_TILE)
    bb = ROW_TILE // tt
    assert tlen % tt == 0 and bsz % bb == 0
    return bb, tt


def _inproj_call(x, mod, norm_w, w_r, layer):
    bsz, tlen, _ = x.shape
    bb, tt = _row_blocks(bsz, tlen)
    widths = (RWKV_COLS, 4 * GW, 4 * GW, 4 * GW, LANES)

    def xmap(i, j):
        return (i, j, 0)

    return pl.pallas_call(
        _inproj_body,
        grid=(bsz // bb, tlen // tt),
        in_specs=[pl.BlockSpec((bb, tt, D_MODEL), xmap),
                  pl.BlockSpec((bb, 1, D_MODEL), lambda i, j: (i, 0, 0)),
                  pl.BlockSpec((bb, 1, D_MODEL), lambda i, j: (i, 0, 1)),
                  pl.BlockSpec((1, 1, D_MODEL), lambda i, j: (0, 0, 0)),
                  pl.BlockSpec((None, D_MODEL, U_COLS), lambda i, j: (layer, 0, 0),
                               pipeline_mode=pl.Buffered(1))],
        out_specs=[pl.BlockSpec((bb, tt, w), xmap) for w in widths],
        out_shape=[jax.ShapeDtypeStruct((bsz, tlen, w), F32) for w in widths],
        compiler_params=pltpu.CompilerParams(vmem_limit_bytes=VMEM_LIMIT,
                                             dimension_semantics=("parallel", "parallel")),
        name="in_proj",
    )(x, mod, mod, norm_w.reshape(1, 1, D_MODEL), w_r)


def _outmlp_body(final, x_ref, y_ref, gt1_ref, sh_ref, sc_ref, gt2_ref, nw_ref, fnw_ref,
                 wo_ref, wu_ref, wd_ref, o_ref):
    bb, tt, _ = x_ref.shape
    rows = bb * tt
    att = jnp.dot(y_ref[...].reshape(rows, D_MODEL).astype(MXU_DT), wo_ref[...],
                  preferred_element_type=F32)
    x1 = x_ref[...] + gt1_ref[...] * att.reshape(bb, tt, D_MODEL)
    h = _rms(x1) * nw_ref[...]
    h = h * (1.0 + sc_ref[...]) + sh_ref[...]
    a = jnp.dot(h.reshape(rows, D_MODEL).astype(MXU_DT), wu_ref[...], preferred_element_type=F32)
    a = jnp.square(jnp.maximum(a, 0.0))
    f = jnp.dot(a.astype(MXU_DT), wd_ref[...], preferred_element_type=F32)
    x2 = x1 + gt2_ref[...] * f.reshape(bb, tt, D_MODEL)
    if final:
        x2 = _rms(x2) * fnw_ref[...]
    o_ref[...] = x2


def _outmlp_call(x, y_mix, mod, norm_w, final_norm_w, w_out, w_up, w_down, layer, final):
    bsz, tlen, _ = x.shape
    bb, tt = _row_blocks(bsz, tlen)

    def xmap(i, j):
        return (i, j, 0)

    def modspec(k):
        return pl.BlockSpec((bb, 1, D_MODEL), lambda i, j: (i, 0, k))

    def wspec(shape):
        return pl.BlockSpec((None,) + shape, lambda i, j: (layer, 0, 0), pipeline_mode=pl.Buffered(1))

    vec = pl.BlockSpec((1, 1, D_MODEL), lambda i, j: (0, 0, 0))
    return pl.pallas_call(
        functools.partial(_outmlp_body, final),
        grid=(bsz // bb, tlen // tt),
        in_specs=[pl.BlockSpec((bb, tt, D_MODEL), xmap), pl.BlockSpec((bb, tt, D_MODEL), xmap),
                  modspec(2), modspec(3), modspec(4), modspec(5), vec, vec,
                  wspec((D_MODEL, D_MODEL)), wspec((D_MODEL, D_FF)), wspec((D_FF, D_MODEL))],
        out_specs=pl.BlockSpec((bb, tt, D_MODEL), xmap),
        out_shape=jax.ShapeDtypeStruct((bsz, tlen, D_MODEL), F32),
        compiler_params=pltpu.CompilerParams(vmem_limit_bytes=VMEM_LIMIT,
                                             dimension_semantics=("parallel", "parallel")),
        name="out_mlp",
    )(x, y_mix, mod, mod, mod, mod, norm_w.reshape(1, 1, D_MODEL),
      final_norm_w.reshape(1, 1, D_MODEL), w_out, w_up, w_down)


(V_RW_W0, V_RW_A0, V_RW_KK, V_RW_KA, V_RW_RK, V_RW_LNW, V_RW_LNB,
 V_GLA_B, V_GLA_NW, V_DN_NW, V_SSM_D, V_SSM_NW) = range(12)
P_DN_ALOG, P_DN_DTB, P_SSM_ALOG, P_SSM_DTB = range(4)
CV_DN_W, CV_SSM_W, CV_SSM_B = 0, 4, 8
LR_RW_W2, LR_RW_A2, LR_RW_G2, LR_GLA_GK = range(4)


def _blockdiag(a, b):
    za = jnp.zeros_like(a)
    return jnp.concatenate([jnp.concatenate([a, za], axis=1),
                            jnp.concatenate([za, b], axis=1)], axis=0)


def _mixer_body(C, n_t, G, NS, n_prev, *refs):
    (u_rw_ref, u_gla_ref, u_dn_ref, u_ssm_ref, misc_ref,
     shift0_ref, wkv0_ref, gla0_ref, dnc0_ref, dn0_ref, ssc0_ref, ssm0_ref,
     mu_ref, v256_ref, v128_ref, conv_ref, lr_ref) = refs[:17]
    prev_refs = refs[17:17 + 7 * n_prev]
    y_ref = refs[17 + 7 * n_prev]
    out_refs = refs[18 + 7 * n_prev:25 + 7 * n_prev]
    xp_rw, xp_dn, xp_ss, s_wkv, s_gla, s_dn, s_ssm = refs[25 + 7 * n_prev:]
    shift1_ref, wkv1_ref, gla1_ref, dnc1_ref, dn1_ref, ssc1_ref, ssm1_ref = (r.at[n_prev] for r in out_refs)
    t = pl.program_id(1)

    @pl.when(t == 0)
    def _init():
        for g in range(G * NS):
            xp_rw[g] = shift0_ref[g]
            xp_dn[g] = dnc0_ref[g]
            xp_ss[g] = ssc0_ref[g]
            for p in range(2):
                s_wkv[g, p] = _blockdiag(wkv0_ref[g, 2 * p], wkv0_ref[g, 2 * p + 1])
                s_gla[g, p] = _blockdiag(gla0_ref[g, 2 * p], gla0_ref[g, 2 * p + 1])
                s_dn[g, p] = _blockdiag(dn0_ref[g, 2 * p], dn0_ref[g, 2 * p + 1])
                s_ssm[g, p] = jnp.concatenate([ssm0_ref[g, 2 * p], ssm0_ref[g, 2 * p + 1]], axis=0)

    ri = lax.broadcasted_iota(jnp.int32, (C, C), 0)
    ci = lax.broadcasted_iota(jnp.int32, (C, C), 1)
    SEG = C // NS
    lg = SEG.bit_length() - 1

    def same_seg(a, b):
        return lax.shift_right_logical(a, lg) == lax.shift_right_logical(b, lg)

    incl = (ci <= ri) & same_seg(ci, ri)
    ri2 = lax.broadcasted_iota(jnp.int32, (C, 2 * C), 0)
    ci2 = lax.broadcasted_iota(jnp.int32, (C, 2 * C), 1)
    left2 = ci2 < C
    cj2 = ci2 & (C - 1)
    incl2 = (cj2 <= ri2) & same_seg(cj2, ri2)
    strict2 = (cj2 < ri2) & same_seg(cj2, ri2)
    cj4 = lax.broadcasted_iota(jnp.int32, (C, 4 * C), 1) & (C - 1)
    ri4 = lax.broadcasted_iota(jnp.int32, (C, 4 * C), 0)
    incl4 = (cj4 <= ri4) & same_seg(cj4, ri4)
    tri = jnp.where(incl, 1.0, 0.0).astype(MXU_DT)
    lane = lax.broadcasted_iota(jnp.int32, (1, LANES), 1)
    hmask = (lane < HD, lane >= HD)
    r128 = lax.shift_right_logical(lax.broadcasted_iota(jnp.int32, (LANES, LANES), 0), 6)
    c128 = lax.shift_right_logical(lax.broadcasted_iota(jnp.int32, (LANES, LANES), 1), 6)
    bd = r128 == c128
    r256 = lax.broadcasted_iota(jnp.int32, (GW, GW), 0)
    c256 = lax.broadcasted_iota(jnp.int32, (GW, GW), 1)
    blk64 = jnp.where(lax.shift_right_logical(r256, 6) == lax.shift_right_logical(c256, 6),
                      1.0, 0.0).astype(MXU_DT)
    blk128 = jnp.where(lax.shift_right_logical(r256, 7) == lax.shift_right_logical(c256, 7),
                       1.0, 0.0).astype(MXU_DT)
    n_iter = 0
    while 2 * (1 << n_iter) < SEG:
        n_iter += 1

    def rows(*xs):
        return jnp.concatenate(xs, axis=0)

    def vrow(i):
        return v256_ref[i:i + 1, :]

    def load(ref, g):
        blk = ref[g * NS:(g + 1) * NS]
        return blk.reshape(C, blk.shape[-1])

    def store(ref, g, lo, val):
        w = val.shape[-1]
        ref[g * NS:(g + 1) * NS, :, lo:lo + w] = val.reshape(NS, SEG, w)

    def sq(x, q):
        return x[q * SEG:(q + 1) * SEG]

    def seg_last(x):
        if NS == 1:
            return x[C - 1:C]
        x3 = x.reshape(NS, SEG, x.shape[-1])
        return jnp.broadcast_to(x3[:, SEG - 1:SEG], x3.shape).reshape(x.shape)

    def last_row(x, q):
        return x[(q + 1) * SEG - 1:(q + 1) * SEG]

    def stack(parts):
        return parts[0] if NS == 1 else rows(*parts)

    def with_state(mm, lhs_list, ref, g, p):
        outs = [[] for _ in lhs_list]
        for q in range(NS):
            res = mm(rows(*[sq(a, q) for a in lhs_list]) if len(lhs_list) > 1 else sq(lhs_list[0], q),
                     ref[g * NS + q, p])
            for i in range(len(lhs_list)):
                outs[i].append(res[i * SEG:(i + 1) * SEG])
        return [stack(o) for o in outs]

    def update_state(ref, g, p, decay, lhs_list, rhs_list, masked):
        for q in range(NS):
            a = rows(*[sq(x, q) for x in lhs_list]) if len(lhs_list) > 1 else sq(lhs_list[0], q)
            b = rows(*[sq(x, q) for x in rhs_list]) if len(rhs_list) > 1 else sq(rhs_list[0], q)
            new = ref[g * NS + q, p] * decay(q) + _mm_tn(a, b)
            ref[g * NS + q, p] = jnp.where(bd, new, 0.0) if masked else new

    def hsum(x):
        return _mm_sel_rhs(x, blk64, n=1)

    def head(x, h):
        return jnp.where(hmask[h], x, 0.0)

    def hrows(x):
        return rows(head(x, 0), head(x, 1))

    def pair_nt(a, b):
        return _mm_nt(a, hrows(b))

    def pair_apply(q2, y):
        return _mm(q2, hrows(y))

    def bd2(q2):
        return rows(jnp.where(left2, q2, 0.0), jnp.where(left2, 0.0, q2))

    def seg2(tile, tile_t, lane0):
        col = jnp.where(left2, tile[:, lane0:lane0 + 1], tile[:, lane0 + 1:lane0 + 2])
        row = jnp.concatenate([tile_t[lane0:lane0 + 1, :], tile_t[lane0 + 1:lane0 + 2, :]], axis=1)
        return _seg_decay(col, row, incl2)

    def expand(tile, base):
        cols = [jnp.broadcast_to(tile[:, base + h:base + h + 1], (C, LANES)) for h in range(NH)]
        return [jnp.where(hmask[0], cols[0], cols[1]), jnp.where(hmask[0], cols[2], cols[3])]

    row8 = lax.broadcasted_iota(jnp.int32, (SUBLANES, 1), 0)

    def shifted(x, prev8, k):
        if NS > 1:
            x3 = x.reshape(NS, SEG, x.shape[-1])
            r3 = lax.broadcasted_iota(jnp.int32, (1, SEG, 1), 1)
            return jnp.where(r3 < k, pltpu.roll(prev8, k, 1), pltpu.roll(x3, k, 1)).reshape(x.shape)
        prev8 = prev8[0]
        top = jnp.where(row8 < k, pltpu.roll(prev8, k, 0), pltpu.roll(x, k, 0)[0:SUBLANES])
        if C == SUBLANES:
            return top
        return rows(top, pltpu.roll(x, k, 0)[SUBLANES:])

    def psl(p):
        return slice(LANES * p, LANES * (p + 1))

    shared = {}

    def neumann(n_list, steps):
        rs = list(n_list)
        ps = [_mm(p, bd2(p)) for p in n_list]
        yield None
        for _ in range(steps - 1):
            both = [_mm(rows(p, r), bd2(p)) for p, r in zip(ps, rs)]
            yield None
            rs = [r + p + b[C:2 * C] for r, p, b in zip(rs, ps, both)]
            ps = [b[0:C] for b in both]
        upd = [_mm(r, bd2(p)) for p, r in zip(ps, rs)]
        yield None
        yield [r + p + u for r, p, u in zip(rs, ps, upd)]

    def rwkv(g):
        u_rw = load(u_rw_ref, g)
        u_prev = shifted(u_rw, xp_rw[g * NS:(g + 1) * NS], 1)
        xp_rw[g * NS:(g + 1) * NS] = u_rw.reshape(NS, SEG, RWKV_COLS)[:, SEG - SUBLANES:SEG]
        xs = u_rw + (u_prev - u_rw) * mu_ref[...]
        r = xs[:, 0:GW]
        k = xs[:, GW:2 * GW]
        v = xs[:, 2 * GW:3 * GW]
        x7 = xs[:, 3 * GW:RWKV_COLS]
        yield
        w_pre = _mm(jnp.tanh(x7), lr_ref[LR_RW_W2])
        a_pre = _mm(x7, lr_ref[LR_RW_A2])
        gate = _mm(jax.nn.sigmoid(x7), lr_ref[LR_RW_G2])
        yield
        kk = k * vrow(V_RW_KK)
        kk_ss = hsum(kk * kk)
        yield
        log_w = -jax.nn.softplus(-(vrow(V_RW_W0) + w_pre)) - 0.5
        lw = -jnp.exp(log_w)
        yield
        a = jax.nn.sigmoid(vrow(V_RW_A0) + a_pre)
        yield
        kk = kk * lax.rsqrt(kk_ss + EPS)
        k2 = k * (1.0 + (a - 1.0) * vrow(V_RW_KA))
        bcum = _mm_sel_lhs(tri, lw)
        bonus = hsum(r * k2 * vrow(V_RW_RK))
        yield
        e_nb = jnp.exp(-bcum)
        blast = seg_last(bcum)
        e_rem = jnp.exp(blast - bcum)
        yield
        nka = -kk * a
        r_t = r * jnp.exp(bcum)
        yield
        a_t = kk * jnp.exp(bcum - lw)
        b_t = nka * e_nb
        yield
        k_t = k2 * e_nb
        k_rem = k2 * e_rem
        b_rem = nka * e_rem
        yield
        ars = [with_state(_mm_nt, [a_t[:, psl(p)], r_t[:, psl(p)]], s_wkv, g, p) for p in range(2)]
        ar = [_mm_nt(rows(a_t[:, psl(p)], r_t[:, psl(p)]), rows(hrows(b_t[:, psl(p)]), hrows(k_t[:, psl(p)])))
              for p in range(2)]
        yield
        n_ab = [jnp.where(strict2, ar[p][0:C, 0:2 * C], 0.0) for p in range(2)]
        n_ak = [jnp.where(strict2, ar[p][0:C, 2 * C:4 * C], 0.0) for p in range(2)]
        a_r = [jnp.where(incl4, ar[p][C:2 * C], 0.0) for p in range(2)]
        rhs = [ars[p][0] + pair_apply(n_ak[p], v[:, psl(p)]) for p in range(2)]
        tinv = None
        for step in neumann(n_ab, n_iter):
            if step is None:
                yield
            else:
                tinv = step
        yield
        uh = [rhs[p] + pair_apply(tinv[p], rhs[p]) for p in range(2)]
        yield
        y_pairs = []
        for p in range(2):
            sl = psl(p)
            vp = v[:, sl]
            u_p = uh[p]
            yp = ars[p][1] + _mm(a_r[p], rows(hrows(u_p), hrows(vp)))
            update_state(s_wkv, g, p, lambda q: jnp.exp(last_row(bcum, q)[:, sl]),
                         [vp, u_p], [k_rem[:, sl], b_rem[:, sl]], True)
            y_pairs.append(yp)
        yield
        y = jnp.concatenate(y_pairs, axis=1)
        mean = hsum(y) * (1.0 / HD)
        yield
        yc = y - mean
        var = hsum(yc * yc) * (1.0 / HD)
        yield
        y = yc * lax.rsqrt(var + RWKV_GN_EPS) * vrow(V_RW_LNW) + vrow(V_RW_LNB)
        store(y_ref, g, 0, (y + bonus * v) * gate)

    def gla(g):
        u_gla = load(u_gla_ref, g)
        q = u_gla[:, 0:GW] * (HD ** -0.5)
        k = u_gla[:, GW:2 * GW]
        v = u_gla[:, 2 * GW:3 * GW]
        gz = u_gla[:, 3 * GW:4 * GW]
        gate = _mm(load(misc_ref, g), lr_ref[LR_GLA_GK])
        yield
        la = jax.nn.log_sigmoid(gate + vrow(V_GLA_B)) * (1.0 / GLA_GATE_NORM)
        bcum = _mm_sel_lhs(tri, la)
        yield
        blast = seg_last(bcum)
        q_in = q * jnp.exp(bcum)
        yield
        k_out = k * jnp.exp(blast - bcum)
        qs = [with_state(_mm, [q_in[:, psl(p)]], s_gla, g, p)[0] for p in range(2)]
        a_blocks = [[], []]
        k_b = None
        ref_prev = None
        for lo in range(0, C, GLA_SUB):
            hi = min(lo + GLA_SUB, C)
            ref = bcum[lo - 1:lo, :] if lo else jnp.zeros((1, GW), F32)
            q_b = q[lo:hi] * jnp.exp(bcum[lo:hi] - ref)
            k_new = k[lo:hi] * jnp.exp(ref - bcum[lo:hi])
            if k_b is None:
                k_b = rows(k_new, jnp.zeros((C - hi, GW), F32)) if hi < C else k_new
            else:
                old = k_b[0:lo] * jnp.exp(ref - ref_prev)
                k_b = rows(old, k_new, jnp.zeros((C - hi, GW), F32)) if hi < C else rows(old, k_new)
            ref_prev = ref
            for p in range(2):
                a_blocks[p].append(pair_nt(q_b[:, psl(p)], k_b[:, psl(p)]))
            yield
        a_h = [rows(*a_blocks[p]) for p in range(2)]
        for p in range(2):
            sl = psl(p)
            def kdecay(q, sl=sl):
                row = jnp.exp(last_row(bcum, q)[:, sl])
                return jnp.broadcast_to(row, (SUBLANES, LANES)).T[:, 0:1]

            update_state(s_gla, g, p, kdecay, [k_out[:, sl]], [v[:, sl]], True)
        yield
        o_pairs = []
        for p in range(2):
            o_pairs.append(qs[p] + pair_apply(jnp.where(incl2, a_h[p], 0.0), v[:, psl(p)]))
        yield
        o = jnp.concatenate(o_pairs, axis=1)
        ss = hsum(o * o)
        yield
        o = o * lax.rsqrt(ss * (1.0 / HD) + EPS) * vrow(V_GLA_NW)
        store(y_ref, g, GW, o * _silu(gz))

    def dn(g):
        misc = load(misc_ref, g)
        g_dn = -jnp.exp(v128_ref[P_DN_ALOG:P_DN_ALOG + 1, :]) * jax.nn.softplus(
            misc + v128_ref[P_DN_DTB:P_DN_DTB + 1, :])
        dt_ss = jax.nn.softplus(misc + v128_ref[P_SSM_DTB:P_SSM_DTB + 1, :])
        la_ss = dt_ss * -jnp.exp(v128_ref[P_SSM_ALOG:P_SSM_ALOG + 1, :])
        beta = jax.nn.sigmoid(misc)
        cum = _mm_sel_lhs(tri, jnp.where(lane < MISC_DN_B, g_dn, la_ss))
        u_dn = load(u_dn_ref, g)
        raw = u_dn[:, 0:CONV_COLS]
        prev8 = xp_dn[g * NS:(g + 1) * NS]
        qkv = raw * conv_ref[CV_DN_W + 3:CV_DN_W + 4, :]
        for k in (1, 2, 3):
            qkv = qkv + shifted(raw, prev8, k) * conv_ref[CV_DN_W + 3 - k:CV_DN_W + 4 - k, :]
        xp_dn[g * NS:(g + 1) * NS] = raw.reshape(NS, SEG, CONV_COLS)[:, SEG - SUBLANES:SEG]
        yield
        qkv = _silu(qkv)
        q = qkv[:, 0:GW]
        k = qkv[:, GW:2 * GW]
        v = qkv[:, 2 * GW:3 * GW]
        z = u_dn[:, CONV_COLS:CONV_COLS + GW]
        q_ss = hsum(q * q)
        k_ss = hsum(k * k)
        yield
        cum_t = cum.T
        shared[g] = (cum, cum_t, dt_ss)
        q = q * lax.rsqrt(q_ss + EPS) * (HD ** -0.5)
        k = k * lax.rsqrt(k_ss + EPS)
        yield
        beta_x = expand(beta, MISC_DN_B)
        cum_x = expand(cum, MISC_DN_A)
        yield
        ecx, kb, vb, kbe, qs, kq = [], [], [], [], [], []
        for p in range(2):
            sl = psl(p)
            ecx.append(jnp.exp(cum_x[p]))
            kb.append(k[:, sl] * beta_x[p])
            vb.append(v[:, sl] * beta_x[p])
            kbe.append(kb[p] * ecx[p])
            yield
            qs.append(with_state(_mm, [q[:, sl] * ecx[p]], s_dn, g, p)[0])
            kq.append(pair_nt(rows(kb[p], q[:, sl]), k[:, sl]))
        yield
        n_m, a_l = [], []
        for p in range(2):
            dec = seg2(cum, cum_t, MISC_DN_A + 2 * p)
            n_m.append(jnp.where(strict2, -(kq[p][0:C] * dec), 0.0))
            a_l.append(kq[p][C:2 * C] * dec)
        tinv = None
        for step in neumann(n_m, n_iter - 1):
            if step is None:
                yield
            else:
                tinv = step
        yield
        err = [n_m[p] - tinv[p] + _mm3(n_m[p], bd2(tinv[p])) for p in range(2)]
        yield
        tinv = [tinv[p] + err[p] + _mm(tinv[p], bd2(err[p])) for p in range(2)]
        yield
        vk = [jnp.concatenate([vb[p], kbe[p]], axis=1) for p in range(2)]
        uw = [vk[p] + _mm(tinv[p], jnp.concatenate([hrows(vb[p]), hrows(kbe[p])], axis=1)) for p in range(2)]
        yield
        u_p = [uw[p][:, 0:LANES] for p in range(2)]
        w_p = [uw[p][:, LANES:2 * LANES] for p in range(2)]
        ws = [with_state(_mm, [w_p[p]], s_dn, g, p)[0] for p in range(2)]
        yield
        o_pairs = []
        for p in range(2):
            sl = psl(p)
            cx = cum_x[p]
            last = seg_last(cx)
            v_new = u_p[p] - ws[p]
            op = qs[p] + pair_apply(a_l[p], v_new)
            update_state(s_dn, g, p, lambda q: jnp.exp(last_row(cx, q)),
                         [k[:, sl] * jnp.exp(last - cx)], [v_new], True)
            o_pairs.append(op)
        yield
        o = jnp.concatenate(o_pairs, axis=1)
        ss = hsum(o * o)
        yield
        o = o * lax.rsqrt(ss * (1.0 / HD) + EPS) * vrow(V_DN_NW)
        store(y_ref, g, 2 * GW, o * _silu(z))

    def ssd(g):
        u_ssm = load(u_ssm_ref, g)
        z = u_ssm[:, 0:GW]
        raw = u_ssm[:, GW:GW + CONV_COLS]
        prev8 = xp_ss[g * NS:(g + 1) * NS]
        xbc = conv_ref[CV_SSM_B:CV_SSM_B + 1, :] + raw * conv_ref[CV_SSM_W + 3:CV_SSM_W + 4, :]
        for k in (1, 2, 3):
            xbc = xbc + shifted(raw, prev8, k) * conv_ref[CV_SSM_W + 3 - k:CV_SSM_W + 4 - k, :]
        xp_ss[g * NS:(g + 1) * NS] = raw.reshape(NS, SEG, CONV_COLS)[:, SEG - SUBLANES:SEG]
        yield
        xbc = _silu(xbc)
        yield
        xs_ = xbc[:, 0:GW]
        bm = xbc[:, GW:2 * GW]
        cm = xbc[:, 2 * GW:3 * GW]
        gmat = [_mm_nt(cm[:, psl(p)], rows(bm[:, psl(p)], bm[:, psl(p)])) for p in range(2)]
        cs = [with_state(_mm_nt, [cm[:, psl(p)]], s_ssm, g, p)[0] for p in range(2)]
        yield
        while g not in shared:
            yield
        cum, cum_t, dt_ss = shared[g]
        dt_x = expand(dt_ss, MISC_SSM_DT)
        cum_x = expand(cum, MISC_SSM_DT)
        yield
        y_pairs = []
        for p in range(2):
            sl = psl(p)
            xh = xs_[:, sl]
            xdt = xh * dt_x[p]
            cx = cum_x[p]
            last = seg_last(cx)
            yp = jnp.exp(cx) * cs[p] + v256_ref[V_SSM_D:V_SSM_D + 1, sl] * xh
            yp = yp + pair_apply(gmat[p] * seg2(cum, cum_t, MISC_SSM_DT + 2 * p), xdt)
            lane0 = MISC_SSM_DT + 2 * p

            def dcol(q, lane0=lane0):
                lr = last_row(cum, q)
                return rows(jnp.broadcast_to(jnp.exp(lr[:, lane0:lane0 + 1]), (HD, SSM_STATE)),
                            jnp.broadcast_to(jnp.exp(lr[:, lane0 + 1:lane0 + 2]), (HD, SSM_STATE)))

            update_state(s_ssm, g, p, dcol, [xdt * jnp.exp(last - cx)], [bm[:, sl]], False)
            y_pairs.append(yp)
            yield
        yield
        y = jnp.concatenate(y_pairs, axis=1) * _silu(z)
        ss = _mm_sel_rhs(y * y, blk128)
        yield
        store(y_ref, g, 3 * GW, y * lax.rsqrt(ss * (1.0 / (2 * HD)) + EPS) * vrow(V_SSM_NW))

    active = []
    for g in range(G):
        active += [rwkv(g), gla(g), dn(g), ssd(g)]
    while active:
        alive = []
        for gen in active:
            try:
                next(gen)
                alive.append(gen)
            except StopIteration:
                pass
        active = alive

    @pl.when(t == n_t - 1)
    def _fin():
        for j in range(n_prev):
            for i in range(7):
                out_refs[i][j] = prev_refs[7 * j + i][...]
        for g in range(G * NS):
            shift1_ref[g] = xp_rw[g]
            dnc1_ref[g] = xp_dn[g]
            ssc1_ref[g] = xp_ss[g]
            for p in range(2):
                sw = s_wkv[g, p]
                sg = s_gla[g, p]
                sd = s_dn[g, p]
                ss = s_ssm[g, p]
                for h in range(2):
                    hs = slice(HD * h, HD * (h + 1))
                    wkv1_ref[g, 2 * p + h] = sw[hs, hs]
                    gla1_ref[g, 2 * p + h] = sg[hs, hs]
                    dn1_ref[g, 2 * p + h] = sd[hs, hs]
                    ssm1_ref[g, 2 * p + h] = ss[hs, :]


def _mixer_call(u_parts, states, mparams, layer, prev=()):
    u_rw, u_gla, u_dn, u_ssm, misc = u_parts
    shift0, wkv0, gla0, dnc0, dn0, ssc0, ssm0 = states
    bsz, tlen, _ = u_rw.shape
    tok = math.gcd(tlen, PROMPT_CHUNK)
    n_t = tlen // tok
    if n_t > 1:
        NS, R = 1, SEQS_PER_STEP_LONG
    else:
        NS = max(1, min(PROMPT_CHUNK // tok, SEQS_PER_STEP_SHORT))
        R = SEQS_PER_STEP_SHORT // NS * (1 if prev else 2)
    C = NS * tok
    G = R * NS
    assert bsz % G == 0 and tok & (tok - 1) == 0 and tok % SUBLANES == 0

    def tmap(b, t):
        return (b, t, 0)

    def bmap3(b, t):
        return (b, 0, 0)

    def bmap4(b, t):
        return (b, 0, 0, 0)

    def full(arr):
        nd = arr.ndim
        return pl.BlockSpec(arr.shape, lambda b, t: (0,) * nd)

    def lmap3(b, t):
        return (layer, b, 0, 0)

    def lmap4(b, t):
        return (layer, b, 0, 0, 0)

    sq = (G, NH, HD, HD)
    in_state_specs = [pl.BlockSpec((None, G, SUBLANES, RWKV_COLS), lmap3), pl.BlockSpec((None,) + sq, lmap4),
                      pl.BlockSpec((None,) + sq, lmap4), pl.BlockSpec((None, G, SUBLANES, CONV_COLS), lmap3),
                      pl.BlockSpec((None,) + sq, lmap4), pl.BlockSpec((None, G, SUBLANES, CONV_COLS), lmap3),
                      pl.BlockSpec((None, G, NH, HD, SSM_STATE), lmap4)]
    n_prev = len(prev)
    n_out = n_prev + 1
    tails = [(SUBLANES, RWKV_COLS), (NH, HD, HD), (NH, HD, HD), (SUBLANES, CONV_COLS), (NH, HD, HD),
             (SUBLANES, CONV_COLS), (NH, HD, SSM_STATE)]
    prev_specs = [pl.BlockSpec((G,) + tl, bmap3 if len(tl) == 2 else bmap4) for tl in tails] * n_prev
    state_specs = [pl.BlockSpec((n_out, G) + tl, (lambda b, t: (0, b, 0, 0)) if len(tl) == 2
                                else (lambda b, t: (0, b, 0, 0, 0))) for tl in tails]
    state_shapes = [jax.ShapeDtypeStruct((n_out, bsz) + tl, F32) for tl in tails]
    outs = pl.pallas_call(
        functools.partial(_mixer_body, C, n_t, R, NS, n_prev),
        grid=(bsz // G, n_t),
        in_specs=[pl.BlockSpec((G, tok, RWKV_COLS), tmap), pl.BlockSpec((G, tok, 4 * GW), tmap),
                  pl.BlockSpec((G, tok, 4 * GW), tmap), pl.BlockSpec((G, tok, 4 * GW), tmap),
                  pl.BlockSpec((G, tok, LANES), tmap)] + in_state_specs + [full(a) for a in mparams] + prev_specs,
        out_specs=[pl.BlockSpec((G, tok, D_MODEL), tmap)] + state_specs,
        out_shape=[jax.ShapeDtypeStruct((bsz, tlen, D_MODEL), F32)] + state_shapes,
        scratch_shapes=[pltpu.VMEM((G, SUBLANES, RWKV_COLS), F32),
                        pltpu.VMEM((G, SUBLANES, CONV_COLS), F32),
                        pltpu.VMEM((G, SUBLANES, CONV_COLS), F32),
                        pltpu.VMEM((G, 2, LANES, LANES), F32), pltpu.VMEM((G, 2, LANES, LANES), F32),
                        pltpu.VMEM((G, 2, LANES, LANES), F32), pltpu.VMEM((G, 2, LANES, SSM_STATE), F32)],
        compiler_params=pltpu.CompilerParams(vmem_limit_bytes=VMEM_LIMIT,
                                             dimension_semantics=("parallel", "arbitrary")),
        name="mixers",
    )(u_rw, u_gla, u_dn, u_ssm, misc, shift0, wkv0, gla0, dnc0, dn0, ssc0, ssm0, *mparams,
      *[a for layer_states in prev for a in layer_states])
    return outs[0], tuple(outs[1:])


def _pad_rows(m, lo, total):
    return jnp.pad(m, ((lo, total - lo - m.shape[0]), (0, 0)))


def _lane_vec(vals, lo):
    return jnp.pad(vals, (lo, LANES - lo - vals.shape[0]))


def _pack_layer(P, l):
    v256 = jnp.stack([P['rwkv_w0'][l], P['rwkv_a0'][l], P['rwkv_k_k'][l], P['rwkv_k_a'][l],
                      P['rwkv_r_k'][l], P['rwkv_ln_w'][l], P['rwkv_ln_b'][l],
                      P['gla_gk_b'][l], P['gla_norm_w'][l], P['dn_norm_w'][l],
                      jnp.repeat(P['ssm_D'][l], HD), P['ssm_norm_w'][l]])
    v256 = jnp.pad(v256, ((0, 16 - v256.shape[0]), (0, 0)))
    v128 = jnp.stack([_lane_vec(P['dn_A_log'][l], MISC_DN_A), _lane_vec(P['dn_dt_bias'][l], MISC_DN_A),
                      _lane_vec(P['ssm_A_log'][l], MISC_SSM_DT), _lane_vec(P['ssm_dt_bias'][l], MISC_SSM_DT)])
    v128 = jnp.pad(v128, ((0, 4), (0, 0)))
    conv = jnp.concatenate([P['dn_conv_w'][l], P['ssm_conv_w'][l], P['ssm_conv_b'][l][None],
                            jnp.zeros((7, CONV_COLS), F32)], axis=0)
    lr = jnp.stack([_pad_rows(P['rwkv_w2'][l], 0, LANES), _pad_rows(P['rwkv_a2'][l], 32, LANES),
                    _pad_rows(P['rwkv_g2'][l], 64, LANES),
                    _pad_rows(P['gla_gk_w2'][l], MISC_GLA_GATE, LANES)]).astype(MXU_DT)
    return (P['rwkv_mu'][l][None], v256, v128, conv, lr)


def _pad_tail_rows(a):
    return jnp.pad(a, ((0, 0), (0, 0), (SUBLANES - a.shape[2], 0), (0, 0)))


def _trunk(x, mod_rows, states, mixer_params, dense, final_norm_w):
    n_layers = len(mixer_params)
    shift0, wkv0, gla0, dnc0, dn0, ssc0, ssm0 = states
    st_in = (_pad_tail_rows(shift0[:, :, None, :]), wkv0, gla0, _pad_tail_rows(dnc0), dn0,
             _pad_tail_rows(ssc0), ssm0)
    prev = []
    for l in range(n_layers):
        mod = mod_rows[l][:, None, :]
        u_parts = _inproj_call(x, mod, dense['norm1'][l], dense['w_r'], l)
        last = l == n_layers - 1
        y_mix, st = _mixer_call(u_parts, st_in, mixer_params[l], l, prev=tuple(prev) if last else ())
        x = _outmlp_call(x, y_mix, mod, dense['norm2'][l], final_norm_w, dense['w_out'],
                         dense['w_up'], dense['w_down'], l, final=last)
        if not last:
            prev.append(tuple(s[0] for s in st))
    shift1, wkv1, gla1, dnc1, dn1, ssc1, ssm1 = st
    return x, (shift1[:, :, SUBLANES - 1], wkv1, gla1, dnc1[:, :, SUBLANES - 3:], dn1,
               ssc1[:, :, SUBLANES - 3:], ssm1)


def kernel(x_prompt, x_sample, state_rwkv_shift, state_rwkv_wkv, state_gla, state_dn_conv,
           state_dn, state_ssm_conv, state_ssm, c_prompt, c_sample,
           ada_w, ada_b, norm1_w, norm2_w, w_in, w_out, w_up, w_down,
           rwkv_mu, rwkv_w0, rwkv_w2, rwkv_a0, rwkv_a2, rwkv_g2, rwkv_k_k, rwkv_k_a, rwkv_r_k,
           rwkv_ln_w, rwkv_ln_b, gla_gk_w2, gla_gk_b, gla_norm_w,
           dn_conv_w, dn_A_log, dn_dt_bias, dn_norm_w,
           ssm_conv_w, ssm_conv_b, ssm_dt_bias, ssm_A_log, ssm_D, ssm_norm_w, final_norm_w):
    P = dict(rwkv_mu=rwkv_mu, rwkv_w0=rwkv_w0, rwkv_w2=rwkv_w2, rwkv_a0=rwkv_a0, rwkv_a2=rwkv_a2,
             rwkv_g2=rwkv_g2, rwkv_k_k=rwkv_k_k, rwkv_k_a=rwkv_k_a, rwkv_r_k=rwkv_r_k,
             rwkv_ln_w=rwkv_ln_w, rwkv_ln_b=rwkv_ln_b, gla_gk_w2=gla_gk_w2, gla_gk_b=gla_gk_b,
             gla_norm_w=gla_norm_w, dn_conv_w=dn_conv_w, dn_A_log=dn_A_log, dn_dt_bias=dn_dt_bias,
             dn_norm_w=dn_norm_w, ssm_conv_w=ssm_conv_w, ssm_conv_b=ssm_conv_b,
             ssm_dt_bias=ssm_dt_bias, ssm_A_log=ssm_A_log, ssm_D=ssm_D, ssm_norm_w=ssm_norm_w)
    n_layers = w_in.shape[0]
    n_prompt = x_prompt.shape[0]
    mixer_params = [_pack_layer(P, l) for l in range(n_layers)]
    dense = dict(w_r=_wprep_call(w_in), w_out=w_out.astype(MXU_DT), w_up=w_up.astype(MXU_DT),
                 w_down=w_down.astype(MXU_DT), norm1=norm1_w, norm2=norm2_w)
    mod_all = _ada_call(jnp.concatenate([c_prompt, c_sample], axis=0), ada_w, ada_b)
    sample_states = (state_rwkv_shift, state_rwkv_wkv, state_gla, state_dn_conv,
                     state_dn, state_ssm_conv, state_ssm)
    prompt_states = tuple(jnp.zeros((n_layers, n_prompt) + s.shape[2:], F32) for s in sample_states)
    y_prompt, ps = _trunk(x_prompt, mod_all[:, :n_prompt], prompt_states, mixer_params, dense, final_norm_w)
    y_sample, ss = _trunk(x_sample, mod_all[:, n_prompt:], sample_states, mixer_params, dense, final_norm_w)
    return (y_prompt, y_sample) + ps + ss
```

```python
import functools
import math

import jax
import jax.numpy as jnp
from jax import lax
from jax.experimental import pallas as pl
from jax.experimental.pallas import tpu as pltpu

F32 = jnp.float32
MXU_DT = jnp.bfloat16

D_MODEL = 1024
NH = 4
HD = 64
GW = NH * HD
D_FF = 4 * D_MODEL
SSM_STATE = 128
EPS = 1e-6
RWKV_GN_EPS = 64e-5
GLA_GATE_NORM = 16.0
GLA_SUB = 16
RWKV_COLS = 3 * GW + 32 + 32 + 64
CONV_COLS = 3 * GW
U_COLS = RWKV_COLS + 3 * 4 * GW + 128
MISC_GLA_GATE = 0
MISC_DN_A = 16
MISC_DN_B = 20
MISC_SSM_DT = 24

LANES = 128
SUBLANES = 8
VMEM_LIMIT = 56 * 1024 * 1024
PROMPT_CHUNK = 64
ROW_TILE = 512
SEQS_PER_STEP_LONG = 4
SEQS_PER_STEP_SHORT = 8


def _mm(a, b):
    return jnp.dot(a.astype(MXU_DT), b.astype(MXU_DT), preferred_element_type=F32)


def _mm_nt(a, b):
    return lax.dot_general(a.astype(MXU_DT), b.astype(MXU_DT), (((1,), (1,)), ((), ())),
                           preferred_element_type=F32)


def _mm_tn(a, b):
    return lax.dot_general(a.astype(MXU_DT), b.astype(MXU_DT), (((0,), (0,)), ((), ())),
                           preferred_element_type=F32)


def _mm3(a, b):
    a_hi, a_lo = _split(a, 2)
    b_hi, b_lo = _split(b, 2)
    return (jnp.dot(a_hi, b_hi, preferred_element_type=F32) + jnp.dot(a_hi, b_lo, preferred_element_type=F32)
            + jnp.dot(a_lo, b_hi, preferred_element_type=F32))


def _split(x, n):
    parts = []
    r = x
    for i in range(n):
        p = r.astype(MXU_DT)
        parts.append(p)
        if i + 1 < n:
            r = r - p.astype(F32)
    return parts


def _mm_sel_lhs(sel, x, n=2):
    acc = None
    for p in _split(x, n):
        d = jnp.dot(sel, p, preferred_element_type=F32)
        acc = d if acc is None else acc + d
    return acc


def _mm_sel_rhs(x, sel, n=2):
    acc = None
    for p in _split(x, n):
        d = jnp.dot(p, sel, preferred_element_type=F32)
        acc = d if acc is None else acc + d
    return acc


def _seg_decay(cum_col, cum_row, incl):
    d = cum_col - cum_row
    return jnp.where(incl, jnp.exp(jnp.where(incl, d, 0.0)), 0.0)


def _silu(x):
    return x * jax.nn.sigmoid(x)


def _ada_body(c_ref, w_ref, b_ref, o_ref):
    c = c_ref[...]
    o_ref[0] = _mm(_silu(c), w_ref[0]) + b_ref[0]


def _ada_call(c_all, ada_w, ada_b):
    n_layers = ada_w.shape[0]
    rows = c_all.shape[0]
    tn = 1536
    return pl.pallas_call(
        _ada_body,
        grid=(n_layers, 6 * D_MODEL // tn),
        in_specs=[pl.BlockSpec((rows, D_MODEL), lambda l, j: (0, 0)),
                  pl.BlockSpec((1, D_MODEL, tn), lambda l, j: (l, 0, j)),
                  pl.BlockSpec((1, 1, tn), lambda l, j: (l, 0, j))],
        out_specs=pl.BlockSpec((1, rows, tn), lambda l, j: (l, 0, j)),
        out_shape=jax.ShapeDtypeStruct((n_layers, rows, 6 * D_MODEL), F32),
        compiler_params=pltpu.CompilerParams(vmem_limit_bytes=VMEM_LIMIT),
        name="ada_mod",
    )(c_all, ada_w, ada_b.reshape(n_layers, 1, 6 * D_MODEL))


W_IN_COLS = RWKV_COLS + (4 * GW + 16) + (4 * GW + 8) + (4 * GW + 4)


def _wprep_body(w_ref, o_ref):
    w = w_ref[0]
    o_gla = RWKV_COLS
    o_dn = o_gla + 4 * GW + 16
    o_ssm = o_dn + 4 * GW + 8
    parts = [w[:, 0:o_gla + 4 * GW],
             w[:, o_dn:o_dn + 4 * GW],
             w[:, o_ssm:o_ssm + 4 * GW],
             w[:, o_gla + 4 * GW:o_dn],
             w[:, o_dn + 4 * GW:o_ssm],
             w[:, o_ssm + 4 * GW:W_IN_COLS],
             jnp.zeros((w.shape[0], LANES - 28), F32)]
    o_ref[0] = jnp.concatenate(parts, axis=1).astype(MXU_DT)


def _wprep_call(w_in):
    n_layers = w_in.shape[0]
    tr = 128
    return pl.pallas_call(
        _wprep_body,
        grid=(n_layers, D_MODEL // tr),
        in_specs=[pl.BlockSpec((1, tr, W_IN_COLS), lambda l, i: (l, i, 0))],
        out_specs=pl.BlockSpec((1, tr, U_COLS), lambda l, i: (l, i, 0)),
        out_shape=jax.ShapeDtypeStruct((n_layers, D_MODEL, U_COLS), MXU_DT),
        compiler_params=pltpu.CompilerParams(vmem_limit_bytes=VMEM_LIMIT,
                                             dimension_semantics=("parallel", "parallel")),
        name="w_in_relayout",
    )(w_in)


def _rms(x):
    return x * lax.rsqrt(jnp.mean(x * x, -1, keepdims=True) + EPS)


def _inproj_body(x_ref, sh_ref, sc_ref, nw_ref, w_ref, o_rw, o_gla, o_dn, o_ssm, o_misc):
    bb, tt, _ = x_ref.shape
    h = _rms(x_ref[...]) * nw_ref[...]
    h = h * (1.0 + sc_ref[...]) + sh_ref[...]
    u = jnp.dot(h.reshape(bb * tt, D_MODEL).astype(MXU_DT), w_ref[...], preferred_element_type=F32)
    off = 0
    for ref in (o_rw, o_gla, o_dn, o_ssm, o_misc):
        w = ref.shape[-1]
        ref[...] = u[:, off:off + w].reshape(bb, tt, w)
        off += w


def _row_blocks(bsz, tlen):
    tt = min(tlen, ROW_TILE)
    bb = ROW_TILE // tt
    assert tlen % tt == 0 and bsz % bb == 0
    return bb, tt


def _inproj_call(x, mod, norm_w, w_r, layer):
    bsz, tlen, _ = x.shape
    bb, tt = _row_blocks(bsz, tlen)
    widths = (RWKV_COLS, 4 * GW, 4 * GW, 4 * GW, LANES)

    def xmap(i, j):
        return (i, j, 0)

    return pl.pallas_call(
        _inproj_body,
        grid=(bsz // bb, tlen // tt),
        in_specs=[pl.BlockSpec((bb, tt, D_MODEL), xmap),
                  pl.BlockSpec((bb, 1, D_MODEL), lambda i, j: (i, 0, 0)),
                  pl.BlockSpec((bb, 1, D_MODEL), lambda i, j: (i, 0, 1)),
                  pl.BlockSpec((1, 1, D_MODEL), lambda i, j: (0, 0, 0)),
                  pl.BlockSpec((None, D_MODEL, U_COLS), lambda i, j: (layer, 0, 0),
                               pipeline_mode=pl.Buffered(1))],
        out_specs=[pl.BlockSpec((bb, tt, w), xmap) for w in widths],
        out_shape=[jax.ShapeDtypeStruct((bsz, tlen, w), F32) for w in widths],
        compiler_params=pltpu.CompilerParams(vmem_limit_bytes=VMEM_LIMIT,
                                             dimension_semantics=("parallel", "parallel")),
        name="in_proj",
    )(x, mod, mod, norm_w.reshape(1, 1, D_MODEL), w_r)


def _outmlp_body(final, x_ref, y_ref, gt1_ref, sh_ref, sc_ref, gt2_ref, nw_ref, fnw_ref,
                 wo_ref, wu_ref, wd_ref, o_ref):
    bb, tt, _ = x_ref.shape
    rows = bb * tt
    att = jnp.dot(y_ref[...].reshape(rows, D_MODEL).astype(MXU_DT), wo_ref[...],
                  preferred_element_type=F32)
    x1 = x_ref[...] + gt1_ref[...] * att.reshape(bb, tt, D_MODEL)
    h = _rms(x1) * nw_ref[...]
    h = h * (1.0 + sc_ref[...]) + sh_ref[...]
    a = jnp.dot(h.reshape(rows, D_MODEL).astype(MXU_DT), wu_ref[...], preferred_element_type=F32)
    a = jnp.square(jnp.maximum(a, 0.0))
    f = jnp.dot(a.astype(MXU_DT), wd_ref[...], preferred_element_type=F32)
    x2 = x1 + gt2_ref[...] * f.reshape(bb, tt, D_MODEL)
    if final:
        x2 = _rms(x2) * fnw_ref[...]
    o_ref[...] = x2


def _outmlp_call(x, y_mix, mod, norm_w, final_norm_w, w_out, w_up, w_down, layer, final):
    bsz, tlen, _ = x.shape
    bb, tt = _row_blocks(bsz, tlen)

    def xmap(i, j):
        return (i, j, 0)

    def modspec(k):
        return pl.BlockSpec((bb, 1, D_MODEL), lambda i, j: (i, 0, k))

    def wspec(shape):
        return pl.BlockSpec((None,) + shape, lambda i, j: (layer, 0, 0), pipeline_mode=pl.Buffered(1))

    vec = pl.BlockSpec((1, 1, D_MODEL), lambda i, j: (0, 0, 0))
    return pl.pallas_call(
        functools.partial(_outmlp_body, final),
        grid=(bsz // bb, tlen // tt),
        in_specs=[pl.BlockSpec((bb, tt, D_MODEL), xmap), pl.BlockSpec((bb, tt, D_MODEL), xmap),
                  modspec(2), modspec(3), modspec(4), modspec(5), vec, vec,
                  wspec((D_MODEL, D_MODEL)), wspec((D_MODEL, D_FF)), wspec((D_FF, D_MODEL))],
        out_specs=pl.BlockSpec((bb, tt, D_MODEL), xmap),
        out_shape=jax.ShapeDtypeStruct((bsz, tlen, D_MODEL), F32),
        compiler_params=pltpu.CompilerParams(vmem_limit_bytes=VMEM_LIMIT,
                                             dimension_semantics=("parallel", "parallel")),
        name="out_mlp",
    )(x, y_mix, mod, mod, mod, mod, norm_w.reshape(1, 1, D_MODEL),
      final_norm_w.reshape(1, 1, D_MODEL), w_out, w_up, w_down)


(V_RW_W0, V_RW_A0, V_RW_KK, V_RW_KA, V_RW_RK, V_RW_LNW, V_RW_LNB,
 V_GLA_B, V_GLA_NW, V_DN_NW, V_SSM_D, V_SSM_NW) = range(12)
P_DN_ALOG, P_DN_DTB, P_SSM_ALOG, P_SSM_DTB = range(4)
CV_DN_W, CV_SSM_W, CV_SSM_B = 0, 4, 8
LR_RW_W2, LR_RW_A2, LR_RW_G2, LR_GLA_GK = range(4)


def _blockdiag(a, b):
    za = jnp.zeros_like(a)
    return jnp.concatenate([jnp.concatenate([a, za], axis=1),
                            jnp.concatenate([za, b], axis=1)], axis=0)


def _mixer_body(C, n_t, G, NS, n_prev, *refs):
    (u_rw_ref, u_gla_ref, u_dn_ref, u_ssm_ref, misc_ref,
     shift0_ref, wkv0_ref, gla0_ref, dnc0_ref, dn0_ref, ssc0_ref, ssm0_ref,
     mu_ref, v256_ref, v128_ref, conv_ref, lr_ref) = refs[:17]
    prev_refs = refs[17:17 + 7 * n_prev]
    y_ref = refs[17 + 7 * n_prev]
    out_refs = refs[18 + 7 * n_prev:25 + 7 * n_prev]
    xp_rw, xp_dn, xp_ss, s_wkv, s_gla, s_dn, s_ssm = refs[25 + 7 * n_prev:]
    shift1_ref, wkv1_ref, gla1_ref, dnc1_ref, dn1_ref, ssc1_ref, ssm1_ref = (r.at[n_prev] for r in out_refs)
    t = pl.program_id(1)

    @pl.when(t == 0)
    def _init():
        for g in range(G * NS):
            xp_rw[g] = shift0_ref[g]
            xp_dn[g] = dnc0_ref[g]
            xp_ss[g] = ssc0_ref[g]
            for p in range(2):
                s_wkv[g, p] = _blockdiag(wkv0_ref[g, 2 * p], wkv0_ref[g, 2 * p + 1])
                s_gla[g, p] = _blockdiag(gla0_ref[g, 2 * p], gla0_ref[g, 2 * p + 1])
                s_dn[g, p] = _blockdiag(dn0_ref[g, 2 * p], dn0_ref[g, 2 * p + 1])
                s_ssm[g, p] = jnp.concatenate([ssm0_ref[g, 2 * p], ssm0_ref[g, 2 * p + 1]], axis=0)

    ri = lax.broadcasted_iota(jnp.int32, (C, C), 0)
    ci = lax.broadcasted_iota(jnp.int32, (C, C), 1)
    SEG = C // NS
    lg = SEG.bit_length() - 1

    def same_seg(a, b):
        return lax.shift_right_logical(a, lg) == lax.shift_right_logical(b, lg)

    incl = (ci <= ri) & same_seg(ci, ri)
    ri2 = lax.broadcasted_iota(jnp.int32, (C, 2 * C), 0)
    ci2 = lax.broadcasted_iota(jnp.int32, (C, 2 * C), 1)
    left2 = ci2 < C
    cj2 = ci2 & (C - 1)
    incl2 = (cj2 <= ri2) & same_seg(cj2, ri2)
    strict2 = (cj2 < ri2) & same_seg(cj2, ri2)
    cj4 = lax.broadcasted_iota(jnp.int32, (C, 4 * C), 1) & (C - 1)
    ri4 = lax.broadcasted_iota(jnp.int32, (C, 4 * C), 0)
    incl4 = (cj4 <= ri4) & same_seg(cj4, ri4)
    tri = jnp.where(incl, 1.0, 0.0).astype(MXU_DT)
    lane = lax.broadcasted_iota(jnp.int32, (1, LANES), 1)
    hmask = (lane < HD, lane >= HD)
    r128 = lax.shift_right_logical(lax.broadcasted_iota(jnp.int32, (LANES, LANES), 0), 6)
    c128 = lax.shift_right_logical(lax.broadcasted_iota(jnp.int32, (LANES, LANES), 1), 6)
    bd = r128 == c128
    r256 = lax.broadcasted_iota(jnp.int32, (GW, GW), 0)
    c256 = lax.broadcasted_iota(jnp.int32, (GW, GW), 1)
    blk64 = jnp.where(lax.shift_right_logical(r256, 6) == lax.shift_right_logical(c256, 6),
                      1.0, 0.0).astype(MXU_DT)
    blk128 = jnp.where(lax.shift_right_logical(r256, 7) == lax.shift_right_logical(c256, 7),
                       1.0, 0.0).astype(MXU_DT)
    n_iter = 0
    while 2 * (1 << n_iter) < SEG:
        n_iter += 1

    def rows(*xs):
        return jnp.concatenate(xs, axis=0)

    def vrow(i):
        return v256_ref[i:i + 1, :]

    def load(ref, g):
        blk = ref[g * NS:(g + 1) * NS]
        return blk.reshape(C, blk.shape[-1])

    def store(ref, g, lo, val):
        w = val.shape[-1]
        ref[g * NS:(g + 1) * NS, :, lo:lo + w] = val.reshape(NS, SEG, w)

    def sq(x, q):
        return x[q * SEG:(q + 1) * SEG]

    def seg_last(x):
        if NS == 1:
            return x[C - 1:C]
        x3 = x.reshape(NS, SEG, x.shape[-1])
        return jnp.broadcast_to(x3[:, SEG - 1:SEG], x3.shape).reshape(x.shape)

    def last_row(x, q):
        return x[(q + 1) * SEG - 1:(q + 1) * SEG]

    def stack(parts):
        return parts[0] if NS == 1 else rows(*parts)

    def with_state(mm, lhs_list, ref, g, p):
        outs = [[] for _ in lhs_list]
        for q in range(NS):
            res = mm(rows(*[sq(a, q) for a in lhs_list]) if len(lhs_list) > 1 else sq(lhs_list[0], q),
                     ref[g * NS + q, p])
            for i in range(len(lhs_list)):
                outs[i].append(res[i * SEG:(i + 1) * SEG])
        return [stack(o) for o in outs]

    def update_state(ref, g, p, decay, lhs_list, rhs_list, masked):
        for q in range(NS):
            a = rows(*[sq(x, q) for x in lhs_list]) if len(lhs_list) > 1 else sq(lhs_list[0], q)
            b = rows(*[sq(x, q) for x in rhs_list]) if len(rhs_list) > 1 else sq(rhs_list[0], q)
            new = ref[g * NS + q, p] * decay(q) + _mm_tn(a, b)
            ref[g * NS + q, p] = jnp.where(bd, new, 0.0) if masked else new

    def hsum(x):
        return _mm_sel_rhs(x, blk64, n=1)

    def head(x, h):
        return jnp.where(hmask[h], x, 0.0)

    def hrows(x):
        return rows(head(x, 0), head(x, 1))

    def pair_nt(a, b):
        return _mm_nt(a, hrows(b))

    def pair_apply(q2, y):
        return _mm(q2, hrows(y))

    def bd2(q2):
        return rows(jnp.where(left2, q2, 0.0), jnp.where(left2, 0.0, q2))

    def seg2(tile, tile_t, lane0):
        col = jnp.where(left2, tile[:, lane0:lane0 + 1], tile[:, lane0 + 1:lane0 + 2])
        row = jnp.concatenate([tile_t[lane0:lane0 + 1, :], tile_t[lane0 + 1:lane0 + 2, :]], axis=1)
        return _seg_decay(col, row, incl2)

    def expand(tile, base):
        cols = [jnp.broadcast_to(tile[:, base + h:base + h + 1], (C, LANES)) for h in range(NH)]
        return [jnp.where(hmask[0], cols[0], cols[1]), jnp.where(hmask[0], cols[2], cols[3])]

    row8 = lax.broadcasted_iota(jnp.int32, (SUBLANES, 1), 0)

    def shifted(x, prev8, k):
        if NS > 1:
            x3 = x.reshape(NS, SEG, x.shape[-1])
            r3 = lax.broadcasted_iota(jnp.int32, (1, SEG, 1), 1)
            return jnp.where(r3 < k, pltpu.roll(prev8, k, 1), pltpu.roll(x3, k, 1)).reshape(x.shape)
        prev8 = prev8[0]
        top = jnp.where(row8 < k, pltpu.roll(prev8, k, 0), pltpu.roll(x, k, 0)[0:SUBLANES])
        if C == SUBLANES:
            return top
        return rows(top, pltpu.roll(x, k, 0)[SUBLANES:])

    def psl(p):
        return slice(LANES * p, LANES * (p + 1))

    shared = {}

    def neumann(n_list, steps):
        rs = list(n_list)
        ps = [_mm(p, bd2(p)) for p in n_list]
        yield None
        for _ in range(steps - 1):
            both = [_mm(rows(p, r), bd2(p)) for p, r in zip(ps, rs)]
            yield None
            rs = [r + p + b[C:2 * C] for r, p, b in zip(rs, ps, both)]
            ps = [b[0:C] for b in both]
        upd = [_mm(r, bd2(p)) for p, r in zip(ps, rs)]
        yield None
        yield [r + p + u for r, p, u in zip(rs, ps, upd)]

    def rwkv(g):
        u_rw = load(u_rw_ref, g)
        u_prev = shifted(u_rw, xp_rw[g * NS:(g + 1) * NS], 1)
        xp_rw[g * NS:(g + 1) * NS] = u_rw.reshape(NS, SEG, RWKV_COLS)[:, SEG - SUBLANES:SEG]
        xs = u_rw + (u_prev - u_rw) * mu_ref[...]
        r = xs[:, 0:GW]
        k = xs[:, GW:2 * GW]
        v = xs[:, 2 * GW:3 * GW]
        x7 = xs[:, 3 * GW:RWKV_COLS]
        yield
        w_pre = _mm(jnp.tanh(x7), lr_ref[LR_RW_W2])
        a_pre = _mm(x7, lr_ref[LR_RW_A2])
        gate = _mm(jax.nn.sigmoid(x7), lr_ref[LR_RW_G2])
        yield
        kk = k * vrow(V_RW_KK)
        kk_ss = hsum(kk * kk)
        yield
        log_w = -jax.nn.softplus(-(vrow(V_RW_W0) + w_pre)) - 0.5
        lw = -jnp.exp(log_w)
        yield
        a = jax.nn.sigmoid(vrow(V_RW_A0) + a_pre)
        yield
        kk = kk * lax.rsqrt(kk_ss + EPS)
        k2 = k * (1.0 + (a - 1.0) * vrow(V_RW_KA))
        bcum = _mm_sel_lhs(tri, lw)
        bonus = hsum(r * k2 * vrow(V_RW_RK))
        yield
        e_nb = jnp.exp(-bcum)
        blast = seg_last(bcum)
        e_rem = jnp.exp(blast - bcum)
        yield
        nka = -kk * a
        r_t = r * jnp.exp(bcum)
        yield
        a_t = kk * jnp.exp(bcum - lw)
        b_t = nka * e_nb
        yield
        k_t = k2 * e_nb
        k_rem = k2 * e_rem
        b_rem = nka * e_rem
        yield
        ars = [with_state(_mm_nt, [a_t[:, psl(p)], r_t[:, psl(p)]], s_wkv, g, p) for p in range(2)]
        ar = [_mm_nt(rows(a_t[:, psl(p)], r_t[:, psl(p)]), rows(hrows(b_t[:, psl(p)]), hrows(k_t[:, psl(p)])))
              for p in range(2)]
        yield
        n_ab = [jnp.where(strict2, ar[p][0:C, 0:2 * C], 0.0) for p in range(2)]
        n_ak = [jnp.where(strict2, ar[p][0:C, 2 * C:4 * C], 0.0) for p in range(2)]
        a_r = [jnp.where(incl4, ar[p][C:2 * C], 0.0) for p in range(2)]
        rhs = [ars[p][0] + pair_apply(n_ak[p], v[:, psl(p)]) for p in range(2)]
        tinv = None
        for step in neumann(n_ab, n_iter):
            if step is None:
                yield
            else:
                tinv = step
        yield
        uh = [rhs[p] + pair_apply(tinv[p], rhs[p]) for p in range(2)]
        yield
        y_pairs = []
        for p in range(2):
            sl = psl(p)
            vp = v[:, sl]
            u_p = uh[p]
            yp = ars[p][1] + _mm(a_r[p], rows(hrows(u_p), hrows(vp)))
            update_state(s_wkv, g, p, lambda q: jnp.exp(last_row(bcum, q)[:, sl]),
                         [vp, u_p], [k_rem[:, sl], b_rem[:, sl]], True)
            y_pairs.append(yp)
        yield
        y = jnp.concatenate(y_pairs, axis=1)
        mean = hsum(y) * (1.0 / HD)
        yield
        yc = y - mean
        var = hsum(yc * yc) * (1.0 / HD)
        yield
        y = yc * lax.rsqrt(var + RWKV_GN_EPS) * vrow(V_RW_LNW) + vrow(V_RW_LNB)
        store(y_ref, g, 0, (y + bonus * v) * gate)

    def gla(g):
        u_gla = load(u_gla_ref, g)
        q = u_gla[:, 0:GW] * (HD ** -0.5)
        k = u_gla[:, GW:2 * GW]
        v = u_gla[:, 2 * GW:3 * GW]
        gz = u_gla[:, 3 * GW:4 * GW]
        gate = _mm(load(misc_ref, g), lr_ref[LR_GLA_GK])
        yield
        la = jax.nn.log_sigmoid(gate + vrow(V_GLA_B)) * (1.0 / GLA_GATE_NORM)
        bcum = _mm_sel_lhs(tri, la)
        yield
        blast = seg_last(bcum)
        q_in = q * jnp.exp(bcum)
        yield
        k_out = k * jnp.exp(blast - bcum)
        qs = [with_state(_mm, [q_in[:, psl(p)]], s_gla, g, p)[0] for p in range(2)]
        a_blocks = [[], []]
        k_b = None
        ref_prev = None
        for lo in range(0, C, GLA_SUB):
            hi = min(lo + GLA_SUB, C)
            ref = bcum[lo - 1:lo, :] if lo else jnp.zeros((1, GW), F32)
            q_b = q[lo:hi] * jnp.exp(bcum[lo:hi] - ref)
            k_new = k[lo:hi] * jnp.exp(ref - bcum[lo:hi])
            if k_b is None:
                k_b = rows(k_new, jnp.zeros((C - hi, GW), F32)) if hi < C else k_new
            else:
                old = k_b[0:lo] * jnp.exp(ref - ref_prev)
                k_b = rows(old, k_new, jnp.zeros((C - hi, GW), F32)) if hi < C else rows(old, k_new)
            ref_prev = ref
            for p in range(2):
                a_blocks[p].append(pair_nt(q_b[:, psl(p)], k_b[:, psl(p)]))
            yield
        a_h = [rows(*a_blocks[p]) for p in range(2)]
        for p in range(2):
            sl = psl(p)
            def kdecay(q, sl=sl):
                row = jnp.exp(last_row(bcum, q)[:, sl])
                return jnp.broadcast_to(row, (SUBLANES, LANES)).T[:, 0:1]

            update_state(s_gla, g, p, kdecay, [k_out[:, sl]], [v[:, sl]], True)
        yield
        o_pairs = []
        for p in range(2):
            o_pairs.append(qs[p] + pair_apply(jnp.where(incl2, a_h[p], 0.0), v[:, psl(p)]))
        yield
        o = jnp.concatenate(o_pairs, axis=1)
        ss = hsum(o * o)
        yield
        o = o * lax.rsqrt(ss * (1.0 / HD) + EPS) * vrow(V_GLA_NW)
        store(y_ref, g, GW, o * _silu(gz))

    def dn(g):
        misc = load(misc_ref, g)
        g_dn = -jnp.exp(v128_ref[P_DN_ALOG:P_DN_ALOG + 1, :]) * jax.nn.softplus(
            misc + v128_ref[P_DN_DTB:P_DN_DTB + 1, :])
        dt_ss = jax.nn.softplus(misc + v128_ref[P_SSM_DTB:P_SSM_DTB + 1, :])
        la_ss = dt_ss * -jnp.exp(v128_ref[P_SSM_ALOG:P_SSM_ALOG + 1, :])
        beta = jax.nn.sigmoid(misc)
        cum = _mm_sel_lhs(tri, jnp.where(lane < MISC_DN_B, g_dn, la_ss))
        u_dn = load(u_dn_ref, g)
        raw = u_dn[:, 0:CONV_COLS]
        prev8 = xp_dn[g * NS:(g + 1) * NS]
        qkv = raw * conv_ref[CV_DN_W + 3:CV_DN_W + 4, :]
        for k in (1, 2, 3):
            qkv = qkv + shifted(raw, prev8, k) * conv_ref[CV_DN_W + 3 - k:CV_DN_W + 4 - k, :]
        xp_dn[g * NS:(g + 1) * NS] = raw.reshape(NS, SEG, CONV_COLS)[:, SEG - SUBLANES:SEG]
        yield
        qkv = _silu(qkv)
        q = qkv[:, 0:GW]
        k = qkv[:, GW:2 * GW]
        v = qkv[:, 2 * GW:3 * GW]
        z = u_dn[:, CONV_COLS:CONV_COLS + GW]
        q_ss = hsum(q * q)
        k_ss = hsum(k * k)
        yield
        cum_t = cum.T
        shared[g] = (cum, cum_t, dt_ss)
        q = q * lax.rsqrt(q_ss + EPS) * (HD ** -0.5)
        k = k * lax.rsqrt(k_ss + EPS)
        yield
        beta_x = expand(beta, MISC_DN_B)
        cum_x = expand(cum, MISC_DN_A)
        yield
        ecx, kb, vb, kbe, qs, kq = [], [], [], [], [], []
        for p in range(2):
            sl = psl(p)
            ecx.append(jnp.exp(cum_x[p]))
            kb.append(k[:, sl] * beta_x[p])
            vb.append(v[:, sl] * beta_x[p])
            kbe.append(kb[p] * ecx[p])
            yield
            qs.append(with_state(_mm, [q[:, sl] * ecx[p]], s_dn, g, p)[0])
            kq.append(pair_nt(rows(kb[p], q[:, sl]), k[:, sl]))
        yield
        n_m, a_l = [], []
        for p in range(2):
            dec = seg2(cum, cum_t, MISC_DN_A + 2 * p)
            n_m.append(jnp.where(strict2, -(kq[p][0:C] * dec), 0.0))
            a_l.append(kq[p][C:2 * C] * dec)
        tinv = None
        for step in neumann(n_m, n_iter - 1):
            if step is None:
                yield
            else:
                tinv = step
        yield
        err = [n_m[p] - tinv[p] + _mm3(n_m[p], bd2(tinv[p])) for p in range(2)]
        yield
        tinv = [tinv[p] + err[p] + _mm(tinv[p], bd2(err[p])) for p in range(2)]
        yield
        vk = [jnp.concatenate([vb[p], kbe[p]], axis=1) for p in range(2)]
        uw = [vk[p] + _mm(tinv[p], jnp.concatenate([hrows(vb[p]), hrows(kbe[p])], axis=1)) for p in range(2)]
        yield
        u_p = [uw[p][:, 0:LANES] for p in range(2)]
        w_p = [uw[p][:, LANES:2 * LANES] for p in range(2)]
        ws = [with_state(_mm, [w_p[p]], s_dn, g, p)[0] for p in range(2)]
        yield
        o_pairs = []
        for p in range(2):
            sl = psl(p)
            cx = cum_x[p]
            last = seg_last(cx)
            v_new = u_p[p] - ws[p]
            op = qs[p] + pair_apply(a_l[p], v_new)
            update_state(s_dn, g, p, lambda q: jnp.exp(last_row(cx, q)),
                         [k[:, sl] * jnp.exp(last - cx)], [v_new], True)
            o_pairs.append(op)
        yield
        o = jnp.concatenate(o_pairs, axis=1)
        ss = hsum(o * o)
        yield
        o = o * lax.rsqrt(ss * (1.0 / HD) + EPS) * vrow(V_DN_NW)
        store(y_ref, g, 2 * GW, o * _silu(z))

    def ssd(g):
        u_ssm = load(u_ssm_ref, g)
        z = u_ssm[:, 0:GW]
        raw = u_ssm[:, GW:GW + CONV_COLS]
        prev8 = xp_ss[g * NS:(g + 1) * NS]
        xbc = conv_ref[CV_SSM_B:CV_SSM_B + 1, :] + raw * conv_ref[CV_SSM_W + 3:CV_SSM_W + 4, :]
        for k in (1, 2, 3):
            xbc = xbc + shifted(raw, prev8, k) * conv_ref[CV_SSM_W + 3 - k:CV_SSM_W + 4 - k, :]
        xp_ss[g * NS:(g + 1) * NS] = raw.reshape(NS, SEG, CONV_COLS)[:, SEG - SUBLANES:SEG]
        yield
        xbc = _silu(xbc)
        yield
        xs_ = xbc[:, 0:GW]
        bm = xbc[:, GW:2 * GW]
        cm = xbc[:, 2 * GW:3 * GW]
        gmat = [_mm_nt(cm[:, psl(p)], rows(bm[:, psl(p)], bm[:, psl(p)])) for p in range(2)]
        cs = [with_state(_mm_nt, [cm[:, psl(p)]], s_ssm, g, p)[0] for p in range(2)]
        yield
        while g not in shared:
            yield
        cum, cum_t, dt_ss = shared[g]
        dt_x = expand(dt_ss, MISC_SSM_DT)
        cum_x = expand(cum, MISC_SSM_DT)
        yield
        y_pairs = []
        for p in range(2):
            sl = psl(p)
            xh = xs_[:, sl]
            xdt = xh * dt_x[p]
            cx = cum_x[p]
            last = seg_last(cx)
            yp = jnp.exp(cx) * cs[p] + v256_ref[V_SSM_D:V_SSM_D + 1, sl] * xh
            yp = yp + pair_apply(gmat[p] * seg2(cum, cum_t, MISC_SSM_DT + 2 * p), xdt)
            lane0 = MISC_SSM_DT + 2 * p

            def dcol(q, lane0=lane0):
                lr = last_row(cum, q)
                return rows(jnp.broadcast_to(jnp.exp(lr[:, lane0:lane0 + 1]), (HD, SSM_STATE)),
                            jnp.broadcast_to(jnp.exp(lr[:, lane0 + 1:lane0 + 2]), (HD, SSM_STATE)))

            update_state(s_ssm, g, p, dcol, [xdt * jnp.exp(last - cx)], [bm[:, sl]], False)
            y_pairs.append(yp)
            yield
        yield
        y = jnp.concatenate(y_pairs, axis=1) * _silu(z)
        ss = _mm_sel_rhs(y * y, blk128)
        yield
        store(y_ref, g, 3 * GW, y * lax.rsqrt(ss * (1.0 / (2 * HD)) + EPS) * vrow(V_SSM_NW))

    active = []
    for g in range(G):
        active += [rwkv(g), dn(g), gla(g), ssd(g)]
    while active:
        alive = []
        for gen in active:
            try:
                next(gen)
                alive.append(gen)
            except StopIteration:
                pass
        active = alive

    @pl.when(t == n_t - 1)
    def _fin():
        for j in range(n_prev):
            for i in range(7):
                out_refs[i][j] = prev_refs[7 * j + i][...]
        for g in range(G * NS):
            shift1_ref[g] = xp_rw[g]
            dnc1_ref[g] = xp_dn[g]
            ssc1_ref[g] = xp_ss[g]
            for p in range(2):
                sw = s_wkv[g, p]
                sg = s_gla[g, p]
                sd = s_dn[g, p]
                ss = s_ssm[g, p]
                for h in range(2):
                    hs = slice(HD * h, HD * (h + 1))
                    wkv1_ref[g, 2 * p + h] = sw[hs, hs]
                    gla1_ref[g, 2 * p + h] = sg[hs, hs]
                    dn1_ref[g, 2 * p + h] = sd[hs, hs]
                    ssm1_ref[g, 2 * p + h] = ss[hs, :]


def _mixer_call(u_parts, states, mparams, layer, prev=()):
    u_rw, u_gla, u_dn, u_ssm, misc = u_parts
    shift0, wkv0, gla0, dnc0, dn0, ssc0, ssm0 = states
    bsz, tlen, _ = u_rw.shape
    tok = math.gcd(tlen, PROMPT_CHUNK)
    n_t = tlen // tok
    if n_t > 1:
        NS, R = 1, SEQS_PER_STEP_LONG
    else:
        NS = max(1, min(PROMPT_CHUNK // tok, SEQS_PER_STEP_SHORT))
        R = SEQS_PER_STEP_SHORT // NS * (1 if prev else 2)
    C = NS * tok
    G = R * NS
    assert bsz % G == 0 and tok & (tok - 1) == 0 and tok % SUBLANES == 0

    def tmap(b, t):
        return (b, t, 0)

    def bmap3(b, t):
        return (b, 0, 0)

    def bmap4(b, t):
        return (b, 0, 0, 0)

    def full(arr):
        nd = arr.ndim
        return pl.BlockSpec(arr.shape, lambda b, t: (0,) * nd)

    def lmap3(b, t):
        return (layer, b, 0, 0)

    def lmap4(b, t):
        return (layer, b, 0, 0, 0)

    sq = (G, NH, HD, HD)
    in_state_specs = [pl.BlockSpec((None, G, SUBLANES, RWKV_COLS), lmap3), pl.BlockSpec((None,) + sq, lmap4),
                      pl.BlockSpec((None,) + sq, lmap4), pl.BlockSpec((None, G, SUBLANES, CONV_COLS), lmap3),
                      pl.BlockSpec((None,) + sq, lmap4), pl.BlockSpec((None, G, SUBLANES, CONV_COLS), lmap3),
                      pl.BlockSpec((None, G, NH, HD, SSM_STATE), lmap4)]
    n_prev = len(prev)
    n_out = n_prev + 1
    tails = [(SUBLANES, RWKV_COLS), (NH, HD, HD), (NH, HD, HD), (SUBLANES, CONV_COLS), (NH, HD, HD),
             (SUBLANES, CONV_COLS), (NH, HD, SSM_STATE)]
    prev_specs = [pl.BlockSpec((G,) + tl, bmap3 if len(tl) == 2 else bmap4) for tl in tails] * n_prev
    state_specs = [pl.BlockSpec((n_out, G) + tl, (lambda b, t: (0, b, 0, 0)) if len(tl) == 2
                                else (lambda b, t: (0, b, 0, 0, 0))) for tl in tails]
    state_shapes = [jax.ShapeDtypeStruct((n_out, bsz) + tl, F32) for tl in tails]
    outs = pl.pallas_call(
        functools.partial(_mixer_body, C, n_t, R, NS, n_prev),
        grid=(bsz // G, n_t),
        in_specs=[pl.BlockSpec((G, tok, RWKV_COLS), tmap), pl.BlockSpec((G, tok, 4 * GW), tmap),
                  pl.BlockSpec((G, tok, 4 * GW), tmap), pl.BlockSpec((G, tok, 4 * GW), tmap),
                  pl.BlockSpec((G, tok, LANES), tmap)] + in_state_specs + [full(a) for a in mparams] + prev_specs,
        out_specs=[pl.BlockSpec((G, tok, D_MODEL), tmap)] + state_specs,
        out_shape=[jax.ShapeDtypeStruct((bsz, tlen, D_MODEL), F32)] + state_shapes,
        scratch_shapes=[pltpu.VMEM((G, SUBLANES, RWKV_COLS), F32),
                        pltpu.VMEM((G, SUBLANES, CONV_COLS), F32),
                        pltpu.VMEM((G, SUBLANES, CONV_COLS), F32),
                        pltpu.VMEM((G, 2, LANES, LANES), F32), pltpu.VMEM((G, 2, LANES, LANES), F32),
                        pltpu.VMEM((G, 2, LANES, LANES), F32), pltpu.VMEM((G, 2, LANES, SSM_STATE), F32)],
        compiler_params=pltpu.CompilerParams(vmem_limit_bytes=VMEM_LIMIT,
                                             dimension_semantics=("parallel", "arbitrary")),
        name="mixers",
    )(u_rw, u_gla, u_dn, u_ssm, misc, shift0, wkv0, gla0, dnc0, dn0, ssc0, ssm0, *mparams,
      *[a for layer_states in prev for a in layer_states])
    return outs[0], tuple(outs[1:])


def _pad_rows(m, lo, total):
    return jnp.pad(m, ((lo, total - lo - m.shape[0]), (0, 0)))


def _lane_vec(vals, lo):
    return jnp.pad(vals, (lo, LANES - lo - vals.shape[0]))


def _pack_layer(P, l):
    v256 = jnp.stack([P['rwkv_w0'][l], P['rwkv_a0'][l], P['rwkv_k_k'][l], P['rwkv_k_a'][l],
                      P['rwkv_r_k'][l], P['rwkv_ln_w'][l], P['rwkv_ln_b'][l],
                      P['gla_gk_b'][l], P['gla_norm_w'][l], P['dn_norm_w'][l],
                      jnp.repeat(P['ssm_D'][l], HD), P['ssm_norm_w'][l]])
    v256 = jnp.pad(v256, ((0, 16 - v256.shape[0]), (0, 0)))
    v128 = jnp.stack([_lane_vec(P['dn_A_log'][l], MISC_DN_A), _lane_vec(P['dn_dt_bias'][l], MISC_DN_A),
                      _lane_vec(P['ssm_A_log'][l], MISC_SSM_DT), _lane_vec(P['ssm_dt_bias'][l], MISC_SSM_DT)])
    v128 = jnp.pad(v128, ((0, 4), (0, 0)))
    conv = jnp.concatenate([P['dn_conv_w'][l], P['ssm_conv_w'][l], P['ssm_conv_b'][l][None],
                            jnp.zeros((7, CONV_COLS), F32)], axis=0)
    lr = jnp.stack([_pad_rows(P['rwkv_w2'][l], 0, LANES), _pad_rows(P['rwkv_a2'][l], 32, LANES),
                    _pad_rows(P['rwkv_g2'][l], 64, LANES),
                    _pad_rows(P['gla_gk_w2'][l], MISC_GLA_GATE, LANES)]).astype(MXU_DT)
    return (P['rwkv_mu'][l][None], v256, v128, conv, lr)


def _pad_tail_rows(a):
    return jnp.pad(a, ((0, 0), (0, 0), (SUBLANES - a.shape[2], 0), (0, 0)))


def _trunk(x, mod_rows, states, mixer_params, dense, final_norm_w):
    n_layers = len(mixer_params)
    shift0, wkv0, gla0, dnc0, dn0, ssc0, ssm0 = states
    st_in = (_pad_tail_rows(shift0[:, :, None, :]), wkv0, gla0, _pad_tail_rows(dnc0), dn0,
             _pad_tail_rows(ssc0), ssm0)
    prev = []
    for l in range(n_layers):
        mod = mod_rows[l][:, None, :]
        u_parts = _inproj_call(x, mod, dense['norm1'][l], dense['w_r'], l)
        last = l == n_layers - 1
        y_mix, st = _mixer_call(u_parts, st_in, mixer_params[l], l, prev=tuple(prev) if last else ())
        x = _outmlp_call(x, y_mix, mod, dense['norm2'][l], final_norm_w, dense['w_out'],
                         dense['w_up'], dense['w_down'], l, final=last)
        if not last:
            prev.append(tuple(s[0] for s in st))
    shift1, wkv1, gla1, dnc1, dn1, ssc1, ssm1 = st
    return x, (shift1[:, :, SUBLANES - 1], wkv1, gla1, dnc1[:, :, SUBLANES - 3:], dn1,
               ssc1[:, :, SUBLANES - 3:], ssm1)


def kernel(x_prompt, x_sample, state_rwkv_shift, state_rwkv_wkv, state_gla, state_dn_conv,
           state_dn, state_ssm_conv, state_ssm, c_prompt, c_sample,
           ada_w, ada_b, norm1_w, norm2_w, w_in, w_out, w_up, w_down,
           rwkv_mu, rwkv_w0, rwkv_w2, rwkv_a0, rwkv_a2, rwkv_g2, rwkv_k_k, rwkv_k_a, rwkv_r_k,
           rwkv_ln_w, rwkv_ln_b, gla_gk_w2, gla_gk_b, gla_norm_w,
           dn_conv_w, dn_A_log, dn_dt_bias, dn_norm_w,
           ssm_conv_w, ssm_conv_b, ssm_dt_bias, ssm_A_log, ssm_D, ssm_norm_w, final_norm_w):
    P = dict(rwkv_mu=rwkv_mu, rwkv_w0=rwkv_w0, rwkv_w2=rwkv_w2, rwkv_a0=rwkv_a0, rwkv_a2=rwkv_a2,
             rwkv_g2=rwkv_g2, rwkv_k_k=rwkv_k_k, rwkv_k_a=rwkv_k_a, rwkv_r_k=rwkv_r_k,
             rwkv_ln_w=rwkv_ln_w, rwkv_ln_b=rwkv_ln_b, gla_gk_w2=gla_gk_w2, gla_gk_b=gla_gk_b,
             gla_norm_w=gla_norm_w, dn_conv_w=dn_conv_w, dn_A_log=dn_A_log, dn_dt_bias=dn_dt_bias,
             dn_norm_w=dn_norm_w, ssm_conv_w=ssm_conv_w, ssm_conv_b=ssm_conv_b,
             ssm_dt_bias=ssm_dt_bias, ssm_A_log=ssm_A_log, ssm_D=ssm_D, ssm_norm_w=ssm_norm_w)
    n_layers = w_in.shape[0]
    n_prompt = x_prompt.shape[0]
    mixer_params = [_pack_layer(P, l) for l in range(n_layers)]
    dense = dict(w_r=_wprep_call(w_in), w_out=w_out.astype(MXU_DT), w_up=w_up.astype(MXU_DT),
                 w_down=w_down.astype(MXU_DT), norm1=norm1_w, norm2=norm2_w)
    mod_all = _ada_call(jnp.concatenate([c_prompt, c_sample], axis=0), ada_w, ada_b)
    sample_states = (state_rwkv_shift, state_rwkv_wkv, state_gla, state_dn_conv,
                     state_dn, state_ssm_conv, state_ssm)
    prompt_states = tuple(jnp.zeros((n_layers, n_prompt) + s.shape[2:], F32) for s in sample_states)
    y_prompt, ps = _trunk(x_prompt, mod_all[:, :n_prompt], prompt_states, mixer_params, dense, final_norm_w)
    y_sample, ss = _trunk(x_sample, mod_all[:, n_prompt:], sample_states, mixer_params, dense, final_norm_w)
    return (y_prompt, y_sample) + ps + ss
```

```python
import functools
import math

import jax
import jax.numpy as jnp
from jax import lax
from jax.experimental import pallas as pl
from jax.experimental.pallas import tpu as pltpu

F32 = jnp.float32
MXU_DT = jnp.bfloat16

D_MODEL = 1024
NH = 4
HD = 64
GW = NH * HD
D_FF = 4 * D_MODEL
SSM_STATE = 128
EPS = 1e-6
RWKV_GN_EPS = 64e-5
GLA_GATE_NORM = 16.0
GLA_SUB = 16
RWKV_COLS = 3 * GW + 32 + 32 + 64
CONV_COLS = 3 * GW
U_COLS = RWKV_COLS + 3 * 4 * GW + 128
MISC_GLA_GATE = 0
MISC_DN_A = 16
MISC_DN_B = 20
MISC_SSM_DT = 24

LANES = 128
SUBLANES = 8
VMEM_LIMIT = 56 * 1024 * 1024
PROMPT_CHUNK = 64
ROW_TILE = 512
SEQS_PER_STEP_LONG = 4
SEQS_PER_STEP_SHORT = 8


def _mm(a, b):
    return jnp.dot(a.astype(MXU_DT), b.astype(MXU_DT), preferred_element_type=F32)


def _mm_nt(a, b):
    return lax.dot_general(a.astype(MXU_DT), b.astype(MXU_DT), (((1,), (1,)), ((), ())),
                           preferred_element_type=F32)


def _mm_tn(a, b):
    return lax.dot_general(a.astype(MXU_DT), b.astype(MXU_DT), (((0,), (0,)), ((), ())),
                           preferred_element_type=F32)


def _mm3(a, b):
    a_hi, a_lo = _split(a, 2)
    b_hi, b_lo = _split(b, 2)
    return (jnp.dot(a_hi, b_hi, preferred_element_type=F32) + jnp.dot(a_hi, b_lo, preferred_element_type=F32)
            + jnp.dot(a_lo, b_hi, preferred_element_type=F32))


def _split(x, n):
    parts = []
    r = x
    for i in range(n):
        p = r.astype(MXU_DT)
        parts.append(p)
        if i + 1 < n:
            r = r - p.astype(F32)
    return parts


def _mm_sel_lhs(sel, x, n=2):
    acc = None
    for p in _split(x, n):
        d = jnp.dot(sel, p, preferred_element_type=F32)
        acc = d if acc is None else acc + d
    return acc


def _mm_sel_rhs(x, sel, n=2):
    acc = None
    for p in _split(x, n):
        d = jnp.dot(p, sel, preferred_element_type=F32)
        acc = d if acc is None else acc + d
    return acc


def _seg_decay(cum_col, cum_row, incl):
    d = cum_col - cum_row
    return jnp.where(incl, jnp.exp(jnp.where(incl, d, 0.0)), 0.0)


def _silu(x):
    return x * jax.nn.sigmoid(x)


def _ada_body(c_ref, w_ref, b_ref, o_ref):
    c = c_ref[...]
    o_ref[0] = _mm(_silu(c), w_ref[0]) + b_ref[0]


def _ada_call(c_all, ada_w, ada_b):
    n_layers = ada_w.shape[0]
    rows = c_all.shape[0]
    tn = 1536
    return pl.pallas_call(
        _ada_body,
        grid=(n_layers, 6 * D_MODEL // tn),
        in_specs=[pl.BlockSpec((rows, D_MODEL), lambda l, j: (0, 0)),
                  pl.BlockSpec((1, D_MODEL, tn), lambda l, j: (l, 0, j)),
                  pl.BlockSpec((1, 1, tn), lambda l, j: (l, 0, j))],
        out_specs=pl.BlockSpec((1, rows, tn), lambda l, j: (l, 0, j)),
        out_shape=jax.ShapeDtypeStruct((n_layers, rows, 6 * D_MODEL), F32),
        compiler_params=pltpu.CompilerParams(vmem_limit_bytes=VMEM_LIMIT),
        name="ada_mod",
    )(c_all, ada_w, ada_b.reshape(n_layers, 1, 6 * D_MODEL))


W_IN_COLS = RWKV_COLS + (4 * GW + 16) + (4 * GW + 8) + (4 * GW + 4)


def _wprep_body(w_ref, o_ref):
    w = w_ref[0]
    o_gla = RWKV_COLS
    o_dn = o_gla + 4 * GW + 16
    o_ssm = o_dn + 4 * GW + 8
    parts = [w[:, 0:o_gla + 4 * GW],
             w[:, o_dn:o_dn + 4 * GW],
             w[:, o_ssm:o_ssm + 4 * GW],
             w[:, o_gla + 4 * GW:o_dn],
             w[:, o_dn + 4 * GW:o_ssm],
             w[:, o_ssm + 4 * GW:W_IN_COLS],
             jnp.zeros((w.shape[0], LANES - 28), F32)]
    o_ref[0] = jnp.concatenate(parts, axis=1).astype(MXU_DT)


def _wprep_call(w_in):
    n_layers = w_in.shape[0]
    tr = 128
    return pl.pallas_call(
        _wprep_body,
        grid=(n_layers, D_MODEL // tr),
        in_specs=[pl.BlockSpec((1, tr, W_IN_COLS), lambda l, i: (l, i, 0))],
        out_specs=pl.BlockSpec((1, tr, U_COLS), lambda l, i: (l, i, 0)),
        out_shape=jax.ShapeDtypeStruct((n_layers, D_MODEL, U_COLS), MXU_DT),
        compiler_params=pltpu.CompilerParams(vmem_limit_bytes=VMEM_LIMIT,
                                             dimension_semantics=("parallel", "parallel")),
        name="w_in_relayout",
    )(w_in)


def _rms(x):
    return x * lax.rsqrt(jnp.mean(x * x, -1, keepdims=True) + EPS)


def _inproj_body(x_ref, sh_ref, sc_ref, nw_ref, w_ref, o_rw, o_gla, o_dn, o_ssm, o_misc):
    bb, tt, _ = x_ref.shape
    h = _rms(x_ref[...]) * nw_ref[...]
    h = h * (1.0 + sc_ref[...]) + sh_ref[...]
    u = jnp.dot(h.reshape(bb * tt, D_MODEL).astype(MXU_DT), w_ref[...], preferred_element_type=F32)
    off = 0
    for ref in (o_rw, o_gla, o_dn, o_ssm, o_misc):
        w = ref.shape[-1]
        ref[...] = u[:, off:off + w].reshape(bb, tt, w)
        off += w


def _row_blocks(bsz, tlen):
    tt = min(tlen, ROW_TILE)
    bb = (ROW_TILE if tlen >= ROW_TILE else ROW_TILE // 2) // tt
    assert tlen % tt == 0 and bsz % bb == 0
    return bb, tt


def _inproj_call(x, mod, norm_w, w_r, layer):
    bsz, tlen, _ = x.shape
    bb, tt = _row_blocks(bsz, tlen)
    widths = (RWKV_COLS, 4 * GW, 4 * GW, 4 * GW, LANES)

    def xmap(i, j):
        return (i, j, 0)

    return pl.pallas_call(
        _inproj_body,
        grid=(bsz // bb, tlen // tt),
        in_specs=[pl.BlockSpec((bb, tt, D_MODEL), xmap),
                  pl.BlockSpec((bb, 1, D_MODEL), lambda i, j: (i, 0, 0)),
                  pl.BlockSpec((bb, 1, D_MODEL), lambda i, j: (i, 0, 1)),
                  pl.BlockSpec((1, 1, D_MODEL), lambda i, j: (0, 0, 0)),
                  pl.BlockSpec((None, D_MODEL, U_COLS), lambda i, j: (layer, 0, 0),
                               pipeline_mode=pl.Buffered(1))],
        out_specs=[pl.BlockSpec((bb, tt, w), xmap) for w in widths],
        out_shape=[jax.ShapeDtypeStruct((bsz, tlen, w), F32) for w in widths],
        compiler_params=pltpu.CompilerParams(vmem_limit_bytes=VMEM_LIMIT,
                                             dimension_semantics=("parallel", "parallel")),
        name="in_proj",
    )(x, mod, mod, norm_w.reshape(1, 1, D_MODEL), w_r)


def _outmlp_body(final, x_ref, y_ref, gt1_ref, sh_ref, sc_ref, gt2_ref, nw_ref, fnw_ref,
                 wo_ref, wu_ref, wd_ref, o_ref):
    bb, tt, _ = x_ref.shape
    rows = bb * tt
    att = jnp.dot(y_ref[...].reshape(rows, D_MODEL).astype(MXU_DT), wo_ref[...],
                  preferred_element_type=F32)
    x1 = x_ref[...] + gt1_ref[...] * att.reshape(bb, tt, D_MODEL)
    h = _rms(x1) * nw_ref[...]
    h = h * (1.0 + sc_ref[...]) + sh_ref[...]
    a = jnp.dot(h.reshape(rows, D_MODEL).astype(MXU_DT), wu_ref[...], preferred_element_type=F32)
    a = jnp.square(jnp.maximum(a, 0.0))
    f = jnp.dot(a.astype(MXU_DT), wd_ref[...], preferred_element_type=F32)
    x2 = x1 + gt2_ref[...] * f.reshape(bb, tt, D_MODEL)
    if final:
        x2 = _rms(x2) * fnw_ref[...]
    o_ref[...] = x2


def _outmlp_call(x, y_mix, mod, norm_w, final_norm_w, w_out, w_up, w_down, layer, final):
    bsz, tlen, _ = x.shape
    bb, tt = _row_blocks(bsz, tlen)

    def xmap(i, j):
        return (i, j, 0)

    def modspec(k):
        return pl.BlockSpec((bb, 1, D_MODEL), lambda i, j: (i, 0, k))

    def wspec(shape):
        return pl.BlockSpec((None,) + shape, lambda i, j: (layer, 0, 0), pipeline_mode=pl.Buffered(1))

    vec = pl.BlockSpec((1, 1, D_MODEL), lambda i, j: (0, 0, 0))
    return pl.pallas_call(
        functools.partial(_outmlp_body, final),
        grid=(bsz // bb, tlen // tt),
        in_specs=[pl.BlockSpec((bb, tt, D_MODEL), xmap), pl.BlockSpec((bb, tt, D_MODEL), xmap),
                  modspec(2), modspec(3), modspec(4), modspec(5), vec, vec,
                  wspec((D_MODEL, D_MODEL)), wspec((D_MODEL, D_FF)), wspec((D_FF, D_MODEL))],
        out_specs=pl.BlockSpec((bb, tt, D_MODEL), xmap),
        out_shape=jax.ShapeDtypeStruct((bsz, tlen, D_MODEL), F32),
        compiler_params=pltpu.CompilerParams(vmem_limit_bytes=VMEM_LIMIT,
                                             dimension_semantics=("parallel", "parallel")),
        name="out_mlp",
    )(x, y_mix, mod, mod, mod, mod, norm_w.reshape(1, 1, D_MODEL),
      final_norm_w.reshape(1, 1, D_MODEL), w_out, w_up, w_down)


(V_RW_W0, V_RW_A0, V_RW_KK, V_RW_KA, V_RW_RK, V_RW_LNW, V_RW_LNB,
 V_GLA_B, V_GLA_NW, V_DN_NW, V_SSM_D, V_SSM_NW) = range(12)
P_DN_ALOG, P_DN_DTB, P_SSM_ALOG, P_SSM_DTB = range(4)
CV_DN_W, CV_SSM_W, CV_SSM_B = 0, 4, 8
LR_RW_W2, LR_RW_A2, LR_RW_G2, LR_GLA_GK = range(4)


def _blockdiag(a, b):
    za = jnp.zeros_like(a)
    return jnp.concatenate([jnp.concatenate([a, za], axis=1),
                            jnp.concatenate([za, b], axis=1)], axis=0)


def _mixer_body(C, n_t, G, NS, n_prev, *refs):
    (u_rw_ref, u_gla_ref, u_dn_ref, u_ssm_ref, misc_ref,
     shift0_ref, wkv0_ref, gla0_ref, dnc0_ref, dn0_ref, ssc0_ref, ssm0_ref,
     mu_ref, v256_ref, v128_ref, conv_ref, lr_ref) = refs[:17]
    prev_refs = refs[17:17 + 7 * n_prev]
    y_ref = refs[17 + 7 * n_prev]
    out_refs = refs[18 + 7 * n_prev:25 + 7 * n_prev]
    xp_rw, xp_dn, xp_ss, s_wkv, s_gla, s_dn, s_ssm = refs[25 + 7 * n_prev:]
    shift1_ref, wkv1_ref, gla1_ref, dnc1_ref, dn1_ref, ssc1_ref, ssm1_ref = (r.at[n_prev] for r in out_refs)
    t = pl.program_id(1)

    @pl.when(t == 0)
    def _init():
        for g in range(G * NS):
            xp_rw[g] = shift0_ref[g]
            xp_dn[g] = dnc0_ref[g]
            xp_ss[g] = ssc0_ref[g]
            for p in range(2):
                s_wkv[g, p] = _blockdiag(wkv0_ref[g, 2 * p], wkv0_ref[g, 2 * p + 1])
                s_gla[g, p] = _blockdiag(gla0_ref[g, 2 * p], gla0_ref[g, 2 * p + 1])
                s_dn[g, p] = _blockdiag(dn0_ref[g, 2 * p], dn0_ref[g, 2 * p + 1])
                s_ssm[g, p] = jnp.concatenate([ssm0_ref[g, 2 * p], ssm0_ref[g, 2 * p + 1]], axis=0)

    ri = lax.broadcasted_iota(jnp.int32, (C, C), 0)
    ci = lax.broadcasted_iota(jnp.int32, (C, C), 1)
    SEG = C // NS
    lg = SEG.bit_length() - 1

    def same_seg(a, b):
        return lax.shift_right_logical(a, lg) == lax.shift_right_logical(b, lg)

    incl = (ci <= ri) & same_seg(ci, ri)
    ri2 = lax.broadcasted_iota(jnp.int32, (C, 2 * C), 0)
    ci2 = lax.broadcasted_iota(jnp.int32, (C, 2 * C), 1)
    left2 = ci2 < C
    cj2 = ci2 & (C - 1)
    incl2 = (cj2 <= ri2) & same_seg(cj2, ri2)
    strict2 = (cj2 < ri2) & same_seg(cj2, ri2)
    cj4 = lax.broadcasted_iota(jnp.int32, (C, 4 * C), 1) & (C - 1)
    ri4 = lax.broadcasted_iota(jnp.int32, (C, 4 * C), 0)
    incl4 = (cj4 <= ri4) & same_seg(cj4, ri4)
    tri = jnp.where(incl, 1.0, 0.0).astype(MXU_DT)
    lane = lax.broadcasted_iota(jnp.int32, (1, LANES), 1)
    hmask = (lane < HD, lane >= HD)
    r128 = lax.shift_right_logical(lax.broadcasted_iota(jnp.int32, (LANES, LANES), 0), 6)
    c128 = lax.shift_right_logical(lax.broadcasted_iota(jnp.int32, (LANES, LANES), 1), 6)
    bd = r128 == c128
    r256 = lax.broadcasted_iota(jnp.int32, (GW, GW), 0)
    c256 = lax.broadcasted_iota(jnp.int32, (GW, GW), 1)
    blk64 = jnp.where(lax.shift_right_logical(r256, 6) == lax.shift_right_logical(c256, 6),
                      1.0, 0.0).astype(MXU_DT)
    blk128 = jnp.where(lax.shift_right_logical(r256, 7) == lax.shift_right_logical(c256, 7),
                       1.0, 0.0).astype(MXU_DT)
    n_iter = 0
    while 2 * (1 << n_iter) < SEG:
        n_iter += 1

    def rows(*xs):
        return jnp.concatenate(xs, axis=0)

    def vrow(i):
        return v256_ref[i:i + 1, :]

    def load(ref, g):
        blk = ref[g * NS:(g + 1) * NS]
        return blk.reshape(C, blk.shape[-1])

    def store(ref, g, lo, val):
        w = val.shape[-1]
        ref[g * NS:(g + 1) * NS, :, lo:lo + w] = val.reshape(NS, SEG, w)

    def sq(x, q):
        return x[q * SEG:(q + 1) * SEG]

    def seg_last(x):
        if NS == 1:
            return x[C - 1:C]
        x3 = x.reshape(NS, SEG, x.shape[-1])
        return jnp.broadcast_to(x3[:, SEG - 1:SEG], x3.shape).reshape(x.shape)

    def last_row(x, q):
        return x[(q + 1) * SEG - 1:(q + 1) * SEG]

    def stack(parts):
        return parts[0] if NS == 1 else rows(*parts)

    def with_state(mm, lhs_list, ref, g, p):
        outs = [[] for _ in lhs_list]
        for q in range(NS):
            res = mm(rows(*[sq(a, q) for a in lhs_list]) if len(lhs_list) > 1 else sq(lhs_list[0], q),
                     ref[g * NS + q, p])
            for i in range(len(lhs_list)):
                outs[i].append(res[i * SEG:(i + 1) * SEG])
        return [stack(o) for o in outs]

    def update_state(ref, g, p, decay, lhs_list, rhs_list, masked):
        for q in range(NS):
            a = rows(*[sq(x, q) for x in lhs_list]) if len(lhs_list) > 1 else sq(lhs_list[0], q)
            b = rows(*[sq(x, q) for x in rhs_list]) if len(rhs_list) > 1 else sq(rhs_list[0], q)
            new = ref[g * NS + q, p] * decay(q) + _mm_tn(a, b)
            ref[g * NS + q, p] = jnp.where(bd, new, 0.0) if masked else new

    def hsum(x):
        return _mm_sel_rhs(x, blk64, n=1)

    def head(x, h):
        return jnp.where(hmask[h], x, 0.0)

    def hrows(x):
        return rows(head(x, 0), head(x, 1))

    def pair_nt(a, b):
        return _mm_nt(a, hrows(b))

    def pair_apply(q2, y):
        return _mm(q2, hrows(y))

    def bd2(q2):
        return rows(jnp.where(left2, q2, 0.0), jnp.where(left2, 0.0, q2))

    def seg2(tile, tile_t, lane0):
        col = jnp.where(left2, tile[:, lane0:lane0 + 1], tile[:, lane0 + 1:lane0 + 2])
        row = jnp.concatenate([tile_t[lane0:lane0 + 1, :], tile_t[lane0 + 1:lane0 + 2, :]], axis=1)
        return _seg_decay(col, row, incl2)

    def expand(tile, base):
        cols = [jnp.broadcast_to(tile[:, base + h:base + h + 1], (C, LANES)) for h in range(NH)]
        return [jnp.where(hmask[0], cols[0], cols[1]), jnp.where(hmask[0], cols[2], cols[3])]

    row8 = lax.broadcasted_iota(jnp.int32, (SUBLANES, 1), 0)

    def shifted(x, prev8, k):
        if NS > 1:
            x3 = x.reshape(NS, SEG, x.shape[-1])
            r3 = lax.broadcasted_iota(jnp.int32, (1, SEG, 1), 1)
            return jnp.where(r3 < k, pltpu.roll(prev8, k, 1), pltpu.roll(x3, k, 1)).reshape(x.shape)
        prev8 = prev8[0]
        top = jnp.where(row8 < k, pltpu.roll(prev8, k, 0), pltpu.roll(x, k, 0)[0:SUBLANES])
        if C == SUBLANES:
            return top
        return rows(top, pltpu.roll(x, k, 0)[SUBLANES:])

    def psl(p):
        return slice(LANES * p, LANES * (p + 1))

    shared = {}

    def neumann(n_list, steps):
        rs = list(n_list)
        ps = [_mm(p, bd2(p)) for p in n_list]
        yield None
        for _ in range(steps - 1):
            both = [_mm(rows(p, r), bd2(p)) for p, r in zip(ps, rs)]
            yield None
            rs = [r + p + b[C:2 * C] for r, p, b in zip(rs, ps, both)]
            ps = [b[0:C] for b in both]
        upd = [_mm(r, bd2(p)) for p, r in zip(ps, rs)]
        yield None
        yield [r + p + u for r, p, u in zip(rs, ps, upd)]

    def rwkv(g):
        u_rw = load(u_rw_ref, g)
        u_prev = shifted(u_rw, xp_rw[g * NS:(g + 1) * NS], 1)
        xp_rw[g * NS:(g + 1) * NS] = u_rw.reshape(NS, SEG, RWKV_COLS)[:, SEG - SUBLANES:SEG]
        xs = u_rw + (u_prev - u_rw) * mu_ref[...]
        r = xs[:, 0:GW]
        k = xs[:, GW:2 * GW]
        v = xs[:, 2 * GW:3 * GW]
        x7 = xs[:, 3 * GW:RWKV_COLS]
        yield
        w_pre = _mm(jnp.tanh(x7), lr_ref[LR_RW_W2])
        a_pre = _mm(x7, lr_ref[LR_RW_A2])
        gate = _mm(jax.nn.sigmoid(x7), lr_ref[LR_RW_G2])
        yield
        kk = k * vrow(V_RW_KK)
        kk_ss = hsum(kk * kk)
        yield
        log_w = -jax.nn.softplus(-(vrow(V_RW_W0) + w_pre)) - 0.5
        lw = -jnp.exp(log_w)
        yield
        a = jax.nn.sigmoid(vrow(V_RW_A0) + a_pre)
        yield
        kk = kk * lax.rsqrt(kk_ss + EPS)
        k2 = k * (1.0 + (a - 1.0) * vrow(V_RW_KA))
        bcum = _mm_sel_lhs(tri, lw)
        bonus = hsum(r * k2 * vrow(V_RW_RK))
        yield
        e_nb = jnp.exp(-bcum)
        blast = seg_last(bcum)
        e_rem = jnp.exp(blast - bcum)
        yield
        nka = -kk * a
        r_t = r * jnp.exp(bcum)
        yield
        a_t = kk * jnp.exp(bcum - lw)
        b_t = nka * e_nb
        yield
        k_t = k2 * e_nb
        k_rem = k2 * e_rem
        b_rem = nka * e_rem
        yield
        ars = [with_state(_mm_nt, [a_t[:, psl(p)], r_t[:, psl(p)]], s_wkv, g, p) for p in range(2)]
        ar = [_mm_nt(rows(a_t[:, psl(p)], r_t[:, psl(p)]), rows(hrows(b_t[:, psl(p)]), hrows(k_t[:, psl(p)])))
              for p in range(2)]
        yield
        n_ab = [jnp.where(strict2, ar[p][0:C, 0:2 * C], 0.0) for p in range(2)]
        n_ak = [jnp.where(strict2, ar[p][0:C, 2 * C:4 * C], 0.0) for p in range(2)]
        a_r = [jnp.where(incl4, ar[p][C:2 * C], 0.0) for p in range(2)]
        rhs = [ars[p][0] + pair_apply(n_ak[p], v[:, psl(p)]) for p in range(2)]
        tinv = None
        for step in neumann(n_ab, n_iter):
            if step is None:
                yield
            else:
                tinv = step
        yield
        uh = [rhs[p] + pair_apply(tinv[p], rhs[p]) for p in range(2)]
        yield
        y_pairs = []
        for p in range(2):
            sl = psl(p)
            vp = v[:, sl]
            u_p = uh[p]
            yp = ars[p][1] + _mm(a_r[p], rows(hrows(u_p), hrows(vp)))
            update_state(s_wkv, g, p, lambda q: jnp.exp(last_row(bcum, q)[:, sl]),
                         [vp, u_p], [k_rem[:, sl], b_rem[:, sl]], True)
            y_pairs.append(yp)
        yield
        y = jnp.concatenate(y_pairs, axis=1)
        mean = hsum(y) * (1.0 / HD)
        yield
        yc = y - mean
        var = hsum(yc * yc) * (1.0 / HD)
        yield
        y = yc * lax.rsqrt(var + RWKV_GN_EPS) * vrow(V_RW_LNW) + vrow(V_RW_LNB)
        store(y_ref, g, 0, (y + bonus * v) * gate)

    def gla(g):
        u_gla = load(u_gla_ref, g)
        q = u_gla[:, 0:GW] * (HD ** -0.5)
        k = u_gla[:, GW:2 * GW]
        v = u_gla[:, 2 * GW:3 * GW]
        gz = u_gla[:, 3 * GW:4 * GW]
        gate = _mm(load(misc_ref, g), lr_ref[LR_GLA_GK])
        yield
        la = jax.nn.log_sigmoid(gate + vrow(V_GLA_B)) * (1.0 / GLA_GATE_NORM)
        bcum = _mm_sel_lhs(tri, la)
        yield
        blast = seg_last(bcum)
        q_in = q * jnp.exp(bcum)
        yield
        k_out = k * jnp.exp(blast - bcum)
        qs = [with_state(_mm, [q_in[:, psl(p)]], s_gla, g, p)[0] for p in range(2)]
        a_blocks = [[], []]
        k_b = None
        ref_prev = None
        for lo in range(0, C, GLA_SUB):
            hi = min(lo + GLA_SUB, C)
            ref = bcum[lo - 1:lo, :] if lo else jnp.zeros((1, GW), F32)
            q_b = q[lo:hi] * jnp.exp(bcum[lo:hi] - ref)
            k_new = k[lo:hi] * jnp.exp(ref - bcum[lo:hi])
            if k_b is None:
                k_b = rows(k_new, jnp.zeros((C - hi, GW), F32)) if hi < C else k_new
            else:
                old = k_b[0:lo] * jnp.exp(ref - ref_prev)
                k_b = rows(old, k_new, jnp.zeros((C - hi, GW), F32)) if hi < C else rows(old, k_new)
            ref_prev = ref
            for p in range(2):
                a_blocks[p].append(pair_nt(q_b[:, psl(p)], k_b[:, psl(p)]))
            yield
        a_h = [rows(*a_blocks[p]) for p in range(2)]
        for p in range(2):
            sl = psl(p)
            def kdecay(q, sl=sl):
                row = jnp.exp(last_row(bcum, q)[:, sl])
                return jnp.broadcast_to(row, (SUBLANES, LANES)).T[:, 0:1]

            update_state(s_gla, g, p, kdecay, [k_out[:, sl]], [v[:, sl]], True)
        yield
        o_pairs = []
        for p in range(2):
            o_pairs.append(qs[p] + pair_apply(jnp.where(incl2, a_h[p], 0.0), v[:, psl(p)]))
        yield
        o = jnp.concatenate(o_pairs, axis=1)
        ss = hsum(o * o)
        yield
        o = o * lax.rsqrt(ss * (1.0 / HD) + EPS) * vrow(V_GLA_NW)
        store(y_ref, g, GW, o * _silu(gz))

    def dn(g):
        misc = load(misc_ref, g)
        g_dn = -jnp.exp(v128_ref[P_DN_ALOG:P_DN_ALOG + 1, :]) * jax.nn.softplus(
            misc + v128_ref[P_DN_DTB:P_DN_DTB + 1, :])
        dt_ss = jax.nn.softplus(misc + v128_ref[P_SSM_DTB:P_SSM_DTB + 1, :])
        la_ss = dt_ss * -jnp.exp(v128_ref[P_SSM_ALOG:P_SSM_ALOG + 1, :])
        beta = jax.nn.sigmoid(misc)
        cum = _mm_sel_lhs(tri, jnp.where(lane < MISC_DN_B, g_dn, la_ss))
        u_dn = load(u_dn_ref, g)
        raw = u_dn[:, 0:CONV_COLS]
        prev8 = xp_dn[g * NS:(g + 1) * NS]
        qkv = raw * conv_ref[CV_DN_W + 3:CV_DN_W + 4, :]
        for k in (1, 2, 3):
            qkv = qkv + shifted(raw, prev8, k) * conv_ref[CV_DN_W + 3 - k:CV_DN_W + 4 - k, :]
        xp_dn[g * NS:(g + 1) * NS] = raw.reshape(NS, SEG, CONV_COLS)[:, SEG - SUBLANES:SEG]
        yield
        qkv = _silu(qkv)
        q = qkv[:, 0:GW]
        k = qkv[:, GW:2 * GW]
        v = qkv[:, 2 * GW:3 * GW]
        z = u_dn[:, CONV_COLS:CONV_COLS + GW]
        q_ss = hsum(q * q)
        k_ss = hsum(k * k)
        yield
        cum_t = cum.T
        shared[g] = (cum, cum_t, dt_ss)
        q = q * lax.rsqrt(q_ss + EPS) * (HD ** -0.5)
        k = k * lax.rsqrt(k_ss + EPS)
        yield
        beta_x = expand(beta, MISC_DN_B)
        cum_x = expand(cum, MISC_DN_A)
        yield
        ecx, kb, vb, kbe, qs, kq = [], [], [], [], [], []
        for p in range(2):
            sl = psl(p)
            ecx.append(jnp.exp(cum_x[p]))
            kb.append(k[:, sl] * beta_x[p])
            vb.append(v[:, sl] * beta_x[p])
            kbe.append(kb[p] * ecx[p])
            yield
            qs.append(with_state(_mm, [q[:, sl] * ecx[p]], s_dn, g, p)[0])
            kq.append(pair_nt(rows(kb[p], q[:, sl]), k[:, sl]))
        yield
        n_m, a_l = [], []
        for p in range(2):
            dec = seg2(cum, cum_t, MISC_DN_A + 2 * p)
            n_m.append(jnp.where(strict2, -(kq[p][0:C] * dec), 0.0))
            a_l.append(kq[p][C:2 * C] * dec)
        tinv = None
        for step in neumann(n_m, n_iter - 1):
            if step is None:
                yield
            else:
                tinv = step
        yield
        err = [n_m[p] - tinv[p] + _mm3(n_m[p], bd2(tinv[p])) for p in range(2)]
        yield
        tinv = [tinv[p] + err[p] + _mm(tinv[p], bd2(err[p])) for p in range(2)]
        yield
        vk = [jnp.concatenate([vb[p], kbe[p]], axis=1) for p in range(2)]
        uw = [vk[p] + _mm(tinv[p], jnp.concatenate([hrows(vb[p]), hrows(kbe[p])], axis=1)) for p in range(2)]
        yield
        u_p = [uw[p][:, 0:LANES] for p in range(2)]
        w_p = [uw[p][:, LANES:2 * LANES] for p in range(2)]
        ws = [with_state(_mm, [w_p[p]], s_dn, g, p)[0] for p in range(2)]
        yield
        o_pairs = []
        for p in range(2):
            sl = psl(p)
            cx = cum_x[p]
            last = seg_last(cx)
            v_new = u_p[p] - ws[p]
            op = qs[p] + pair_apply(a_l[p], v_new)
            update_state(s_dn, g, p, lambda q: jnp.exp(last_row(cx, q)),
                         [k[:, sl] * jnp.exp(last - cx)], [v_new], True)
            o_pairs.append(op)
        yield
        o = jnp.concatenate(o_pairs, axis=1)
        ss = hsum(o * o)
        yield
        o = o * lax.rsqrt(ss * (1.0 / HD) + EPS) * vrow(V_DN_NW)
        store(y_ref, g, 2 * GW, o * _silu(z))

    def ssd(g):
        u_ssm = load(u_ssm_ref, g)
        z = u_ssm[:, 0:GW]
        raw = u_ssm[:, GW:GW + CONV_COLS]
        prev8 = xp_ss[g * NS:(g + 1) * NS]
        xbc = conv_ref[CV_SSM_B:CV_SSM_B + 1, :] + raw * conv_ref[CV_SSM_W + 3:CV_SSM_W + 4, :]
        for k in (1, 2, 3):
            xbc = xbc + shifted(raw, prev8, k) * conv_ref[CV_SSM_W + 3 - k:CV_SSM_W + 4 - k, :]
        xp_ss[g * NS:(g + 1) * NS] = raw.reshape(NS, SEG, CONV_COLS)[:, SEG - SUBLANES:SEG]
        yield
        xbc = _silu(xbc)
        yield
        xs_ = xbc[:, 0:GW]
        bm = xbc[:, GW:2 * GW]
        cm = xbc[:, 2 * GW:3 * GW]
        gmat = [_mm_nt(cm[:, psl(p)], rows(bm[:, psl(p)], bm[:, psl(p)])) for p in range(2)]
        cs = [with_state(_mm_nt, [cm[:, psl(p)]], s_ssm, g, p)[0] for p in range(2)]
        yield
        while g not in shared:
            yield
        cum, cum_t, dt_ss = shared[g]
        dt_x = expand(dt_ss, MISC_SSM_DT)
        cum_x = expand(cum, MISC_SSM_DT)
        yield
        y_pairs = []
        for p in range(2):
            sl = psl(p)
            xh = xs_[:, sl]
            xdt = xh * dt_x[p]
            cx = cum_x[p]
            last = seg_last(cx)
            yp = jnp.exp(cx) * cs[p] + v256_ref[V_SSM_D:V_SSM_D + 1, sl] * xh
            yp = yp + pair_apply(gmat[p] * seg2(cum, cum_t, MISC_SSM_DT + 2 * p), xdt)
            lane0 = MISC_SSM_DT + 2 * p

            def dcol(q, lane0=lane0):
                lr = last_row(cum, q)
                return rows(jnp.broadcast_to(jnp.exp(lr[:, lane0:lane0 + 1]), (HD, SSM_STATE)),
                            jnp.broadcast_to(jnp.exp(lr[:, lane0 + 1:lane0 + 2]), (HD, SSM_STATE)))

            update_state(s_ssm, g, p, dcol, [xdt * jnp.exp(last - cx)], [bm[:, sl]], False)
            y_pairs.append(yp)
            yield
        yield
        y = jnp.concatenate(y_pairs, axis=1) * _silu(z)
        ss = _mm_sel_rhs(y * y, blk128)
        yield
        store(y_ref, g, 3 * GW, y * lax.rsqrt(ss * (1.0 / (2 * HD)) + EPS) * vrow(V_SSM_NW))

    active = []
    for g in range(G):
        active += [rwkv(g), dn(g), gla(g), ssd(g)]
    while active:
        alive = []
        for gen in active:
            try:
                next(gen)
                alive.append(gen)
            except StopIteration:
                pass
        active = alive

    @pl.when(t == n_t - 1)
    def _fin():
        for j in range(n_prev):
            for i in range(7):
                out_refs[i][j] = prev_refs[7 * j + i][...]
        for g in range(G * NS):
            shift1_ref[g] = xp_rw[g]
            dnc1_ref[g] = xp_dn[g]
            ssc1_ref[g] = xp_ss[g]
            for p in range(2):
                sw = s_wkv[g, p]
                sg = s_gla[g, p]
                sd = s_dn[g, p]
                ss = s_ssm[g, p]
                for h in range(2):
                    hs = slice(HD * h, HD * (h + 1))
                    wkv1_ref[g, 2 * p + h] = sw[hs, hs]
                    gla1_ref[g, 2 * p + h] = sg[hs, hs]
                    dn1_ref[g, 2 * p + h] = sd[hs, hs]
                    ssm1_ref[g, 2 * p + h] = ss[hs, :]


def _mixer_call(u_parts, states, mparams, layer, prev=()):
    u_rw, u_gla, u_dn, u_ssm, misc = u_parts
    shift0, wkv0, gla0, dnc0, dn0, ssc0, ssm0 = states
    bsz, tlen, _ = u_rw.shape
    tok = math.gcd(tlen, PROMPT_CHUNK)
    n_t = tlen // tok
    if n_t > 1:
        NS, R = 1, SEQS_PER_STEP_LONG
    else:
        NS = max(1, min(PROMPT_CHUNK // tok, SEQS_PER_STEP_SHORT))
        R = SEQS_PER_STEP_SHORT // NS * (1 if prev else 2)
    C = NS * tok
    G = R * NS
    assert bsz % G == 0 and tok & (tok - 1) == 0 and tok % SUBLANES == 0

    def tmap(b, t):
        return (b, t, 0)

    def bmap3(b, t):
        return (b, 0, 0)

    def bmap4(b, t):
        return (b, 0, 0, 0)

    def full(arr):
        nd = arr.ndim
        return pl.BlockSpec(arr.shape, lambda b, t: (0,) * nd)

    def lmap3(b, t):
        return (layer, b, 0, 0)

    def lmap4(b, t):
        return (layer, b, 0, 0, 0)

    sq = (G, NH, HD, HD)
    in_state_specs = [pl.BlockSpec((None, G, SUBLANES, RWKV_COLS), lmap3), pl.BlockSpec((None,) + sq, lmap4),
                      pl.BlockSpec((None,) + sq, lmap4), pl.BlockSpec((None, G, SUBLANES, CONV_COLS), lmap3),
                      pl.BlockSpec((None,) + sq, lmap4), pl.BlockSpec((None, G, SUBLANES, CONV_COLS), lmap3),
                      pl.BlockSpec((None, G, NH, HD, SSM_STATE), lmap4)]
    n_prev = len(prev)
    n_out = n_prev + 1
    tails = [(SUBLANES, RWKV_COLS), (NH, HD, HD), (NH, HD, HD), (SUBLANES, CONV_COLS), (NH, HD, HD),
             (SUBLANES, CONV_COLS), (NH, HD, SSM_STATE)]
    prev_specs = [pl.BlockSpec((G,) + tl, bmap3 if len(tl) == 2 else bmap4) for tl in tails] * n_prev
    state_specs = [pl.BlockSpec((n_out, G) + tl, (lambda b, t: (0, b, 0, 0)) if len(tl) == 2
                                else (lambda b, t: (0, b, 0, 0, 0))) for tl in tails]
    state_shapes = [jax.ShapeDtypeStruct((n_out, bsz) + tl, F32) for tl in tails]
    outs = pl.pallas_call(
        functools.partial(_mixer_body, C, n_t, R, NS, n_prev),
        grid=(bsz // G, n_t),
        in_specs=[pl.BlockSpec((G, tok, RWKV_COLS), tmap), pl.BlockSpec((G, tok, 4 * GW), tmap),
                  pl.BlockSpec((G, tok, 4 * GW), tmap), pl.BlockSpec((G, tok, 4 * GW), tmap),
                  pl.BlockSpec((G, tok, LANES), tmap)] + in_state_specs + [full(a) for a in mparams] + prev_specs,
        out_specs=[pl.BlockSpec((G, tok, D_MODEL), tmap)] + state_specs,
        out_shape=[jax.ShapeDtypeStruct((bsz, tlen, D_MODEL), F32)] + state_shapes,
        scratch_shapes=[pltpu.VMEM((G, SUBLANES, RWKV_COLS), F32),
                        pltpu.VMEM((G, SUBLANES, CONV_COLS), F32),
                        pltpu.VMEM((G, SUBLANES, CONV_COLS), F32),
                        pltpu.VMEM((G, 2, LANES, LANES), F32), pltpu.VMEM((G, 2, LANES, LANES), F32),
                        pltpu.VMEM((G, 2, LANES, LANES), F32), pltpu.VMEM((G, 2, LANES, SSM_STATE), F32)],
        compiler_params=pltpu.CompilerParams(vmem_limit_bytes=VMEM_LIMIT,
                                             dimension_semantics=("parallel", "arbitrary")),
        name="mixers",
    )(u_rw, u_gla, u_dn, u_ssm, misc, shift0, wkv0, gla0, dnc0, dn0, ssc0, ssm0, *mparams,
      *[a for layer_states in prev for a in layer_states])
    return outs[0], tuple(outs[1:])


def _pad_rows(m, lo, total):
    return jnp.pad(m, ((lo, total - lo - m.shape[0]), (0, 0)))


def _lane_vec(vals, lo):
    return jnp.pad(vals, (lo, LANES - lo - vals.shape[0]))


def _pack_layer(P, l):
    v256 = jnp.stack([P['rwkv_w0'][l], P['rwkv_a0'][l], P['rwkv_k_k'][l], P['rwkv_k_a'][l],
                      P['rwkv_r_k'][l], P['rwkv_ln_w'][l], P['rwkv_ln_b'][l],
                      P['gla_gk_b'][l], P['gla_norm_w'][l], P['dn_norm_w'][l],
                      jnp.repeat(P['ssm_D'][l], HD), P['ssm_norm_w'][l]])
    v256 = jnp.pad(v256, ((0, 16 - v256.shape[0]), (0, 0)))
    v128 = jnp.stack([_lane_vec(P['dn_A_log'][l], MISC_DN_A), _lane_vec(P['dn_dt_bias'][l], MISC_DN_A),
                      _lane_vec(P['ssm_A_log'][l], MISC_SSM_DT), _lane_vec(P['ssm_dt_bias'][l], MISC_SSM_DT)])
    v128 = jnp.pad(v128, ((0, 4), (0, 0)))
    conv = jnp.concatenate([P['dn_conv_w'][l], P['ssm_conv_w'][l], P['ssm_conv_b'][l][None],
                            jnp.zeros((7, CONV_COLS), F32)], axis=0)
    lr = jnp.stack([_pad_rows(P['rwkv_w2'][l], 0, LANES), _pad_rows(P['rwkv_a2'][l], 32, LANES),
                    _pad_rows(P['rwkv_g2'][l], 64, LANES),
                    _pad_rows(P['gla_gk_w2'][l], MISC_GLA_GATE, LANES)]).astype(MXU_DT)
    return (P['rwkv_mu'][l][None], v256, v128, conv, lr)


def _pad_tail_rows(a):
    return jnp.pad(a, ((0, 0), (0, 0), (SUBLANES - a.shape[2], 0), (0, 0)))


def _trunk(x, mod_rows, states, mixer_params, dense, final_norm_w):
    n_layers = len(mixer_params)
    shift0, wkv0, gla0, dnc0, dn0, ssc0, ssm0 = states
    st_in = (_pad_tail_rows(shift0[:, :, None, :]), wkv0, gla0, _pad_tail_rows(dnc0), dn0,
             _pad_tail_rows(ssc0), ssm0)
    prev = []
    for l in range(n_layers):
        mod = mod_rows[l][:, None, :]
        u_parts = _inproj_call(x, mod, dense['norm1'][l], dense['w_r'], l)
        last = l == n_layers - 1
        y_mix, st = _mixer_call(u_parts, st_in, mixer_params[l], l, prev=tuple(prev) if last else ())
        x = _outmlp_call(x, y_mix, mod, dense['norm2'][l], final_norm_w, dense['w_out'],
                         dense['w_up'], dense['w_down'], l, final=last)
        if not last:
            prev.append(tuple(s[0] for s in st))
    shift1, wkv1, gla1, dnc1, dn1, ssc1, ssm1 = st
    return x, (shift1[:, :, SUBLANES - 1], wkv1, gla1, dnc1[:, :, SUBLANES - 3:], dn1,
               ssc1[:, :, SUBLANES - 3:], ssm1)


def kernel(x_prompt, x_sample, state_rwkv_shift, state_rwkv_wkv, state_gla, state_dn_conv,
           state_dn, state_ssm_conv, state_ssm, c_prompt, c_sample,
           ada_w, ada_b, norm1_w, norm2_w, w_in, w_out, w_up, w_down,
           rwkv_mu, rwkv_w0, rwkv_w2, rwkv_a0, rwkv_a2, rwkv_g2, rwkv_k_k, rwkv_k_a, rwkv_r_k,
           rwkv_ln_w, rwkv_ln_b, gla_gk_w2, gla_gk_b, gla_norm_w,
           dn_conv_w, dn_A_log, dn_dt_bias, dn_norm_w,
           ssm_conv_w, ssm_conv_b, ssm_dt_bias, ssm_A_log, ssm_D, ssm_norm_w, final_norm_w):
    P = dict(rwkv_mu=rwkv_mu, rwkv_w0=rwkv_w0, rwkv_w2=rwkv_w2, rwkv_a0=rwkv_a0, rwkv_a2=rwkv_a2,
             rwkv_g2=rwkv_g2, rwkv_k_k=rwkv_k_k, rwkv_k_a=rwkv_k_a, rwkv_r_k=rwkv_r_k,
             rwkv_ln_w=rwkv_ln_w, rwkv_ln_b=rwkv_ln_b, gla_gk_w2=gla_gk_w2, gla_gk_b=gla_gk_b,
             gla_norm_w=gla_norm_w, dn_conv_w=dn_conv_w, dn_A_log=dn_A_log, dn_dt_bias=dn_dt_bias,
             dn_norm_w=dn_norm_w, ssm_conv_w=ssm_conv_w, ssm_conv_b=ssm_conv_b,
             ssm_dt_bias=ssm_dt_bias, ssm_A_log=ssm_A_log, ssm_D=ssm_D, ssm_norm_w=ssm_norm_w)
    n_layers = w_in.shape[0]
    n_prompt = x_prompt.shape[0]
    mixer_params = [_pack_layer(P, l) for l in range(n_layers)]
    dense = dict(w_r=_wprep_call(w_in), w_out=w_out.astype(MXU_DT), w_up=w_up.astype(MXU_DT),
                 w_down=w_down.astype(MXU_DT), norm1=norm1_w, norm2=norm2_w)
    mod_all = _ada_call(jnp.concatenate([c_prompt, c_sample], axis=0), ada_w, ada_b)
    sample_states = (state_rwkv_shift, state_rwkv_wkv, state_gla, state_dn_conv,
                     state_dn, state_ssm_conv, state_ssm)
    prompt_states = tuple(jnp.zeros((n_layers, n_prompt) + s.shape[2:], F32) for s in sample_states)
    y_prompt, ps = _trunk(x_prompt, mod_all[:, :n_prompt], prompt_states, mixer_params, dense, final_norm_w)
    y_sample, ss = _trunk(x_sample, mod_all[:, n_prompt:], sample_states, mixer_params, dense, final_norm_w)
    return (y_prompt, y_sample) + ps + ss
```
